```python
import math
import jax, jax.numpy as jnp
from jax import lax
import numpy as np

D_MODEL = 1024
BATCH = 2
SEQ = 8192
DEPTH = 2

HEAD_DIM = 64
D_MIX = D_MODEL
N_HEADS_TOTAL = D_MIX // HEAD_DIM
N_HEADS_DIL = N_HEADS_TOTAL // 2
N_LRU_BLOCKS = N_HEADS_TOTAL // 4
N_HEADS_SB = N_HEADS_TOTAL // 4
W_DIL = N_HEADS_DIL * HEAD_DIM
W_LRU = N_LRU_BLOCKS * HEAD_DIM
W_SB = N_HEADS_SB * HEAD_DIM
D_IN = 4 * W_DIL + 2 * W_LRU + 4 * W_SB
ROPE_DIM = HEAD_DIM // 4
ROPE_THETA = 500000.0
DILATED_PATTERNS = ((128, 1), (512, 4), (2048, 16))
CONV_WIDTH = 4
LRU_C = 8.0
Q_BLOCK = 128
EPS = 1e-6

kernel_name = 'hybrid_dilated_rglru_stickbreaking'


def _in_split_points():
    sizes = [W_DIL] * 4 + [W_LRU] * 2 + [W_SB] * 4
    points, acc = [], 0
    for s in sizes[:-1]:
        acc += s
        points.append(acc)
    return points


def rms_norm(x, gain):
    xf = x.astype(jnp.float32)
    xf = xf * lax.rsqrt(jnp.mean(xf * xf, axis=-1, keepdims=True) + EPS)
    return (xf * gain.astype(jnp.float32)).astype(x.dtype)


def rope_tables(seq_len):
    pos = jnp.arange(seq_len, dtype=jnp.float32)
    inv_freq = ROPE_THETA ** (-jnp.arange(0, ROPE_DIM, 2, dtype=jnp.float32) / ROPE_DIM)
    ang = pos[:, None] * inv_freq[None, :]
    return jnp.cos(ang), jnp.sin(ang)


def apply_partial_rope(x, cos, sin):
    half = ROPE_DIM // 2
    xf = x.astype(jnp.float32)
    x1 = xf[..., :half]
    x2 = xf[..., half:ROPE_DIM]
    c = cos[None, :, None, :]
    s = sin[None, :, None, :]
    out = jnp.concatenate([x1 * c - x2 * s, x2 * c + x1 * s, xf[..., ROPE_DIM:]], axis=-1)
    return out.astype(x.dtype)


def dilated_attention(q, k, v):
    b, h, seq, _ = q.shape
    scale = 1.0 / math.sqrt(HEAD_DIM)
    offs = jnp.arange(Q_BLOCK)

    def block(blk):
        t = blk * Q_BLOCK + offs
        q_blk = lax.dynamic_slice_in_dim(q, blk * Q_BLOCK, Q_BLOCK, axis=2).astype(jnp.float32) * scale
        lses, outs = [], []
        for window, dilation in DILATED_PATTERNS:
            n_keys = window // dilation + 1
            idx = t[:, None] - dilation * jnp.arange(n_keys)[None, :]
            valid = idx >= 0
            idx = jnp.maximum(idx, 0)
            k_g = jnp.take(k, idx, axis=2).astype(jnp.float32)
            v_g = jnp.take(v, idx, axis=2).astype(jnp.float32)
            s = jnp.einsum('bhqd,bhqjd->bhqj', q_blk, k_g)
            s = jnp.where(valid[None, None], s, -jnp.inf)
            lse = jax.nn.logsumexp(s, axis=-1)
            p = jnp.exp(s - lse[..., None])
            outs.append(jnp.einsum('bhqj,bhqjd->bhqd', p, v_g))
            lses.append(lse)
        mix = jax.nn.softmax(jnp.stack(lses), axis=0)
        return jnp.sum(mix[..., None] * jnp.stack(outs), axis=0)

    out = lax.map(block, jnp.arange(seq // Q_BLOCK))
    return out.transpose(1, 2, 0, 3, 4).reshape(b, h, seq, HEAD_DIM).astype(q.dtype)


def stick_breaking_attention(q, k, v):
    b, h, seq, _ = q.shape
    scale = 1.0 / math.sqrt(HEAD_DIM)
    offs = jnp.arange(Q_BLOCK)
    key_pos = jnp.arange(seq)
    kf = k.astype(jnp.float32)
    vf = v.astype(jnp.float32)

    def block(blk):
        t = blk * Q_BLOCK + offs
        q_blk = lax.dynamic_slice_in_dim(q, blk * Q_BLOCK, Q_BLOCK, axis=2).astype(jnp.float32)
        z = jnp.einsum('bhqd,bhkd->bhqk', q_blk, kf) * scale
        causal = key_pos[None, :] < t[:, None]
        log_keep = jnp.where(causal, jax.nn.log_sigmoid(-z), 0.0)
        tail = lax.cumsum(log_keep, axis=3, reverse=True)
        between = jnp.concatenate([tail[..., 1:], jnp.zeros_like(tail[..., :1])], axis=-1)
        weights = jnp.where(causal, jnp.exp(jax.nn.log_sigmoid(z) + between), 0.0)
        return jnp.einsum('bhqk,bhkd->bhqd', weights, vf)

    out = lax.map(block, jnp.arange(seq // Q_BLOCK))
    return out.transpose(1, 2, 0, 3, 4).reshape(b, h, seq, HEAD_DIM).astype(q.dtype)


def _linear_recurrence_combine(left, right):
    a1, b1 = left
    a2, b2 = right
    return a1 * a2, a2 * b1 + b2


def rg_lru_branch(x, conv_w, conv_b, gate_a_w, gate_a_b, gate_x_w, gate_x_b, lru_lambda):
    b, s, w = x.shape
    xc = lax.conv_general_dilated(
        x, conv_w[:, None, :].astype(x.dtype), window_strides=(1,), padding=[(CONV_WIDTH - 1, 0)],
        dimension_numbers=('NWC', 'WIO', 'NWC'), feature_group_count=w) + conv_b
    xg = xc.reshape(b, s, N_LRU_BLOCKS, HEAD_DIM)
    r = jax.nn.sigmoid(jnp.einsum('bsnd,nde->bsne', xg, gate_a_w) + gate_a_b).reshape(b, s, w)
    i = jax.nn.sigmoid(jnp.einsum('bsnd,nde->bsne', xg, gate_x_w) + gate_x_b).reshape(b, s, w)
    log_a = LRU_C * r.astype(jnp.float32) * jax.nn.log_sigmoid(lru_lambda.astype(jnp.float32))
    a = jnp.exp(log_a)
    u = jnp.sqrt(-jnp.expm1(2.0 * log_a)) * (i * xc).astype(jnp.float32)
    _, h = lax.associative_scan(_linear_recurrence_combine, (a, u), axis=1)
    return h.astype(x.dtype)


def hybrid_layer(x, cos, sin, norm_gain, w_in, conv_w, conv_b, gate_a_w, gate_a_b,
                 gate_x_w, gate_x_b, lru_lambda, w_out):
    b, s, _ = x.shape
    hn = rms_norm(x, norm_gain)
    proj = jnp.einsum('bsd,de->bse', hn, w_in)
    a_q, a_k, a_v, a_g, b_x, b_g, c_q, c_k, c_v, c_g = jnp.split(proj, _in_split_points(), axis=-1)

    qa = apply_partial_rope(a_q.reshape(b, s, N_HEADS_DIL, HEAD_DIM), cos, sin).transpose(0, 2, 1, 3)
    ka = apply_partial_rope(a_k.reshape(b, s, N_HEADS_DIL, HEAD_DIM), cos, sin).transpose(0, 2, 1, 3)
    va = a_v.reshape(b, s, N_HEADS_DIL, HEAD_DIM).transpose(0, 2, 1, 3)
    y_a = dilated_attention(qa, ka, va).transpose(0, 2, 1, 3).reshape(b, s, W_DIL) * jax.nn.silu(a_g)

    y_b = rg_lru_branch(b_x, conv_w, conv_b, gate_a_w, gate_a_b, gate_x_w, gate_x_b, lru_lambda) * jax.nn.silu(b_g)

    qc = c_q.reshape(b, s, N_HEADS_SB, HEAD_DIM).transpose(0, 2, 1, 3)
    kc = c_k.reshape(b, s, N_HEADS_SB, HEAD_DIM).transpose(0, 2, 1, 3)
    vc = c_v.reshape(b, s, N_HEADS_SB, HEAD_DIM).transpose(0, 2, 1, 3)
    y_c = stick_breaking_attention(qc, kc, vc).transpose(0, 2, 1, 3).reshape(b, s, W_SB) * jax.nn.silu(c_g)

    y = jnp.concatenate([y_a, y_b, y_c], axis=-1)
    return x + jnp.einsum('bse,ed->bsd', y, w_out)


def setup_inputs(seed: int = 0) -> dict:
    key = jax.random.key(seed)
    ks = jax.random.split(key, 13)
    f32 = jnp.float32
    x = jax.random.normal(ks[0], (BATCH, SEQ, D_MODEL), f32)
    norm_gain = 1.0 + 0.02 * jax.random.normal(ks[1], (DEPTH, D_MODEL), f32)
    w_in = jax.random.normal(ks[2], (DEPTH, D_MODEL, D_IN), f32) * D_MODEL ** -0.5
    conv_w = jax.random.normal(ks[3], (DEPTH, CONV_WIDTH, W_LRU), f32) * CONV_WIDTH ** -0.5
    conv_b = 0.01 * jax.random.normal(ks[4], (DEPTH, W_LRU), f32)
    gate_a_w = jax.random.normal(ks[5], (DEPTH, N_LRU_BLOCKS, HEAD_DIM, HEAD_DIM), f32) * HEAD_DIM ** -0.5
    gate_a_b = 0.01 * jax.random.normal(ks[6], (DEPTH, N_LRU_BLOCKS, HEAD_DIM), f32)
    gate_x_w = jax.random.normal(ks[7], (DEPTH, N_LRU_BLOCKS, HEAD_DIM, HEAD_DIM), f32) * HEAD_DIM ** -0.5
    gate_x_b = 0.01 * jax.random.normal(ks[8], (DEPTH, N_LRU_BLOCKS, HEAD_DIM), f32)
    u = jax.random.uniform(ks[9], (DEPTH, W_LRU), f32, minval=0.9, maxval=0.999)
    a0 = u ** (1.0 / LRU_C)
    lru_lambda = jnp.log(a0) - jnp.log1p(-a0)
    w_out = jax.random.normal(ks[10], (DEPTH, D_MIX, D_MODEL), f32) * D_MIX ** -0.5
    final_gain = 1.0 + 0.02 * jax.random.normal(ks[11], (D_MODEL,), f32)
    return {'x': x, 'norm_gain': norm_gain, 'w_in': w_in, 'conv_w': conv_w, 'conv_b': conv_b,
            'gate_a_w': gate_a_w, 'gate_a_b': gate_a_b, 'gate_x_w': gate_x_w, 'gate_x_b': gate_x_b,
            'lru_lambda': lru_lambda, 'w_out': w_out, 'final_gain': final_gain}


def reference(x, norm_gain, w_in, conv_w, conv_b, gate_a_w, gate_a_b, gate_x_w, gate_x_b,
              lru_lambda, w_out, final_gain):
    cos, sin = rope_tables(x.shape[1])
    h = x
    for l in range(DEPTH):
        h = hybrid_layer(h, cos, sin, norm_gain[l], w_in[l], conv_w[l], conv_b[l], gate_a_w[l],
                         gate_a_b[l], gate_x_w[l], gate_x_b[l], lru_lambda[l], w_out[l])
    return rms_norm(h, final_gain)
```

```python
import functools
import math

import jax
import jax.numpy as jnp
import numpy as np
from jax import lax
from jax.experimental import pallas as pl
from jax.experimental.pallas import tpu as pltpu

F32 = jnp.float32
BF16 = jnp.bfloat16

D_MODEL = 1024
HEAD_DIM = 64
W_DIL = 512
W_LRU = 256
W_SB = 256
D_IN = 4 * W_DIL + 2 * W_LRU + 4 * W_SB
ROPE_DIM = 16
ROPE_THETA = 500000.0
DILATED_PATTERNS = ((128, 1), (512, 4), (2048, 16))
LRU_C = 8.0
EPS = 1e-6

LANES = 128
QB = 128
NEG_BIG = -1e30
SB_DEAD_LOG = -104.0

VMEM_LIMIT = 56 * 1024 * 1024


def _cparams(sem):
    return pltpu.CompilerParams(dimension_semantics=sem, vmem_limit_bytes=VMEM_LIMIT)


def _inproj_kernel(x_ref, gain_ref, w_ref, cos_ref, s1_ref, s2_ref,
                   qa_ref, ka_ref, va_ref, ga_ref, bx_ref, bg_ref,
                   qc_ref, kc_ref, vc_ref, gc_ref):
    x = x_ref[...]
    ms = jnp.mean(x * x, axis=-1, keepdims=True)
    xn = ((x * lax.rsqrt(ms + EPS)) * gain_ref[...]).astype(BF16)

    def proj(lo, width):
        return jnp.dot(xn, w_ref[:, lo:lo + width], preferred_element_type=F32)

    cos = cos_ref[...]
    s1 = s1_ref[...]
    s2 = s2_ref[...]

    def rope(t):
        parts = []
        for c in range(t.shape[1] // LANES):
            xc = t[:, c * LANES:(c + 1) * LANES]
            parts.append(xc * cos + pltpu.roll(xc, LANES - ROPE_DIM // 2, 1) * s1
                         + pltpu.roll(xc, ROPE_DIM // 2, 1) * s2)
        return jnp.concatenate(parts, axis=1)

    def silu(t):
        return t * jax.nn.sigmoid(t)

    scale = 1.0 / math.sqrt(HEAD_DIM)
    o = 0
    qa_ref[...] = (rope(proj(o, W_DIL)) * scale).astype(BF16); o += W_DIL
    ka_ref[...] = rope(proj(o, W_DIL)).astype(BF16); o += W_DIL
    va_ref[...] = proj(o, W_DIL).astype(BF16); o += W_DIL
    ga_ref[...] = silu(proj(o, W_DIL)); o += W_DIL
    bx_ref[...] = proj(o, W_LRU); o += W_LRU
    bg_ref[...] = silu(proj(o, W_LRU)); o += W_LRU
    qc_ref[...] = (proj(o, W_SB) * scale).astype(BF16); o += W_SB
    kc_ref[...] = proj(o, W_SB).astype(BF16); o += W_SB
    vc_ref[...] = proj(o, W_SB).astype(BF16); o += W_SB
    gc_ref[...] = silu(proj(o, W_SB))


def _in_projection(h, gain, w_bf16, cos_t, s1_t, s2_t, seq, tm):
    n = h.shape[0]
    blocks_per_seq = seq // tm
    row = lambda i: (i, 0)
    tab = lambda i: (i % blocks_per_seq, 0)
    const = lambda i: (0, 0)
    widths = [(W_DIL, BF16), (W_DIL, BF16), (W_DIL, BF16), (W_DIL, F32), (W_LRU, F32),
              (W_LRU, F32), (W_SB, BF16), (W_SB, BF16), (W_SB, BF16), (W_SB, F32)]
    return pl.pallas_call(
        _inproj_kernel,
        grid=(n // tm,),
        in_specs=[pl.BlockSpec((tm, D_MODEL), row),
                  pl.BlockSpec((1, D_MODEL), const),
                  pl.BlockSpec((D_MODEL, D_IN), const),
                  pl.BlockSpec((tm, LANES), tab),
                  pl.BlockSpec((tm, LANES), tab),
                  pl.BlockSpec((tm, LANES), tab)],
        out_specs=[pl.BlockSpec((tm, w), row) for w, _ in widths],
        out_shape=[jax.ShapeDtypeStruct((n, w), dt) for w, dt in widths],
        compiler_params=_cparams(("parallel",)),
        name="in_projection",
    )(h, gain, w_bf16, cos_t, s1_t, s2_t)


def _dilated_bias_table():
    tq = np.arange(QB)[:, None]
    sk = np.arange(QB)[None, :]
    tables = []
    for k in (0, 1, 2, 4, 5, 16):
        delta = QB * k + tq - sk
        mult = np.zeros((QB, QB), np.float64)
        for window, dilation in DILATED_PATTERNS:
            mult += (delta >= 0) & (delta <= window) & (delta % dilation == 0)
        tables.append(np.where(mult > 0, np.log(np.maximum(mult, 1.0)), NEG_BIG))
    return np.stack(tables).astype(np.float32)


def _head_masks():
    lane = lax.broadcasted_iota(jnp.int32, (QB, LANES), 1)
    return lane < HEAD_DIM


def _dilated_kernel(q_ref, k_ref, v_ref, g_ref, bias_ref, o_ref):
    qi = pl.program_id(2)
    first_head = _head_masks()
    q = q_ref[...].astype(F32)
    q_heads = (jnp.where(first_head, q, 0.0).astype(BF16), jnp.where(first_head, 0.0, q).astype(BF16))
    n_far = DILATED_PATTERNS[-1][0] // QB

    def body(delta, carry):
        kb = qi - delta
        start = pl.multiple_of(kb * QB, QB)
        kblk = k_ref[pl.ds(start, QB), :]
        vblk = v_ref[pl.ds(start, QB), :]
        cls = jnp.where(delta == 0, 0, jnp.where(delta == 1, 1, jnp.where(delta < 4, 2,
              jnp.where(delta == 4, 3, jnp.where(delta < n_far, 4, 5)))))
        bias = bias_ref[cls]
        new = []
        for h in range(2):
            m, l, acc = carry[h]
            s = lax.dot_general(q_heads[h], kblk, (((1,), (1,)), ((), ())),
                                preferred_element_type=F32) + bias
            m_new = jnp.maximum(m, jnp.max(s, axis=-1, keepdims=True))
            alpha = jnp.exp(m - m_new)
            p = jnp.exp(s - m_new)
            l_new = alpha * l + jnp.sum(p, axis=-1, keepdims=True)
            acc_new = alpha * acc + jnp.dot(p.astype(BF16), vblk, preferred_element_type=F32)
            new.append((m_new, l_new, acc_new))
        return tuple(new)

    init = tuple((jnp.full((QB, 1), NEG_BIG, F32), jnp.zeros((QB, 1), F32),
                  jnp.zeros((QB, LANES), F32)) for _ in range(2))
    (_, l0, acc0), (_, l1, acc1) = lax.fori_loop(0, jnp.minimum(qi, n_far) + 1, body, init)
    out = jnp.where(first_head, acc0 / l0, acc1 / l1)
    o_ref[...] = (out * g_ref[...]).astype(BF16)


def _dilated_attention(qa, ka, va, ga, bias, batch, seq):
    n = qa.shape[0]
    pairs = W_DIL // LANES
    nq = seq // QB
    qmap = lambda b, p, i: (b * nq + i, p)
    kvmap = lambda b, p, i: (b, p)
    return pl.pallas_call(
        _dilated_kernel,
        grid=(batch, pairs, nq),
        in_specs=[pl.BlockSpec((QB, LANES), qmap),
                  pl.BlockSpec((seq, LANES), kvmap),
                  pl.BlockSpec((seq, LANES), kvmap),
                  pl.BlockSpec((QB, LANES), qmap),
                  pl.BlockSpec(bias.shape, lambda b, p, i: (0, 0, 0))],
        out_specs=pl.BlockSpec((QB, LANES), qmap),
        out_shape=jax.ShapeDtypeStruct((n, W_DIL), BF16),
        compiler_params=_cparams(("parallel", "parallel", "arbitrary")),
        name="dilated_attention",
    )(qa, ka, va, ga, bias)


def _suffix_count_matrix():
    j = np.arange(QB)[:, None]
    s = np.arange(QB)[None, :]
    return np.concatenate([(j > s), np.ones((QB, QB), bool)], axis=1).astype(np.float32)


def _log_keep_and_beta(z):
    sp = jnp.maximum(z, 0.0) + jnp.log1p(jnp.exp(-jnp.abs(z)))
    return -sp, z - sp


def _stick_kernel(q_ref, k_ref, v_ref, g_ref, u_ref, o_ref):
    qi = pl.program_id(2)
    first_head = _head_masks()
    q = q_ref[...].astype(F32)
    q_heads = (jnp.where(first_head, q, 0.0).astype(BF16), jnp.where(first_head, 0.0, q).astype(BF16))
    u = u_ref[...]
    row = lax.broadcasted_iota(jnp.int32, (QB, QB), 0)
    col = lax.broadcasted_iota(jnp.int32, (QB, QB), 1)
    strictly_causal = col < row

    def tile(kb, state, diagonal):
        start = pl.multiple_of(kb * QB, QB)
        kblk = k_ref[pl.ds(start, QB), :]
        vblk = v_ref[pl.ds(start, QB), :]
        new = []
        for h in range(2):
            passed, acc = state[h]
            z = lax.dot_general(q_heads[h], kblk, (((1,), (1,)), ((), ())), preferred_element_type=F32)
            log_keep, log_beta = _log_keep_and_beta(z)
            if diagonal:
                log_keep = jnp.where(strictly_causal, log_keep, 0.0)
            hi = log_keep.astype(BF16)
            lo = (log_keep - hi.astype(F32)).astype(BF16)
            sums = (jnp.dot(hi, u, preferred_element_type=F32) + jnp.dot(lo, u, preferred_element_type=F32))
            between = passed + sums[:, :QB]
            w = jnp.exp(log_beta + between)
            if diagonal:
                w = jnp.where(strictly_causal, w, 0.0)
            acc = acc + jnp.dot(w.astype(BF16), vblk, preferred_element_type=F32)
            new.append((passed + sums[:, QB:], acc))
        return tuple(new)

    zeros = jnp.zeros((QB, LANES), F32)
    state = tile(qi, ((zeros, zeros), (zeros, zeros)), diagonal=True)

    def alive(state):
        return jnp.max(jnp.maximum(state[0][0], state[1][0])) > SB_DEAD_LOG

    def cond(c):
        kb, live, _ = c
        return jnp.logical_and(kb >= 0, live)

    def body(c):
        kb, _, state = c
        state = tile(kb, state, diagonal=False)
        return kb - 1, alive(state), state

    _, _, state = lax.while_loop(cond, body, (qi - 1, alive(state), state))
    out = jnp.where(first_head, state[0][1], state[1][1])
    o_ref[...] = (out * g_ref[...]).astype(BF16)


def _stick_breaking_attention(qc, kc, vc, gc, u, batch, seq):
    n = qc.shape[0]
    pairs = W_SB // LANES
    nq = seq // QB
    qmap = lambda b, p, i: (b * nq + i, p)
    kvmap = lambda b, p, i: (b, p)
    return pl.pallas_call(
        _stick_kernel,
        grid=(batch, pairs, nq),
        in_specs=[pl.BlockSpec((QB, LANES), qmap),
                  pl.BlockSpec((seq, LANES), kvmap),
                  pl.BlockSpec((seq, LANES), kvmap),
                  pl.BlockSpec((QB, LANES), qmap),
                  pl.BlockSpec(u.shape, lambda b, p, i: (0, 0))],
        out_specs=pl.BlockSpec((QB, LANES), qmap),
        out_shape=jax.ShapeDtypeStruct((n, W_SB), BF16),
        compiler_params=_cparams(("parallel", "parallel", "arbitrary")),
        name="stick_breaking_attention",
    )(qc, kc, vc, gc, u)


SUBLANES = 8
CONV_WIDTH = 4


def _rglru_kernel(x_ref, g_ref, cw_ref, cb_ref, wa_ref, ba_ref, wx_ref, bxg_ref, lam_ref, o_ref,
                  xbuf, a_sc, u_sc, h_sc, hcar):
    tc = x_ref.shape[0]

    @pl.when(pl.program_id(1) == 0)
    def _():
        xbuf[0:SUBLANES, :] = jnp.zeros((SUBLANES, W_LRU), F32)
        hcar[...] = jnp.zeros_like(hcar)

    x = x_ref[...]
    xbuf[SUBLANES:SUBLANES + tc, :] = x
    xc = x * cw_ref[CONV_WIDTH - 1:CONV_WIDTH, :] + cb_ref[...]
    for back in range(1, CONV_WIDTH):
        xc = xc + xbuf[SUBLANES - back:SUBLANES - back + tc, :] * cw_ref[CONV_WIDTH - 1 - back:CONV_WIDTH - back, :]
    xbuf[0:SUBLANES, :] = xbuf[tc:tc + SUBLANES, :]

    xcb = xc.astype(BF16)
    r = jax.nn.sigmoid(jnp.dot(xcb, wa_ref[...], preferred_element_type=F32) + ba_ref[...])
    i = jax.nn.sigmoid(jnp.dot(xcb, wx_ref[...], preferred_element_type=F32) + bxg_ref[...])
    lam = lam_ref[...]
    log_sig_lam = -(jnp.maximum(-lam, 0.0) + jnp.log1p(jnp.exp(-jnp.abs(lam))))
    log_a = LRU_C * r * log_sig_lam
    a = jnp.exp(log_a)
    a_sc[...] = a
    u_sc[...] = jnp.sqrt(jnp.tanh(-log_a) * (a * a + 1.0)) * (i * xc)

    rows = lax.broadcasted_iota(jnp.int32, (SUBLANES, W_LRU), 0)

    def step(j, h_prev):
        start = pl.multiple_of(j * SUBLANES, SUBLANES)
        a = a_sc[pl.ds(start, SUBLANES), :]
        b = u_sc[pl.ds(start, SUBLANES), :]
        for shift in (1, 2, 4):
            keep = rows >= shift
            a_prev = jnp.where(keep, pltpu.roll(a, shift, 0), 1.0)
            b_prev = jnp.where(keep, pltpu.roll(b, shift, 0), 0.0)
            b = a * b_prev + b
            a = a * a_prev
        h = a * h_prev + b
        h_sc[pl.ds(start, SUBLANES), :] = h
        return jnp.broadcast_to(h[SUBLANES - 1:SUBLANES, :], (SUBLANES, W_LRU))

    hcar[...] = lax.fori_loop(0, tc // SUBLANES, step, hcar[...])
    o_ref[...] = (h_sc[...] * g_ref[...]).astype(BF16)


def _rg_lru(bx, bg, cw, cb, wa, ba, wx, bxg, lam, batch, seq, tc):
    n = bx.shape[0]
    nchunks = seq // tc
    row = lambda b, c: (b * nchunks + c, 0)
    const = lambda b, c: (0, 0)
    return pl.pallas_call(
        _rglru_kernel,
        grid=(batch, nchunks),
        in_specs=[pl.BlockSpec((tc, W_LRU), row),
                  pl.BlockSpec((tc, W_LRU), row),
                  pl.BlockSpec((CONV_WIDTH, W_LRU), const),
                  pl.BlockSpec((1, W_LRU), const),
                  pl.BlockSpec((W_LRU, W_LRU), const),
                  pl.BlockSpec((1, W_LRU), const),
                  pl.BlockSpec((W_LRU, W_LRU), const),
                  pl.BlockSpec((1, W_LRU), const),
                  pl.BlockSpec((1, W_LRU), const)],
        out_specs=pl.BlockSpec((tc, W_LRU), row),
        out_shape=jax.ShapeDtypeStruct((n, W_LRU), BF16),
        scratch_shapes=[pltpu.VMEM((tc + SUBLANES, W_LRU), F32),
                        pltpu.VMEM((tc, W_LRU), F32),
                        pltpu.VMEM((tc, W_LRU), F32),
                        pltpu.VMEM((tc, W_LRU), F32),
                        pltpu.VMEM((SUBLANES, W_LRU), F32)],
        compiler_params=_cparams(("arbitrary", "arbitrary")),
        name="rg_lru",
    )(bx, bg, cw, cb, wa, ba, wx, bxg, lam)


def _outproj_kernel(x_ref, ya_ref, yb_ref, yc_ref, w_ref, fg_ref, o_ref, *, final_norm):
    acc = x_ref[...]
    acc = acc + jnp.dot(ya_ref[...], w_ref[0:W_DIL, :], preferred_element_type=F32)
    acc = acc + jnp.dot(yb_ref[...], w_ref[W_DIL:W_DIL + W_LRU, :], preferred_element_type=F32)
    acc = acc + jnp.dot(yc_ref[...], w_ref[W_DIL + W_LRU:, :], preferred_element_type=F32)
    if final_norm:
        ms = jnp.mean(acc * acc, axis=-1, keepdims=True)
        acc = (acc * lax.rsqrt(ms + EPS)) * fg_ref[...]
    o_ref[...] = acc


def _out_projection(h, ya, yb, yc, w_bf16, final_gain, final_norm, tm):
    n = h.shape[0]
    row = lambda i: (i, 0)
    const = lambda i: (0, 0)
    return pl.pallas_call(
        functools.partial(_outproj_kernel, final_norm=final_norm),
        grid=(n // tm,),
        in_specs=[pl.BlockSpec((tm, D_MODEL), row),
                  pl.BlockSpec((tm, W_DIL), row),
                  pl.BlockSpec((tm, W_LRU), row),
                  pl.BlockSpec((tm, W_SB), row),
                  pl.BlockSpec((D_MODEL, D_MODEL), const),
                  pl.BlockSpec((1, D_MODEL), const)],
        out_specs=pl.BlockSpec((tm, D_MODEL), row),
        out_shape=jax.ShapeDtypeStruct((n, D_MODEL), F32),
        compiler_params=_cparams(("parallel",)),
        name="out_projection",
    )(h, ya, yb, yc, w_bf16, final_gain)


def _rope_tables(seq):
    pos = jnp.arange(seq, dtype=F32)
    inv_freq = ROPE_THETA ** (-jnp.arange(0, ROPE_DIM, 2, dtype=F32) / ROPE_DIM)
    ang = pos[:, None] * inv_freq[None, :]
    cos, sin = jnp.cos(ang), jnp.sin(ang)
    half = ROPE_DIM // 2
    lane = np.arange(LANES) % HEAD_DIM
    freq = lane % half
    cos_l, sin_l = cos[:, freq], sin[:, freq]
    cos_t = jnp.where(lane < ROPE_DIM, cos_l, 1.0)
    s1_t = jnp.where(lane < half, -sin_l, 0.0)
    s2_t = jnp.where((lane >= half) & (lane < ROPE_DIM), sin_l, 0.0)
    return cos_t, s1_t, s2_t


def _block_diagonal(w):
    nb, d, _ = w.shape
    out = jnp.zeros((nb * d, nb * d), w.dtype)
    for i in range(nb):
        out = lax.dynamic_update_slice(out, w[i], (i * d, i * d))
    return out


def kernel(x, norm_gain, w_in, conv_w, conv_b, gate_a_w, gate_a_b, gate_x_w, gate_x_b, lru_lambda, w_out, final_gain):
    batch, seq, d = x.shape
    depth = w_in.shape[0]
    n = batch * seq
    cos_t, s1_t, s2_t = _rope_tables(seq)
    bias = jnp.asarray(_dilated_bias_table())
    u = jnp.asarray(_suffix_count_matrix(), dtype=BF16)
    h = x.reshape(n, d)
    for l in range(depth):
        qa, ka, va, ga, bx, bg, qc, kc, vc, gc = _in_projection(
            h, norm_gain[l].reshape(1, d), w_in[l].astype(BF16), cos_t, s1_t, s2_t, seq, tm=256)
        ya = _dilated_attention(qa, ka, va, ga, bias, batch, seq)
        yb = _rg_lru(bx, bg, conv_w[l], conv_b[l].reshape(1, W_LRU),
                     _block_diagonal(gate_a_w[l]).astype(BF16), gate_a_b[l].reshape(1, W_LRU),
                     _block_diagonal(gate_x_w[l]).astype(BF16), gate_x_b[l].reshape(1, W_LRU),
                     lru_lambda[l].reshape(1, W_LRU), batch, seq, tc=512)
        yc = _stick_breaking_attention(qc, kc, vc, gc, u, batch, seq)
        h = _out_projection(h, ya, yb, yc, w_out[l].astype(BF16), final_gain.reshape(1, d),
                            final_norm=(l == depth - 1), tm=256)
    return h.reshape(batch, seq, d)
```

```python
import functools
import math

import jax
import jax.numpy as jnp
import numpy as np
from jax import lax
from jax.experimental import pallas as pl
from jax.experimental.pallas import tpu as pltpu

F32 = jnp.float32
BF16 = jnp.bfloat16

D_MODEL = 1024
HEAD_DIM = 64
W_DIL = 512
W_LRU = 256
W_SB = 256
D_IN = 4 * W_DIL + 2 * W_LRU + 4 * W_SB
ROPE_DIM = 16
ROPE_THETA = 500000.0
DILATED_PATTERNS = ((128, 1), (512, 4), (2048, 16))
LRU_C = 8.0
EPS = 1e-6

LANES = 128
QB = 128
NEG_BIG = -1e30
SB_DEAD_LOG = -104.0

VMEM_LIMIT = 56 * 1024 * 1024


def _cparams(sem):
    return pltpu.CompilerParams(dimension_semantics=sem, vmem_limit_bytes=VMEM_LIMIT)


def _inproj_kernel(x_ref, gain_ref, w_ref, cos_ref, s1_ref, s2_ref,
                   qa_ref, ka_ref, va_ref, ga_ref, bx_ref, bg_ref,
                   qc_ref, kc_ref, vc_ref, gc_ref):
    x = x_ref[...]
    ms = jnp.mean(x * x, axis=-1, keepdims=True)
    xn = ((x * lax.rsqrt(ms + EPS)) * gain_ref[...]).astype(BF16)

    def proj(lo, width):
        return jnp.dot(xn, w_ref[:, lo:lo + width], preferred_element_type=F32)

    cos = cos_ref[...]
    s1 = s1_ref[...]
    s2 = s2_ref[...]

    def rope(t):
        parts = []
        for c in range(t.shape[1] // LANES):
            xc = t[:, c * LANES:(c + 1) * LANES]
            parts.append(xc * cos + pltpu.roll(xc, LANES - ROPE_DIM // 2, 1) * s1
                         + pltpu.roll(xc, ROPE_DIM // 2, 1) * s2)
        return jnp.concatenate(parts, axis=1)

    def silu(t):
        return t * jax.nn.sigmoid(t)

    scale = 1.0 / math.sqrt(HEAD_DIM)
    o = 0
    qa_ref[...] = rope(proj(o, W_DIL)) * scale; o += W_DIL
    ka_ref[...] = rope(proj(o, W_DIL)); o += W_DIL
    va_ref[...] = proj(o, W_DIL); o += W_DIL
    ga_ref[...] = silu(proj(o, W_DIL)); o += W_DIL
    bx_ref[...] = proj(o, W_LRU); o += W_LRU
    bg_ref[...] = silu(proj(o, W_LRU)); o += W_LRU
    qc_ref[...] = (proj(o, W_SB) * scale).astype(BF16); o += W_SB
    kc_ref[...] = proj(o, W_SB).astype(BF16); o += W_SB
    vc_ref[...] = proj(o, W_SB).astype(BF16); o += W_SB
    gc_ref[...] = silu(proj(o, W_SB))


def _in_projection(h, gain, w_bf16, cos_t, s1_t, s2_t, seq, tm):
    n = h.shape[0]
    blocks_per_seq = seq // tm
    row = lambda i: (i, 0)
    tab = lambda i: (i % blocks_per_seq, 0)
    const = lambda i: (0, 0)
    widths = [(W_DIL, F32), (W_DIL, F32), (W_DIL, F32), (W_DIL, F32), (W_LRU, F32),
              (W_LRU, F32), (W_SB, BF16), (W_SB, BF16), (W_SB, BF16), (W_SB, F32)]
    return pl.pallas_call(
        _inproj_kernel,
        grid=(n // tm,),
        in_specs=[pl.BlockSpec((tm, D_MODEL), row),
                  pl.BlockSpec((1, D_MODEL), const),
                  pl.BlockSpec((D_MODEL, D_IN), const),
                  pl.BlockSpec((tm, LANES), tab),
                  pl.BlockSpec((tm, LANES), tab),
                  pl.BlockSpec((tm, LANES), tab)],
        out_specs=[pl.BlockSpec((tm, w), row) for w, _ in widths],
        out_shape=[jax.ShapeDtypeStruct((n, w), dt) for w, dt in widths],
        compiler_params=_cparams(("parallel",)),
        name="in_projection",
    )(h, gain, w_bf16, cos_t, s1_t, s2_t)


def _head_masks():
    lane = lax.broadcasted_iota(jnp.int32, (QB, LANES), 1)
    return lane < HEAD_DIM


SUPER = DILATED_PATTERNS[-1][0]
GROUP = 4


def _dilated_kernel(q_ref, k_ref, v_ref, g_ref, o_ref, k1, v1, k4, v4, k16, v16, m_sc, acc_sc):
    span = pl.program_id(2)
    seq = k_ref.shape[0]
    first_head = _head_masks()
    copies = {1: (k1, v1), 4: (k4, v4), 16: (k16, v16)}

    @pl.when(span == 0)
    def _():
        def build(j, carry):
            dst = pl.ds(pl.multiple_of(j * QB, QB), QB)
            for d, (kc, vc) in copies.items():
                tiles_per_residue = seq // d // QB
                r = j // tiles_per_residue
                u0 = (j % tiles_per_residue) * QB
                src = dst if d == 1 else pl.ds(d * u0 + r, QB, stride=d)
                kc[dst, :] = k_ref[src, :].astype(BF16)
                vc[dst, :] = v_ref[src, :].astype(BF16)
            return carry
        lax.fori_loop(0, seq // QB, build, 0)

    row = lax.broadcasted_iota(jnp.int32, (QB, 2 * QB), 0)
    col = lax.broadcasted_iota(jnp.int32, (QB, 2 * QB), 1) % QB
    bias_cur = jnp.where(col <= row, 0.0, NEG_BIG)
    bias_prev = jnp.where(col >= row, 0.0, NEG_BIG)
    head_sel = (jnp.where(first_head, 1.0, 0.0).astype(BF16), jnp.where(first_head, 0.0, 1.0).astype(BF16))

    def attend(qt, kc, vc, cur_start, prev_valid, state):
        cur_start = pl.multiple_of(cur_start, QB)
        prev_start = pl.multiple_of(jnp.maximum(cur_start - QB, 0), QB)
        prev_pen = jnp.where(prev_valid, 0.0, NEG_BIG)
        scores, values = [], []
        for start, bias in ((cur_start, bias_cur), (prev_start, bias_prev + prev_pen)):
            kblk = kc[pl.ds(start, QB), :]
            vblk = vc[pl.ds(start, QB), :]
            k_bd = jnp.concatenate([kblk * head_sel[0], kblk * head_sel[1]], axis=0)
            scores.append(lax.dot_general(qt, k_bd, (((1,), (1,)), ((), ())),
                                          preferred_element_type=F32) + bias)
            values.append(jnp.concatenate(
                [jnp.concatenate([vblk * head_sel[0], head_sel[0]], axis=1),
                 jnp.concatenate([vblk * head_sel[1], head_sel[1]], axis=1)], axis=0))
        top = jnp.maximum(scores[0], scores[1])
        m_new = []
        for h in range(2):
            m_h = jnp.max(top[:, h * QB:(h + 1) * QB], axis=-1, keepdims=True)
            if state is not None:
                m_h = jnp.maximum(state[h], m_h)
            m_new.append(jnp.broadcast_to(m_h, (QB, QB)))
        m_both = jnp.concatenate(m_new, axis=1)
        pv = None
        for s, vext in zip(scores, values):
            p = jnp.exp(s - m_both).astype(BF16)
            t = jnp.dot(p, vext, preferred_element_type=F32)
            pv = t if pv is None else pv + t
        if state is not None:
            alpha = jnp.exp(jnp.where(first_head, state[0] - m_new[0], state[1] - m_new[1]))
            pv = pv + jnp.concatenate([alpha, alpha], axis=1) * state[2]
        return m_new[0], m_new[1], pv

    def load_state(rows):
        acc = jnp.concatenate([acc_sc[0, rows, :], acc_sc[1, rows, :]], axis=1)
        return m_sc[0, rows, :], m_sc[1, rows, :], acc

    def store_state(rows, state):
        m_sc[0, rows, :] = state[0]
        m_sc[1, rows, :] = state[1]
        acc_sc[0, rows, :] = state[2][:, :LANES]
        acc_sc[1, rows, :] = state[2][:, LANES:]

    blocks = SUPER // QB

    def pattern16(r):
        rows = pl.ds(r, QB, stride=16)
        qt = q_ref[rows, :].astype(BF16)
        return rows, attend(qt, k16, v16, r * (seq // 16) + span * QB, span > 0, None)

    def pattern4(idx):
        r = idx // 4
        ub = idx % 4
        rows = pl.ds(ub * (4 * QB) + r, QB, stride=4)
        qt = q_ref[rows, :].astype(BF16)
        block = span * 4 + ub
        return rows, attend(qt, k4, v4, r * (seq // 4) + block * QB, block > 0, load_state(rows))

    def pattern1(i):
        rows = pl.ds(pl.multiple_of(i * QB, QB), QB)
        qt = q_ref[rows, :].astype(BF16)
        block = span * blocks + i
        return rows, attend(qt, k1, v1, block * QB, block > 0, load_state(rows))

    def sweep(one_pass, final):
        def body(it, carry):
            results = [one_pass(it * GROUP + g) for g in range(GROUP)]
            for rows, state in results:
                if final:
                    acc = state[2]
                    o_ref[rows, :] = (acc[:, :LANES] / acc[:, LANES:] * g_ref[rows, :]).astype(BF16)
                else:
                    store_state(rows, state)
            return carry
        lax.fori_loop(0, blocks // GROUP, body, 0)

    sweep(pattern16, False)
    sweep(pattern4, False)
    sweep(pattern1, True)


def _dilated_attention(qa, ka, va, ga, batch, seq):
    n = qa.shape[0]
    pairs = W_DIL // LANES
    spans = seq // SUPER
    qmap = lambda b, p, i: (b * spans + i, p)
    kvmap = lambda b, p, i: (b, p)
    copy = pltpu.VMEM((seq, LANES), BF16)
    return pl.pallas_call(
        _dilated_kernel,
        grid=(batch, pairs, spans),
        in_specs=[pl.BlockSpec((SUPER, LANES), qmap),
                  pl.BlockSpec((seq, LANES), kvmap),
                  pl.BlockSpec((seq, LANES), kvmap),
                  pl.BlockSpec((SUPER, LANES), qmap)],
        out_specs=pl.BlockSpec((SUPER, LANES), qmap),
        out_shape=jax.ShapeDtypeStruct((n, W_DIL), BF16),
        scratch_shapes=[copy] * 6 + [pltpu.VMEM((2, SUPER, LANES), F32)] * 2,
        compiler_params=_cparams(("parallel", "parallel", "arbitrary")),
        name="dilated_attention",
    )(qa, ka, va, ga)


def _suffix_count_matrix():
    j = np.arange(QB)[:, None]
    s = np.arange(QB)[None, :]
    return np.concatenate([(j > s), np.ones((QB, QB), bool)], axis=1).astype(np.float32)


def _log_keep_and_beta(z):
    sp = jnp.maximum(z, 0.0) + jnp.log1p(jnp.exp(-jnp.abs(z)))
    return -sp, z - sp


def _stick_kernel(q_ref, k_ref, v_ref, g_ref, u_ref, o_ref):
    qi = pl.program_id(2)
    first_head = _head_masks()
    q = q_ref[...].astype(F32)
    q_heads = (jnp.where(first_head, q, 0.0).astype(BF16), jnp.where(first_head, 0.0, q).astype(BF16))
    u = u_ref[...]
    row = lax.broadcasted_iota(jnp.int32, (QB, QB), 0)
    col = lax.broadcasted_iota(jnp.int32, (QB, QB), 1)
    strictly_causal = col < row

    def tile(kb, state, diagonal):
        start = pl.multiple_of(kb * QB, QB)
        kblk = k_ref[pl.ds(start, QB), :]
        vblk = v_ref[pl.ds(start, QB), :]
        new = []
        for h in range(2):
            passed, acc = state[h]
            z = lax.dot_general(q_heads[h], kblk, (((1,), (1,)), ((), ())), preferred_element_type=F32)
            log_keep, log_beta = _log_keep_and_beta(z)
            if diagonal:
                log_keep = jnp.where(strictly_causal, log_keep, 0.0)
            hi = log_keep.astype(BF16)
            lo = (log_keep - hi.astype(F32)).astype(BF16)
            sums = (jnp.dot(hi, u, preferred_element_type=F32) + jnp.dot(lo, u, preferred_element_type=F32))
            between = passed + sums[:, :QB]
            w = jnp.exp(log_beta + between)
            if diagonal:
                w = jnp.where(strictly_causal, w, 0.0)
            acc = acc + jnp.dot(w.astype(BF16), vblk, preferred_element_type=F32)
            new.append((passed + sums[:, QB:], acc))
        return tuple(new)

    zeros = jnp.zeros((QB, LANES), F32)
    state = tile(qi, ((zeros, zeros), (zeros, zeros)), diagonal=True)

    def alive(state):
        return jnp.max(jnp.maximum(state[0][0], state[1][0])) > SB_DEAD_LOG

    def cond(c):
        kb, live, _ = c
        return jnp.logical_and(kb >= 0, live)

    def body(c):
        kb, _, state = c
        state = tile(kb, state, diagonal=False)
        return kb - 1, alive(state), state

    _, _, state = lax.while_loop(cond, body, (qi - 1, alive(state), state))
    out = jnp.where(first_head, state[0][1], state[1][1])
    o_ref[...] = (out * g_ref[...]).astype(BF16)


def _stick_breaking_attention(qc, kc, vc, gc, u, batch, seq):
    n = qc.shape[0]
    pairs = W_SB // LANES
    nq = seq // QB
    qmap = lambda b, p, i: (b * nq + i, p)
    kvmap = lambda b, p, i: (b, p)
    return pl.pallas_call(
        _stick_kernel,
        grid=(batch, pairs, nq),
        in_specs=[pl.BlockSpec((QB, LANES), qmap),
                  pl.BlockSpec((seq, LANES), kvmap),
                  pl.BlockSpec((seq, LANES), kvmap),
                  pl.BlockSpec((QB, LANES), qmap),
                  pl.BlockSpec(u.shape, lambda b, p, i: (0, 0))],
        out_specs=pl.BlockSpec((QB, LANES), qmap),
        out_shape=jax.ShapeDtypeStruct((n, W_SB), BF16),
        compiler_params=_cparams(("parallel", "parallel", "arbitrary")),
        name="stick_breaking_attention",
    )(qc, kc, vc, gc, u)


SUBLANES = 8
CONV_WIDTH = 4


def _rglru_kernel(x_ref, g_ref, cw_ref, cb_ref, wa_ref, ba_ref, wx_ref, bxg_ref, lam_ref, o_ref,
                  xbuf, a_sc, u_sc, h_sc, hcar):
    tc = x_ref.shape[0]

    @pl.when(pl.program_id(1) == 0)
    def _():
        xbuf[0:SUBLANES, :] = jnp.zeros((SUBLANES, W_LRU), F32)
        hcar[...] = jnp.zeros_like(hcar)

    x = x_ref[...]
    xbuf[SUBLANES:SUBLANES + tc, :] = x
    xc = x * cw_ref[CONV_WIDTH - 1:CONV_WIDTH, :] + cb_ref[...]
    for back in range(1, CONV_WIDTH):
        xc = xc + xbuf[SUBLANES - back:SUBLANES - back + tc, :] * cw_ref[CONV_WIDTH - 1 - back:CONV_WIDTH - back, :]
    xbuf[0:SUBLANES, :] = xbuf[tc:tc + SUBLANES, :]

    xcb = xc.astype(BF16)
    r = jax.nn.sigmoid(jnp.dot(xcb, wa_ref[...], preferred_element_type=F32) + ba_ref[...])
    i = jax.nn.sigmoid(jnp.dot(xcb, wx_ref[...], preferred_element_type=F32) + bxg_ref[...])
    lam = lam_ref[...]
    log_sig_lam = -(jnp.maximum(-lam, 0.0) + jnp.log1p(jnp.exp(-jnp.abs(lam))))
    log_a = LRU_C * r * log_sig_lam
    a = jnp.exp(log_a)
    a_sc[...] = a
    u_sc[...] = jnp.sqrt(jnp.tanh(-log_a) * (a * a + 1.0)) * (i * xc)

    rows = lax.broadcasted_iota(jnp.int32, (SUBLANES, W_LRU), 0)

    def step(j, h_prev):
        start = pl.multiple_of(j * SUBLANES, SUBLANES)
        a = a_sc[pl.ds(start, SUBLANES), :]
        b = u_sc[pl.ds(start, SUBLANES), :]
        for shift in (1, 2, 4):
            keep = rows >= shift
            a_prev = jnp.where(keep, pltpu.roll(a, shift, 0), 1.0)
            b_prev = jnp.where(keep, pltpu.roll(b, shift, 0), 0.0)
            b = a * b_prev + b
            a = a * a_prev
        h = a * h_prev + b
        h_sc[pl.ds(start, SUBLANES), :] = h
        return jnp.broadcast_to(h[SUBLANES - 1:SUBLANES, :], (SUBLANES, W_LRU))

    hcar[...] = lax.fori_loop(0, tc // SUBLANES, step, hcar[...])
    o_ref[...] = (h_sc[...] * g_ref[...]).astype(BF16)


def _rg_lru(bx, bg, cw, cb, wa, ba, wx, bxg, lam, batch, seq, tc):
    n = bx.shape[0]
    nchunks = seq // tc
    row = lambda b, c: (b * nchunks + c, 0)
    const = lambda b, c: (0, 0)
    return pl.pallas_call(
        _rglru_kernel,
        grid=(batch, nchunks),
        in_specs=[pl.BlockSpec((tc, W_LRU), row),
                  pl.BlockSpec((tc, W_LRU), row),
                  pl.BlockSpec((CONV_WIDTH, W_LRU), const),
                  pl.BlockSpec((1, W_LRU), const),
                  pl.BlockSpec((W_LRU, W_LRU), const),
                  pl.BlockSpec((1, W_LRU), const),
                  pl.BlockSpec((W_LRU, W_LRU), const),
                  pl.BlockSpec((1, W_LRU), const),
                  pl.BlockSpec((1, W_LRU), const)],
        out_specs=pl.BlockSpec((tc, W_LRU), row),
        out_shape=jax.ShapeDtypeStruct((n, W_LRU), BF16),
        scratch_shapes=[pltpu.VMEM((tc + SUBLANES, W_LRU), F32),
                        pltpu.VMEM((tc, W_LRU), F32),
                        pltpu.VMEM((tc, W_LRU), F32),
                        pltpu.VMEM((tc, W_LRU), F32),
                        pltpu.VMEM((SUBLANES, W_LRU), F32)],
        compiler_params=_cparams(("arbitrary", "arbitrary")),
        name="rg_lru",
    )(bx, bg, cw, cb, wa, ba, wx, bxg, lam)


def _outproj_kernel(x_ref, ya_ref, yb_ref, yc_ref, w_ref, fg_ref, o_ref, *, final_norm):
    acc = x_ref[...]
    acc = acc + jnp.dot(ya_ref[...], w_ref[0:W_DIL, :], preferred_element_type=F32)
    acc = acc + jnp.dot(yb_ref[...], w_ref[W_DIL:W_DIL + W_LRU, :], preferred_element_type=F32)
    acc = acc + jnp.dot(yc_ref[...], w_ref[W_DIL + W_LRU:, :], preferred_element_type=F32)
    if final_norm:
        ms = jnp.mean(acc * acc, axis=-1, keepdims=True)
        acc = (acc * lax.rsqrt(ms + EPS)) * fg_ref[...]
    o_ref[...] = acc


def _out_projection(h, ya, yb, yc, w_bf16, final_gain, final_norm, tm):
    n = h.shape[0]
    row = lambda i: (i, 0)
    const = lambda i: (0, 0)
    return pl.pallas_call(
        functools.partial(_outproj_kernel, final_norm=final_norm),
        grid=(n // tm,),
        in_specs=[pl.BlockSpec((tm, D_MODEL), row),
                  pl.BlockSpec((tm, W_DIL), row),
                  pl.BlockSpec((tm, W_LRU), row),
                  pl.BlockSpec((tm, W_SB), row),
                  pl.BlockSpec((D_MODEL, D_MODEL), const),
                  pl.BlockSpec((1, D_MODEL), const)],
        out_specs=pl.BlockSpec((tm, D_MODEL), row),
        out_shape=jax.ShapeDtypeStruct((n, D_MODEL), F32),
        compiler_params=_cparams(("parallel",)),
        name="out_projection",
    )(h, ya, yb, yc, w_bf16, final_gain)


def _rope_tables(seq):
    pos = jnp.arange(seq, dtype=F32)
    inv_freq = ROPE_THETA ** (-jnp.arange(0, ROPE_DIM, 2, dtype=F32) / ROPE_DIM)
    ang = pos[:, None] * inv_freq[None, :]
    cos, sin = jnp.cos(ang), jnp.sin(ang)
    half = ROPE_DIM // 2
    lane = np.arange(LANES) % HEAD_DIM
    freq = lane % half
    cos_l, sin_l = cos[:, freq], sin[:, freq]
    cos_t = jnp.where(lane < ROPE_DIM, cos_l, 1.0)
    s1_t = jnp.where(lane < half, -sin_l, 0.0)
    s2_t = jnp.where((lane >= half) & (lane < ROPE_DIM), sin_l, 0.0)
    return cos_t, s1_t, s2_t


def _block_diagonal(w):
    nb, d, _ = w.shape
    out = jnp.zeros((nb * d, nb * d), w.dtype)
    for i in range(nb):
        out = lax.dynamic_update_slice(out, w[i], (i * d, i * d))
    return out


def kernel(x, norm_gain, w_in, conv_w, conv_b, gate_a_w, gate_a_b, gate_x_w, gate_x_b, lru_lambda, w_out, final_gain):
    batch, seq, d = x.shape
    depth = w_in.shape[0]
    n = batch * seq
    cos_t, s1_t, s2_t = _rope_tables(seq)
    u = jnp.asarray(_suffix_count_matrix(), dtype=BF16)
    h = x.reshape(n, d)
    for l in range(depth):
        qa, ka, va, ga, bx, bg, qc, kc, vc, gc = _in_projection(
            h, norm_gain[l].reshape(1, d), w_in[l].astype(BF16), cos_t, s1_t, s2_t, seq, tm=256)
        ya = _dilated_attention(qa, ka, va, ga, batch, seq)
        yb = _rg_lru(bx, bg, conv_w[l], conv_b[l].reshape(1, W_LRU),
                     _block_diagonal(gate_a_w[l]).astype(BF16), gate_a_b[l].reshape(1, W_LRU),
                     _block_diagonal(gate_x_w[l]).astype(BF16), gate_x_b[l].reshape(1, W_LRU),
                     lru_lambda[l].reshape(1, W_LRU), batch, seq, tc=512)
        yc = _stick_breaking_attention(qc, kc, vc, gc, u, batch, seq)
        h = _out_projection(h, ya, yb, yc, w_out[l].astype(BF16), final_gain.reshape(1, d),
                            final_norm=(l == depth - 1), tm=256)
    return h.reshape(batch, seq, d)
```

```python
import functools
import math

import jax
import jax.numpy as jnp
import numpy as np
from jax import lax
from jax.experimental import pallas as pl
from jax.experimental.pallas import tpu as pltpu

F32 = jnp.float32
BF16 = jnp.bfloat16

D_MODEL = 1024
HEAD_DIM = 64
W_DIL = 512
W_LRU = 256
W_SB = 256
D_IN = 4 * W_DIL + 2 * W_LRU + 4 * W_SB
ROPE_DIM = 16
ROPE_THETA = 500000.0
DILATED_PATTERNS = ((128, 1), (512, 4), (2048, 16))
LRU_C = 8.0
EPS = 1e-6

LANES = 128
QB = 128
NEG_BIG = -1e30
SB_DEAD_LOG = -104.0

VMEM_LIMIT = 56 * 1024 * 1024


def _cparams(sem):
    return pltpu.CompilerParams(dimension_semantics=sem, vmem_limit_bytes=VMEM_LIMIT)


def _inproj_kernel(x_ref, gain_ref, w_ref, cos_ref, s1_ref, s2_ref,
                   qa_ref, ka_ref, va_ref, ga_ref, bx_ref, bg_ref,
                   qc_ref, kc_ref, vc_ref, gc_ref):
    x = x_ref[...]
    ms = jnp.mean(x * x, axis=-1, keepdims=True)
    xn = ((x * lax.rsqrt(ms + EPS)) * gain_ref[...]).astype(BF16)

    def proj(lo, width):
        return jnp.dot(xn, w_ref[:, lo:lo + width], preferred_element_type=F32)

    cos = cos_ref[...]
    s1 = s1_ref[...]
    s2 = s2_ref[...]

    def rope(t):
        parts = []
        for c in range(t.shape[1] // LANES):
            xc = t[:, c * LANES:(c + 1) * LANES]
            parts.append(xc * cos + pltpu.roll(xc, LANES - ROPE_DIM // 2, 1) * s1
                         + pltpu.roll(xc, ROPE_DIM // 2, 1) * s2)
        return jnp.concatenate(parts, axis=1)

    def silu(t):
        return t * jax.nn.sigmoid(t)

    scale = 1.0 / math.sqrt(HEAD_DIM)
    o = 0
    qa_ref[...] = rope(proj(o, W_DIL)) * scale; o += W_DIL
    ka_ref[...] = rope(proj(o, W_DIL)); o += W_DIL
    va_ref[...] = proj(o, W_DIL); o += W_DIL
    ga_ref[...] = silu(proj(o, W_DIL)); o += W_DIL
    bx_ref[...] = proj(o, W_LRU); o += W_LRU
    bg_ref[...] = silu(proj(o, W_LRU)); o += W_LRU
    qc_ref[...] = (proj(o, W_SB) * scale).astype(BF16); o += W_SB
    kc_ref[...] = proj(o, W_SB).astype(BF16); o += W_SB
    vc_ref[...] = proj(o, W_SB).astype(BF16); o += W_SB
    gc_ref[...] = silu(proj(o, W_SB))


def _in_projection(h, gain, w_bf16, cos_t, s1_t, s2_t, seq, tm):
    n = h.shape[0]
    blocks_per_seq = seq // tm
    row = lambda i: (i, 0)
    tab = lambda i: (i % blocks_per_seq, 0)
    const = lambda i: (0, 0)
    widths = [(W_DIL, F32), (W_DIL, F32), (W_DIL, F32), (W_DIL, F32), (W_LRU, F32),
              (W_LRU, F32), (W_SB, BF16), (W_SB, BF16), (W_SB, BF16), (W_SB, F32)]
    return pl.pallas_call(
        _inproj_kernel,
        grid=(n // tm,),
        in_specs=[pl.BlockSpec((tm, D_MODEL), row),
                  pl.BlockSpec((1, D_MODEL), const),
                  pl.BlockSpec((D_MODEL, D_IN), const),
                  pl.BlockSpec((tm, LANES), tab),
                  pl.BlockSpec((tm, LANES), tab),
                  pl.BlockSpec((tm, LANES), tab)],
        out_specs=[pl.BlockSpec((tm, w), row) for w, _ in widths],
        out_shape=[jax.ShapeDtypeStruct((n, w), dt) for w, dt in widths],
        compiler_params=_cparams(("parallel",)),
        name="in_projection",
    )(h, gain, w_bf16, cos_t, s1_t, s2_t)


def _head_masks():
    lane = lax.broadcasted_iota(jnp.int32, (QB, LANES), 1)
    return lane < HEAD_DIM


SUPER = DILATED_PATTERNS[-1][0]
GROUP = 4


def _dilated_kernel(q_ref, k_ref, v_ref, g_ref, o_ref, k1, v1, k4, v4, k16, v16, m_sc, acc_sc):
    span = pl.program_id(2)
    seq = k_ref.shape[0]
    first_head = _head_masks()
    copies = {1: (k1, v1), 4: (k4, v4), 16: (k16, v16)}

    @pl.when(span == 0)
    def _():
        def build(j, carry):
            dst = pl.ds(pl.multiple_of(j * QB, QB), QB)
            for d, (kc, vc) in copies.items():
                tiles_per_residue = seq // d // QB
                r = j // tiles_per_residue
                u0 = (j % tiles_per_residue) * QB
                src = dst if d == 1 else pl.ds(d * u0 + r, QB, stride=d)
                kc[dst, :] = k_ref[src, :].astype(BF16)
                vc[dst, :] = v_ref[src, :].astype(BF16)
            return carry
        lax.fori_loop(0, seq // QB, build, 0)

    row = lax.broadcasted_iota(jnp.int32, (QB, 2 * QB), 0)
    col = lax.broadcasted_iota(jnp.int32, (QB, 2 * QB), 1) % QB
    bias_cur = jnp.where(col <= row, 0.0, NEG_BIG)
    bias_prev = jnp.where(col >= row, 0.0, NEG_BIG)
    head_sel = (jnp.where(first_head, 1.0, 0.0).astype(BF16), jnp.where(first_head, 0.0, 1.0).astype(BF16))

    groups = range(GROUP)

    def attend(kc, vc, rows, cur_start, prev_valid, fresh):
        qt = [q_ref[rows[g], :].astype(BF16) for g in groups]
        if not fresh:
            old = [(m_sc[0, rows[g], :], m_sc[1, rows[g], :],
                    jnp.concatenate([acc_sc[0, rows[g], :], acc_sc[1, rows[g], :]], axis=1)) for g in groups]
        starts, biases = [], []
        for g in groups:
            cur = pl.multiple_of(cur_start[g], QB)
            prev = pl.multiple_of(jnp.maximum(cur_start[g] - QB, 0), QB)
            starts.append((cur, prev))
            biases.append((bias_cur, bias_prev + jnp.where(prev_valid[g], 0.0, NEG_BIG)))
        k_bd = [[jnp.concatenate([kblk * head_sel[0], kblk * head_sel[1]], axis=0)
                 for kblk in (kc[pl.ds(s, QB), :] for s in starts[g])] for g in groups]
        scores = [[lax.dot_general(qt[g], k_bd[g][t], (((1,), (1,)), ((), ())),
                                   preferred_element_type=F32) + biases[g][t] for t in range(2)]
                  for g in groups]
        top = [jnp.maximum(scores[g][0], scores[g][1]) for g in groups]
        m_new = []
        for g in groups:
            pair = []
            for h in range(2):
                m_h = jnp.max(top[g][:, h * QB:(h + 1) * QB], axis=-1, keepdims=True)
                if not fresh:
                    m_h = jnp.maximum(old[g][h], m_h)
                pair.append(jnp.broadcast_to(m_h, (QB, QB)))
            m_new.append(pair)
        p = [[jnp.exp(scores[g][t] - jnp.concatenate(m_new[g], axis=1)).astype(BF16) for t in range(2)]
             for g in groups]
        v_ext = [[jnp.concatenate([jnp.concatenate([vblk * head_sel[0], head_sel[0]], axis=1),
                                   jnp.concatenate([vblk * head_sel[1], head_sel[1]], axis=1)], axis=0)
                  for vblk in (vc[pl.ds(s, QB), :] for s in starts[g])] for g in groups]
        pv = [jnp.dot(p[g][0], v_ext[g][0], preferred_element_type=F32)
              + jnp.dot(p[g][1], v_ext[g][1], preferred_element_type=F32) for g in groups]
        if not fresh:
            alpha = [jnp.exp(jnp.where(first_head, old[g][0] - m_new[g][0], old[g][1] - m_new[g][1]))
                     for g in groups]
            pv = [pv[g] + jnp.concatenate([alpha[g], alpha[g]], axis=1) * old[g][2] for g in groups]
        return [(m_new[g][0], m_new[g][1], pv[g]) for g in groups]

    def store_state(rows, state):
        m_sc[0, rows, :] = state[0]
        m_sc[1, rows, :] = state[1]
        acc_sc[0, rows, :] = state[2][:, :LANES]
        acc_sc[1, rows, :] = state[2][:, LANES:]

    blocks = SUPER // QB

    def pattern16(r):
        return pl.ds(r, QB, stride=16), r * (seq // 16) + span * QB, span > 0

    def pattern4(idx):
        r = idx // 4
        ub = idx % 4
        block = span * 4 + ub
        return pl.ds(ub * (4 * QB) + r, QB, stride=4), r * (seq // 4) + block * QB, block > 0

    def pattern1(i):
        block = span * blocks + i
        return pl.ds(pl.multiple_of(i * QB, QB), QB), block * QB, block > 0

    def sweep(pattern, kc, vc, fresh, final):
        def body(it, carry):
            rows, cur_start, prev_valid = zip(*[pattern(it * GROUP + g) for g in groups])
            states = attend(kc, vc, rows, cur_start, prev_valid, fresh)
            for g in groups:
                if final:
                    acc = states[g][2]
                    o_ref[rows[g], :] = (acc[:, :LANES] / acc[:, LANES:] * g_ref[rows[g], :]).astype(BF16)
                else:
                    store_state(rows[g], states[g])
            return carry
        lax.fori_loop(0, blocks // GROUP, body, 0)

    sweep(pattern16, k16, v16, fresh=True, final=False)
    sweep(pattern4, k4, v4, fresh=False, final=False)
    sweep(pattern1, k1, v1, fresh=False, final=True)


def _dilated_attention(qa, ka, va, ga, batch, seq):
    n = qa.shape[0]
    pairs = W_DIL // LANES
    spans = seq // SUPER
    qmap = lambda b, p, i: (b * spans + i, p)
    kvmap = lambda b, p, i: (b, p)
    copy = pltpu.VMEM((seq, LANES), BF16)
    return pl.pallas_call(
        _dilated_kernel,
        grid=(batch, pairs, spans),
        in_specs=[pl.BlockSpec((SUPER, LANES), qmap),
                  pl.BlockSpec((seq, LANES), kvmap),
                  pl.BlockSpec((seq, LANES), kvmap),
                  pl.BlockSpec((SUPER, LANES), qmap)],
        out_specs=pl.BlockSpec((SUPER, LANES), qmap),
        out_shape=jax.ShapeDtypeStruct((n, W_DIL), BF16),
        scratch_shapes=[copy] * 6 + [pltpu.VMEM((2, SUPER, LANES), F32)] * 2,
        compiler_params=_cparams(("parallel", "parallel", "arbitrary")),
        name="dilated_attention",
    )(qa, ka, va, ga)


def _suffix_count_matrix():
    j = np.arange(QB)[:, None]
    s = np.arange(QB)[None, :]
    u = (j > s).astype(np.float32)
    z = np.zeros_like(u)
    return np.block([[u, z], [z, u]])


def _log_keep_and_beta(z):
    log_keep = jnp.log(1.0 / (1.0 + jnp.exp(-jnp.abs(z)))) - jnp.maximum(z, 0.0)
    return log_keep, z + log_keep


SB_GROUP = 4


def _stick_kernel(q_ref, k_ref, v_ref, g_ref, u_ref, o_ref):
    step = pl.program_id(2)
    first_head = _head_masks()
    head_sel = (jnp.where(first_head, 1.0, 0.0).astype(BF16), jnp.where(first_head, 0.0, 1.0).astype(BF16))
    u = u_ref[...]
    row = lax.broadcasted_iota(jnp.int32, (QB, 2 * QB), 0)
    col = lax.broadcasted_iota(jnp.int32, (QB, 2 * QB), 1) % QB
    strictly_causal = col < row

    groups = range(SB_GROUP)

    def tiles(back, passed, acc, diagonal):
        kbs = [step * SB_GROUP + g - back for g in groups]
        k_bd, v_bd = [], []
        for kb in kbs:
            start = pl.multiple_of(jnp.maximum(kb, 0) * QB, QB)
            kblk = k_ref[pl.ds(start, QB), :]
            vblk = v_ref[pl.ds(start, QB), :]
            k_bd.append(jnp.concatenate([kblk * head_sel[0], kblk * head_sel[1]], axis=0))
            v_bd.append(jnp.concatenate([vblk * head_sel[0], vblk * head_sel[1]], axis=0))
        z = [lax.dot_general(q_ref[g * QB:(g + 1) * QB, :], k_bd[g], (((1,), (1,)), ((), ())),
                             preferred_element_type=F32) for g in groups]
        keep_beta = [_log_keep_and_beta(z[g]) for g in groups]
        if diagonal:
            log_keep = [jnp.where(strictly_causal, kb_[0], 0.0) for kb_ in keep_beta]
        else:
            log_keep = [kb_[0] for kb_ in keep_beta]
            passed = [passed[g] + jnp.where(kbs[g] >= 0, 0.0, NEG_BIG) for g in groups]
        hi = [lk.astype(BF16) for lk in log_keep]
        lo = [(log_keep[g] - hi[g].astype(F32)).astype(BF16) for g in groups]
        between = [passed[g] + (jnp.dot(hi[g], u, preferred_element_type=F32)
                                + jnp.dot(lo[g], u, preferred_element_type=F32)) for g in groups]
        w = [jnp.exp(keep_beta[g][1] + between[g]) for g in groups]
        if diagonal:
            w = [jnp.where(strictly_causal, w_, 0.0) for w_ in w]
        acc = [acc[g] + jnp.dot(w[g].astype(BF16), v_bd[g], preferred_element_type=F32) for g in groups]
        total = [between[g] + log_keep[g] for g in groups]
        total = [(t[:, 0:1], t[:, QB:QB + 1]) for t in total]
        passed = [jnp.concatenate([jnp.broadcast_to(t, (QB, QB)) for t in tt], axis=1) for tt in total]
        top = functools.reduce(jnp.maximum, [jnp.maximum(tt[0], tt[1]) for tt in total])
        return tuple(passed), tuple(acc), jnp.max(top) > SB_DEAD_LOG

    passed, acc, live = tiles(0, [jnp.zeros((QB, 2 * QB), F32)] * SB_GROUP,
                              [jnp.zeros((QB, LANES), F32)] * SB_GROUP, diagonal=True)

    def cond(c):
        back, live, _, _ = c
        return jnp.logical_and(back <= step * SB_GROUP + SB_GROUP - 1, live)

    def body(c):
        back, _, passed, acc = c
        passed, acc, live = tiles(back, passed, acc, diagonal=False)
        return back + 1, live, passed, acc

    _, _, _, acc = lax.while_loop(cond, body, (1, live, passed, acc))
    for g in groups:
        rows = slice(g * QB, (g + 1) * QB)
        o_ref[rows, :] = (acc[g] * g_ref[rows, :]).astype(BF16)


def _stick_breaking_attention(qc, kc, vc, gc, u, batch, seq):
    n = qc.shape[0]
    pairs = W_SB // LANES
    rows = SB_GROUP * QB
    steps = seq // rows
    qmap = lambda b, p, i: (b * steps + i, p)
    kvmap = lambda b, p, i: (b, p)
    return pl.pallas_call(
        _stick_kernel,
        grid=(batch, pairs, steps),
        in_specs=[pl.BlockSpec((rows, LANES), qmap),
                  pl.BlockSpec((seq, LANES), kvmap),
                  pl.BlockSpec((seq, LANES), kvmap),
                  pl.BlockSpec((rows, LANES), qmap),
                  pl.BlockSpec(u.shape, lambda b, p, i: (0, 0))],
        out_specs=pl.BlockSpec((rows, LANES), qmap),
        out_shape=jax.ShapeDtypeStruct((n, W_SB), BF16),
        compiler_params=_cparams(("parallel", "parallel", "arbitrary")),
        name="stick_breaking_attention",
    )(qc, kc, vc, gc, u)


SUBLANES = 8
CONV_WIDTH = 4


def _rglru_kernel(x_ref, g_ref, cw_ref, cb_ref, wa_ref, ba_ref, wx_ref, bxg_ref, lam_ref, o_ref,
                  xbuf, a_sc, u_sc, h_sc, hcar):
    tc = x_ref.shape[0]

    @pl.when(pl.program_id(1) == 0)
    def _():
        xbuf[0:SUBLANES, :] = jnp.zeros((SUBLANES, W_LRU), F32)
        hcar[...] = jnp.zeros_like(hcar)

    x = x_ref[...]
    xbuf[SUBLANES:SUBLANES + tc, :] = x
    xc = x * cw_ref[CONV_WIDTH - 1:CONV_WIDTH, :] + cb_ref[...]
    for back in range(1, CONV_WIDTH):
        xc = xc + xbuf[SUBLANES - back:SUBLANES - back + tc, :] * cw_ref[CONV_WIDTH - 1 - back:CONV_WIDTH - back, :]
    xbuf[0:SUBLANES, :] = xbuf[tc:tc + SUBLANES, :]

    xcb = xc.astype(BF16)
    r = jax.nn.sigmoid(jnp.dot(xcb, wa_ref[...], preferred_element_type=F32) + ba_ref[...])
    i = jax.nn.sigmoid(jnp.dot(xcb, wx_ref[...], preferred_element_type=F32) + bxg_ref[...])
    lam = lam_ref[...]
    log_sig_lam = -(jnp.maximum(-lam, 0.0) + jnp.log1p(jnp.exp(-jnp.abs(lam))))
    log_a = LRU_C * r * log_sig_lam
    a = jnp.exp(log_a)
    a_sc[...] = a
    u_sc[...] = jnp.sqrt(jnp.tanh(-log_a) * (a * a + 1.0)) * (i * xc)

    rows = lax.broadcasted_iota(jnp.int32, (SUBLANES, W_LRU), 0)

    def step(j, h_prev):
        start = pl.multiple_of(j * SUBLANES, SUBLANES)
        a = a_sc[pl.ds(start, SUBLANES), :]
        b = u_sc[pl.ds(start, SUBLANES), :]
        for shift in (1, 2, 4):
            keep = rows >= shift
            a_prev = jnp.where(keep, pltpu.roll(a, shift, 0), 1.0)
            b_prev = jnp.where(keep, pltpu.roll(b, shift, 0), 0.0)
            b = a * b_prev + b
            a = a * a_prev
        h = a * h_prev + b
        h_sc[pl.ds(start, SUBLANES), :] = h
        return jnp.broadcast_to(h[SUBLANES - 1:SUBLANES, :], (SUBLANES, W_LRU))

    hcar[...] = lax.fori_loop(0, tc // SUBLANES, step, hcar[...])
    o_ref[...] = (h_sc[...] * g_ref[...]).astype(BF16)


def _rg_lru(bx, bg, cw, cb, wa, ba, wx, bxg, lam, batch, seq, tc):
    n = bx.shape[0]
    nchunks = seq // tc
    row = lambda b, c: (b * nchunks + c, 0)
    const = lambda b, c: (0, 0)
    return pl.pallas_call(
        _rglru_kernel,
        grid=(batch, nchunks),
        in_specs=[pl.BlockSpec((tc, W_LRU), row),
                  pl.BlockSpec((tc, W_LRU), row),
                  pl.BlockSpec((CONV_WIDTH, W_LRU), const),
                  pl.BlockSpec((1, W_LRU), const),
                  pl.BlockSpec((W_LRU, W_LRU), const),
                  pl.BlockSpec((1, W_LRU), const),
                  pl.BlockSpec((W_LRU, W_LRU), const),
                  pl.BlockSpec((1, W_LRU), const),
                  pl.BlockSpec((1, W_LRU), const)],
        out_specs=pl.BlockSpec((tc, W_LRU), row),
        out_shape=jax.ShapeDtypeStruct((n, W_LRU), BF16),
        scratch_shapes=[pltpu.VMEM((tc + SUBLANES, W_LRU), F32),
                        pltpu.VMEM((tc, W_LRU), F32),
                        pltpu.VMEM((tc, W_LRU), F32),
                        pltpu.VMEM((tc, W_LRU), F32),
                        pltpu.VMEM((SUBLANES, W_LRU), F32)],
        compiler_params=_cparams(("arbitrary", "arbitrary")),
        name="rg_lru",
    )(bx, bg, cw, cb, wa, ba, wx, bxg, lam)


def _outproj_kernel(x_ref, ya_ref, yb_ref, yc_ref, w_ref, fg_ref, o_ref, *, final_norm):
    acc = x_ref[...]
    acc = acc + jnp.dot(ya_ref[...], w_ref[0:W_DIL, :], preferred_element_type=F32)
    acc = acc + jnp.dot(yb_ref[...], w_ref[W_DIL:W_DIL + W_LRU, :], preferred_element_type=F32)
    acc = acc + jnp.dot(yc_ref[...], w_ref[W_DIL + W_LRU:, :], preferred_element_type=F32)
    if final_norm:
        ms = jnp.mean(acc * acc, axis=-1, keepdims=True)
        acc = (acc * lax.rsqrt(ms + EPS)) * fg_ref[...]
    o_ref[...] = acc


def _out_projection(h, ya, yb, yc, w_bf16, final_gain, final_norm, tm):
    n = h.shape[0]
    row = lambda i: (i, 0)
    const = lambda i: (0, 0)
    return pl.pallas_call(
        functools.partial(_outproj_kernel, final_norm=final_norm),
        grid=(n // tm,),
        in_specs=[pl.BlockSpec((tm, D_MODEL), row),
                  pl.BlockSpec((tm, W_DIL), row),
                  pl.BlockSpec((tm, W_LRU), row),
                  pl.BlockSpec((tm, W_SB), row),
                  pl.BlockSpec((D_MODEL, D_MODEL), const),
                  pl.BlockSpec((1, D_MODEL), const)],
        out_specs=pl.BlockSpec((tm, D_MODEL), row),
        out_shape=jax.ShapeDtypeStruct((n, D_MODEL), F32),
        compiler_params=_cparams(("parallel",)),
        name="out_projection",
    )(h, ya, yb, yc, w_bf16, final_gain)


def _rope_tables(seq):
    pos = jnp.arange(seq, dtype=F32)
    inv_freq = ROPE_THETA ** (-jnp.arange(0, ROPE_DIM, 2, dtype=F32) / ROPE_DIM)
    ang = pos[:, None] * inv_freq[None, :]
    cos, sin = jnp.cos(ang), jnp.sin(ang)
    half = ROPE_DIM // 2
    lane = np.arange(LANES) % HEAD_DIM
    freq = lane % half
    cos_l, sin_l = cos[:, freq], sin[:, freq]
    cos_t = jnp.where(lane < ROPE_DIM, cos_l, 1.0)
    s1_t = jnp.where(lane < half, -sin_l, 0.0)
    s2_t = jnp.where((lane >= half) & (lane < ROPE_DIM), sin_l, 0.0)
    return cos_t, s1_t, s2_t


def _block_diagonal(w):
    nb, d, _ = w.shape
    out = jnp.zeros((nb * d, nb * d), w.dtype)
    for i in range(nb):
        out = lax.dynamic_update_slice(out, w[i], (i * d, i * d))
    return out


def kernel(x, norm_gain, w_in, conv_w, conv_b, gate_a_w, gate_a_b, gate_x_w, gate_x_b, lru_lambda, w_out, final_gain):
    batch, seq, d = x.shape
    depth = w_in.shape[0]
    n = batch * seq
    cos_t, s1_t, s2_t = _rope_tables(seq)
    u = jnp.asarray(_suffix_count_matrix(), dtype=BF16)
    h = x.reshape(n, d)
    for l in range(depth):
        qa, ka, va, ga, bx, bg, qc, kc, vc, gc = _in_projection(
            h, norm_gain[l].reshape(1, d), w_in[l].astype(BF16), cos_t, s1_t, s2_t, seq, tm=256)
        ya = _dilated_attention(qa, ka, va, ga, batch, seq)
        yb = _rg_lru(bx, bg, conv_w[l], conv_b[l].reshape(1, W_LRU),
                     _block_diagonal(gate_a_w[l]).astype(BF16), gate_a_b[l].reshape(1, W_LRU),
                     _block_diagonal(gate_x_w[l]).astype(BF16), gate_x_b[l].reshape(1, W_LRU),
                     lru_lambda[l].reshape(1, W_LRU), batch, seq, tc=512)
        yc = _stick_breaking_attention(qc, kc, vc, gc, u, batch, seq)
        h = _out_projection(h, ya, yb, yc, w_out[l].astype(BF16), final_gain.reshape(1, d),
                            final_norm=(l == depth - 1), tm=256)
    return h.reshape(batch, seq, d)
```

```python
import functools
import math

import jax
import jax.numpy as jnp
import numpy as np
from jax import lax
from jax.experimental import pallas as pl
from jax.experimental.pallas import tpu as pltpu

F32 = jnp.float32
BF16 = jnp.bfloat16

D_MODEL = 1024
HEAD_DIM = 64
W_DIL = 512
W_LRU = 256
W_SB = 256
D_IN = 4 * W_DIL + 2 * W_LRU + 4 * W_SB
ROPE_DIM = 16
ROPE_THETA = 500000.0
DILATED_PATTERNS = ((128, 1), (512, 4), (2048, 16))
LRU_C = 8.0
EPS = 1e-6

LANES = 128
QB = 128
NEG_BIG = -1e30
SB_DEAD_LOG = -104.0

VMEM_LIMIT = 56 * 1024 * 1024


def _cparams(sem):
    return pltpu.CompilerParams(dimension_semantics=sem, vmem_limit_bytes=VMEM_LIMIT)


def _inproj_kernel(x_ref, gain_ref, w_ref, cos_ref, s1_ref, s2_ref,
                   qa_ref, ka_ref, va_ref, ga_ref, bx_ref, bg_ref,
                   qc_ref, kc_ref, vc_ref, gc_ref):
    x = x_ref[...]
    ms = jnp.mean(x * x, axis=-1, keepdims=True)
    xn = ((x * lax.rsqrt(ms + EPS)) * gain_ref[...]).astype(BF16)

    def proj(lo, width):
        return jnp.dot(xn, w_ref[:, lo:lo + width], preferred_element_type=F32)

    cos = cos_ref[...]
    s1 = s1_ref[...]
    s2 = s2_ref[...]

    def rope(t):
        parts = []
        for c in range(t.shape[1] // LANES):
            xc = t[:, c * LANES:(c + 1) * LANES]
            parts.append(xc * cos + pltpu.roll(xc, LANES - ROPE_DIM // 2, 1) * s1
                         + pltpu.roll(xc, ROPE_DIM // 2, 1) * s2)
        return jnp.concatenate(parts, axis=1)

    def silu(t):
        return t * jax.nn.sigmoid(t)

    scale = 1.0 / math.sqrt(HEAD_DIM)
    o = 0
    qa_ref[...] = rope(proj(o, W_DIL)) * (scale * math.log2(math.e)); o += W_DIL
    ka_ref[...] = rope(proj(o, W_DIL)); o += W_DIL
    va_ref[...] = proj(o, W_DIL); o += W_DIL
    ga_ref[...] = silu(proj(o, W_DIL)); o += W_DIL
    bx_ref[...] = proj(o, W_LRU); o += W_LRU
    bg_ref[...] = silu(proj(o, W_LRU)); o += W_LRU
    qc_ref[...] = (proj(o, W_SB) * scale).astype(BF16); o += W_SB
    kc_ref[...] = proj(o, W_SB).astype(BF16); o += W_SB
    vc_ref[...] = proj(o, W_SB).astype(BF16); o += W_SB
    gc_ref[...] = silu(proj(o, W_SB))


def _in_projection(h, gain, w_bf16, cos_t, s1_t, s2_t, seq, tm):
    n = h.shape[0]
    blocks_per_seq = seq // tm
    row = lambda i: (i, 0)
    tab = lambda i: (i % blocks_per_seq, 0)
    const = lambda i: (0, 0)
    widths = [(W_DIL, F32), (W_DIL, F32), (W_DIL, F32), (W_DIL, F32), (W_LRU, F32),
              (W_LRU, F32), (W_SB, BF16), (W_SB, BF16), (W_SB, BF16), (W_SB, F32)]
    return pl.pallas_call(
        _inproj_kernel,
        grid=(n // tm,),
        in_specs=[pl.BlockSpec((tm, D_MODEL), row),
                  pl.BlockSpec((1, D_MODEL), const),
                  pl.BlockSpec((D_MODEL, D_IN), const),
                  pl.BlockSpec((tm, LANES), tab),
                  pl.BlockSpec((tm, LANES), tab),
                  pl.BlockSpec((tm, LANES), tab)],
        out_specs=[pl.BlockSpec((tm, w), row) for w, _ in widths],
        out_shape=[jax.ShapeDtypeStruct((n, w), dt) for w, dt in widths],
        compiler_params=_cparams(("parallel",)),
        name="in_projection",
    )(h, gain, w_bf16, cos_t, s1_t, s2_t)


def _head_masks():
    lane = lax.broadcasted_iota(jnp.int32, (QB, LANES), 1)
    return lane < HEAD_DIM


SUPER = DILATED_PATTERNS[-1][0]
GROUP = 2


def _dilated_kernel(q_ref, k_ref, v_ref, g_ref, o_ref, k1, v1, k4, v4, k16, v16, m_sc, acc_sc):
    span = pl.program_id(2)
    seq = k_ref.shape[0]
    first_head = _head_masks()
    copies = {1: (k1, v1), 4: (k4, v4), 16: (k16, v16)}

    @pl.when(span == 0)
    def _():
        def build(j, carry):
            dst = pl.ds(pl.multiple_of(j * QB, QB), QB)
            for d, (kc, vc) in copies.items():
                tiles_per_residue = seq // d // QB
                r = j // tiles_per_residue
                u0 = (j % tiles_per_residue) * QB
                src = dst if d == 1 else pl.ds(d * u0 + r, QB, stride=d)
                kc[dst, :] = k_ref[src, :].astype(BF16)
                vc[dst, :] = v_ref[src, :].astype(BF16)
            return carry
        lax.fori_loop(0, seq // QB, build, 0)

    row = lax.broadcasted_iota(jnp.int32, (QB, 2 * QB), 0)
    col = lax.broadcasted_iota(jnp.int32, (QB, 2 * QB), 1) % QB
    bias_cur = jnp.where(col <= row, 0.0, NEG_BIG)
    bias_prev = jnp.where(col >= row, 0.0, NEG_BIG)
    head_sel = (jnp.where(first_head, 1.0, 0.0).astype(BF16), jnp.where(first_head, 0.0, 1.0).astype(BF16))

    groups = range(GROUP)

    def score_stage(kc, rows, cur_start, prev_valid):
        qt = [q_ref[rows[g], :].astype(BF16) for g in groups]
        starts, biases = [], []
        for g in groups:
            cur = pl.multiple_of(cur_start[g], QB)
            prev = pl.multiple_of(jnp.maximum(cur_start[g] - QB, 0), QB)
            starts.append((cur, prev))
            known = isinstance(prev_valid[g], bool)
            assert not known or prev_valid[g]
            biases.append((bias_cur, bias_prev if known else bias_prev + jnp.where(prev_valid[g], 0.0, NEG_BIG)))
        k_bd = [[jnp.concatenate([kblk * head_sel[0], kblk * head_sel[1]], axis=0)
                 for kblk in (kc[pl.ds(s, QB), :] for s in starts[g])] for g in groups]
        scores = [[lax.dot_general(qt[g], k_bd[g][t], (((1,), (1,)), ((), ())),
                                   preferred_element_type=F32) + biases[g][t] for t in range(2)]
                  for g in groups]
        return starts, scores

    def softmax_stage(vc, rows, starts, scores, fresh, final):
        if not fresh:
            old = [(m_sc[0, rows[g], :], m_sc[1, rows[g], :],
                    jnp.concatenate([acc_sc[0, rows[g], :], acc_sc[1, rows[g], :]], axis=1)) for g in groups]
        top = [jnp.maximum(scores[g][0], scores[g][1]) for g in groups]
        m_new = []
        for g in groups:
            pair = []
            for h in range(2):
                m_h = jnp.max(top[g][:, h * QB:(h + 1) * QB], axis=-1, keepdims=True)
                if not fresh:
                    m_h = jnp.maximum(old[g][h], m_h)
                pair.append(jnp.broadcast_to(m_h, (QB, QB)))
            m_new.append(pair)
        p = [[jnp.exp2(scores[g][t] - jnp.concatenate(m_new[g], axis=1)).astype(BF16) for t in range(2)]
             for g in groups]
        v_ext = [[jnp.concatenate([jnp.concatenate([vblk * head_sel[0], head_sel[0]], axis=1),
                                   jnp.concatenate([vblk * head_sel[1], head_sel[1]], axis=1)], axis=0)
                  for vblk in (vc[pl.ds(s, QB), :] for s in starts[g])] for g in groups]
        pv = [jnp.dot(p[g][0], v_ext[g][0], preferred_element_type=F32)
              + jnp.dot(p[g][1], v_ext[g][1], preferred_element_type=F32) for g in groups]
        if not fresh:
            alpha = [jnp.exp2(jnp.where(first_head, old[g][0] - m_new[g][0], old[g][1] - m_new[g][1]))
                     for g in groups]
            pv = [pv[g] + jnp.concatenate([alpha[g], alpha[g]], axis=1) * old[g][2] for g in groups]
        for g in groups:
            if final:
                o_ref[rows[g], :] = (pv[g][:, :LANES] / pv[g][:, LANES:] * g_ref[rows[g], :]).astype(BF16)
            else:
                m_sc[0, rows[g], :] = m_new[g][0]
                m_sc[1, rows[g], :] = m_new[g][1]
                acc_sc[0, rows[g], :] = pv[g][:, :LANES]
                acc_sc[1, rows[g], :] = pv[g][:, LANES:]

    blocks = SUPER // QB

    def pattern16(r):
        return pl.ds(r, QB, stride=16), r * (seq // 16) + span * QB, span > 0

    def pattern4(idx):
        r, ub = idx // 4, idx % 4
        return (pl.ds(ub * (4 * QB) + r, QB, stride=4), r * (seq // 4) + (span * 4 + ub) * QB,
                True if ub > 0 else span > 0)

    def pattern1(i):
        return pl.ds(i * QB, QB), (span * blocks + i) * QB, True if i > 0 else span > 0

    items = [(pattern, kc, vc, fresh, final, first)
             for pattern, kc, vc, fresh, final in ((pattern16, k16, v16, True, False),
                                                   (pattern4, k4, v4, False, False),
                                                   (pattern1, k1, v1, False, True))
             for first in range(0, blocks, GROUP)]

    def scores_of(item):
        pattern, kc, _, _, _, first = item
        rows, cur_start, prev_valid = zip(*[pattern(first + g) for g in groups])
        return (rows,) + score_stage(kc, rows, cur_start, prev_valid)

    ahead = scores_of(items[0])
    for k, item in enumerate(items):
        rows, starts, scores = ahead
        if k + 1 < len(items):
            ahead = scores_of(items[k + 1])
        softmax_stage(item[2], rows, starts, scores, item[3], item[4])


def _dilated_attention(qa, ka, va, ga, batch, seq):
    n = qa.shape[0]
    pairs = W_DIL // LANES
    spans = seq // SUPER
    qmap = lambda b, p, i: (b * spans + i, p)
    kvmap = lambda b, p, i: (b, p)
    copy = pltpu.VMEM((seq, LANES), BF16)
    return pl.pallas_call(
        _dilated_kernel,
        grid=(batch, pairs, spans),
        in_specs=[pl.BlockSpec((SUPER, LANES), qmap),
                  pl.BlockSpec((seq, LANES), kvmap),
                  pl.BlockSpec((seq, LANES), kvmap),
                  pl.BlockSpec((SUPER, LANES), qmap)],
        out_specs=pl.BlockSpec((SUPER, LANES), qmap),
        out_shape=jax.ShapeDtypeStruct((n, W_DIL), BF16),
        scratch_shapes=[copy] * 6 + [pltpu.VMEM((2, SUPER, LANES), F32)] * 2,
        compiler_params=_cparams(("parallel", "parallel", "arbitrary")),
        name="dilated_attention",
    )(qa, ka, va, ga)


def _suffix_count_matrix():
    j = np.arange(QB)[:, None]
    s = np.arange(QB)[None, :]
    u = (j > s).astype(np.float32)
    z = np.zeros_like(u)
    return np.block([[u, z], [z, u]])


def _log_keep_and_beta(z):
    log_keep = jnp.log(1.0 / (1.0 + jnp.exp(-jnp.abs(z)))) - jnp.maximum(z, 0.0)
    return log_keep, z + log_keep


SB_GROUP = 4


def _stick_kernel(q_ref, k_ref, v_ref, g_ref, u_ref, o_ref):
    step = pl.program_id(2)
    first_head = _head_masks()
    head_sel = (jnp.where(first_head, 1.0, 0.0).astype(BF16), jnp.where(first_head, 0.0, 1.0).astype(BF16))
    u = u_ref[...]
    row = lax.broadcasted_iota(jnp.int32, (QB, 2 * QB), 0)
    col = lax.broadcasted_iota(jnp.int32, (QB, 2 * QB), 1) % QB
    strictly_causal = col < row

    groups = range(SB_GROUP)

    def key_block(g, back):
        kb = step * SB_GROUP + g - back
        return kb, pl.multiple_of(jnp.maximum(kb, 0) * QB, QB)

    def scores(back):
        z = []
        for g in groups:
            kblk = k_ref[pl.ds(key_block(g, back)[1], QB), :]
            k_bd = jnp.concatenate([kblk * head_sel[0], kblk * head_sel[1]], axis=0)
            z.append(lax.dot_general(q_ref[g * QB:(g + 1) * QB, :], k_bd, (((1,), (1,)), ((), ())),
                                     preferred_element_type=F32))
        return tuple(z)

    def tiles(back, z, passed, acc, diagonal):
        def logs(g, st):
            st["kb"], st["start"] = key_block(g, back)
            log_keep, st["log_beta"] = _log_keep_and_beta(z[g])
            if diagonal:
                log_keep = jnp.where(strictly_causal, log_keep, 0.0)
            st["log_keep"] = log_keep
            hi = log_keep.astype(BF16)
            st["pieces"] = (hi, (log_keep - hi.astype(F32)).astype(BF16))

        def suffix(g, st):
            passed_g = passed[g]
            if not diagonal:
                passed_g = passed_g + jnp.where(st["kb"] >= 0, 0.0, NEG_BIG)
            hi, lo = st.pop("pieces")
            st["between"] = passed_g + (jnp.dot(hi, u, preferred_element_type=F32)
                                        + jnp.dot(lo, u, preferred_element_type=F32))

        def weights(g, st):
            w = jnp.exp(st.pop("log_beta") + st["between"])
            if diagonal:
                w = jnp.where(strictly_causal, w, 0.0)
            vblk = v_ref[pl.ds(st["start"], QB), :]
            v_bd = jnp.concatenate([vblk * head_sel[0], vblk * head_sel[1]], axis=0)
            st["acc"] = acc[g] + jnp.dot(w.astype(BF16), v_bd, preferred_element_type=F32)

        def carry(g, st):
            total = st.pop("between") + st.pop("log_keep")
            total = (total[:, 0:1], total[:, QB:QB + 1])
            st["passed"] = jnp.concatenate([jnp.broadcast_to(t, (QB, QB)) for t in total], axis=1)
            st["top"] = jnp.maximum(total[0], total[1])

        states = [{} for _ in groups]
        for stage in (logs, suffix, weights, carry):
            for g in groups:
                stage(g, states[g])
        top = functools.reduce(jnp.maximum, [st["top"] for st in states])
        return (tuple(st["passed"] for st in states), tuple(st["acc"] for st in states),
                jnp.max(top) > SB_DEAD_LOG)

    z_own = scores(0)
    z_ahead = scores(1)
    passed, acc, live = tiles(0, z_own, [jnp.zeros((QB, 2 * QB), F32)] * SB_GROUP,
                              [jnp.zeros((QB, LANES), F32)] * SB_GROUP, diagonal=True)

    def cond(c):
        back, live = c[:2]
        return jnp.logical_and(back <= step * SB_GROUP + SB_GROUP - 1, live)

    def body(c):
        back, _, z, passed, acc = c
        z_ahead = scores(back + 1)
        passed, acc, live = tiles(back, z, passed, acc, diagonal=False)
        return back + 1, live, z_ahead, passed, acc

    acc = lax.while_loop(cond, body, (1, live, z_ahead, passed, acc))[-1]
    for g in groups:
        rows = slice(g * QB, (g + 1) * QB)
        o_ref[rows, :] = (acc[g] * g_ref[rows, :]).astype(BF16)


def _stick_breaking_attention(qc, kc, vc, gc, u, batch, seq):
    n = qc.shape[0]
    pairs = W_SB // LANES
    rows = SB_GROUP * QB
    steps = seq // rows
    qmap = lambda b, p, i: (b * steps + i, p)
    kvmap = lambda b, p, i: (b, p)
    return pl.pallas_call(
        _stick_kernel,
        grid=(batch, pairs, steps),
        in_specs=[pl.BlockSpec((rows, LANES), qmap),
                  pl.BlockSpec((seq, LANES), kvmap),
                  pl.BlockSpec((seq, LANES), kvmap),
                  pl.BlockSpec((rows, LANES), qmap),
                  pl.BlockSpec(u.shape, lambda b, p, i: (0, 0))],
        out_specs=pl.BlockSpec((rows, LANES), qmap),
        out_shape=jax.ShapeDtypeStruct((n, W_SB), BF16),
        compiler_params=_cparams(("parallel", "parallel", "arbitrary")),
        name="stick_breaking_attention",
    )(qc, kc, vc, gc, u)


SUBLANES = 8
CONV_WIDTH = 4


def _rglru_kernel(x_ref, g_ref, cw_ref, cb_ref, wa_ref, ba_ref, wx_ref, bxg_ref, lam_ref, o_ref,
                  xbuf, a_sc, u_sc, h_sc, hcar):
    tc = x_ref.shape[0]

    @pl.when(pl.program_id(1) == 0)
    def _():
        xbuf[0:SUBLANES, :] = jnp.zeros((SUBLANES, W_LRU), F32)
        hcar[...] = jnp.zeros_like(hcar)

    x = x_ref[...]
    xbuf[SUBLANES:SUBLANES + tc, :] = x
    xc = x * cw_ref[CONV_WIDTH - 1:CONV_WIDTH, :] + cb_ref[...]
    for back in range(1, CONV_WIDTH):
        xc = xc + xbuf[SUBLANES - back:SUBLANES - back + tc, :] * cw_ref[CONV_WIDTH - 1 - back:CONV_WIDTH - back, :]
    xbuf[0:SUBLANES, :] = xbuf[tc:tc + SUBLANES, :]

    xcb = xc.astype(BF16)
    r = jax.nn.sigmoid(jnp.dot(xcb, wa_ref[...], preferred_element_type=F32) + ba_ref[...])
    i = jax.nn.sigmoid(jnp.dot(xcb, wx_ref[...], preferred_element_type=F32) + bxg_ref[...])
    lam = lam_ref[...]
    log_sig_lam = -(jnp.maximum(-lam, 0.0) + jnp.log1p(jnp.exp(-jnp.abs(lam))))
    log_a = LRU_C * r * log_sig_lam
    a = jnp.exp(log_a)
    a_sc[...] = a
    u_sc[...] = jnp.sqrt(jnp.tanh(-log_a) * (a * a + 1.0)) * (i * xc)

    rows = lax.broadcasted_iota(jnp.int32, (SUBLANES, W_LRU), 0)

    def step(j, h_prev):
        start = pl.multiple_of(j * SUBLANES, SUBLANES)
        a = a_sc[pl.ds(start, SUBLANES), :]
        b = u_sc[pl.ds(start, SUBLANES), :]
        for shift in (1, 2, 4):
            keep = rows >= shift
            a_prev = jnp.where(keep, pltpu.roll(a, shift, 0), 1.0)
            b_prev = jnp.where(keep, pltpu.roll(b, shift, 0), 0.0)
            b = a * b_prev + b
            a = a * a_prev
        h = a * h_prev + b
        h_sc[pl.ds(start, SUBLANES), :] = h
        return jnp.broadcast_to(h[SUBLANES - 1:SUBLANES, :], (SUBLANES, W_LRU))

    hcar[...] = lax.fori_loop(0, tc // SUBLANES, step, hcar[...])
    o_ref[...] = (h_sc[...] * g_ref[...]).astype(BF16)


def _rg_lru(bx, bg, cw, cb, wa, ba, wx, bxg, lam, batch, seq, tc):
    n = bx.shape[0]
    nchunks = seq // tc
    row = lambda b, c: (b * nchunks + c, 0)
    const = lambda b, c: (0, 0)
    return pl.pallas_call(
        _rglru_kernel,
        grid=(batch, nchunks),
        in_specs=[pl.BlockSpec((tc, W_LRU), row),
                  pl.BlockSpec((tc, W_LRU), row),
                  pl.BlockSpec((CONV_WIDTH, W_LRU), const),
                  pl.BlockSpec((1, W_LRU), const),
                  pl.BlockSpec((W_LRU, W_LRU), const),
                  pl.BlockSpec((1, W_LRU), const),
                  pl.BlockSpec((W_LRU, W_LRU), const),
                  pl.BlockSpec((1, W_LRU), const),
                  pl.BlockSpec((1, W_LRU), const)],
        out_specs=pl.BlockSpec((tc, W_LRU), row),
        out_shape=jax.ShapeDtypeStruct((n, W_LRU), BF16),
        scratch_shapes=[pltpu.VMEM((tc + SUBLANES, W_LRU), F32),
                        pltpu.VMEM((tc, W_LRU), F32),
                        pltpu.VMEM((tc, W_LRU), F32),
                        pltpu.VMEM((tc, W_LRU), F32),
                        pltpu.VMEM((SUBLANES, W_LRU), F32)],
        compiler_params=_cparams(("arbitrary", "arbitrary")),
        name="rg_lru",
    )(bx, bg, cw, cb, wa, ba, wx, bxg, lam)


def _outproj_kernel(x_ref, ya_ref, yb_ref, yc_ref, w_ref, fg_ref, o_ref, *, final_norm):
    acc = x_ref[...]
    acc = acc + jnp.dot(ya_ref[...], w_ref[0:W_DIL, :], preferred_element_type=F32)
    acc = acc + jnp.dot(yb_ref[...], w_ref[W_DIL:W_DIL + W_LRU, :], preferred_element_type=F32)
    acc = acc + jnp.dot(yc_ref[...], w_ref[W_DIL + W_LRU:, :], preferred_element_type=F32)
    if final_norm:
        ms = jnp.mean(acc * acc, axis=-1, keepdims=True)
        acc = (acc * lax.rsqrt(ms + EPS)) * fg_ref[...]
    o_ref[...] = acc


def _out_projection(h, ya, yb, yc, w_bf16, final_gain, final_norm, tm):
    n = h.shape[0]
    row = lambda i: (i, 0)
    const = lambda i: (0, 0)
    return pl.pallas_call(
        functools.partial(_outproj_kernel, final_norm=final_norm),
        grid=(n // tm,),
        in_specs=[pl.BlockSpec((tm, D_MODEL), row),
                  pl.BlockSpec((tm, W_DIL), row),
                  pl.BlockSpec((tm, W_LRU), row),
                  pl.BlockSpec((tm, W_SB), row),
                  pl.BlockSpec((D_MODEL, D_MODEL), const),
                  pl.BlockSpec((1, D_MODEL), const)],
        out_specs=pl.BlockSpec((tm, D_MODEL), row),
        out_shape=jax.ShapeDtypeStruct((n, D_MODEL), F32),
        compiler_params=_cparams(("parallel",)),
        name="out_projection",
    )(h, ya, yb, yc, w_bf16, final_gain)


def _rope_tables(seq):
    pos = jnp.arange(seq, dtype=F32)
    inv_freq = ROPE_THETA ** (-jnp.arange(0, ROPE_DIM, 2, dtype=F32) / ROPE_DIM)
    ang = pos[:, None] * inv_freq[None, :]
    cos, sin = jnp.cos(ang), jnp.sin(ang)
    half = ROPE_DIM // 2
    lane = np.arange(LANES) % HEAD_DIM
    freq = lane % half
    cos_l, sin_l = cos[:, freq], sin[:, freq]
    cos_t = jnp.where(lane < ROPE_DIM, cos_l, 1.0)
    s1_t = jnp.where(lane < half, -sin_l, 0.0)
    s2_t = jnp.where((lane >= half) & (lane < ROPE_DIM), sin_l, 0.0)
    return cos_t, s1_t, s2_t


def _block_diagonal(w):
    nb, d, _ = w.shape
    out = jnp.zeros((nb * d, nb * d), w.dtype)
    for i in range(nb):
        out = lax.dynamic_update_slice(out, w[i], (i * d, i * d))
    return out


def kernel(x, norm_gain, w_in, conv_w, conv_b, gate_a_w, gate_a_b, gate_x_w, gate_x_b, lru_lambda, w_out, final_gain):
    batch, seq, d = x.shape
    depth = w_in.shape[0]
    n = batch * seq
    cos_t, s1_t, s2_t = _rope_tables(seq)
    u = jnp.asarray(_suffix_count_matrix(), dtype=BF16)
    h = x.reshape(n, d)
    for l in range(depth):
        qa, ka, va, ga, bx, bg, qc, kc, vc, gc = _in_projection(
            h, norm_gain[l].reshape(1, d), w_in[l].astype(BF16), cos_t, s1_t, s2_t, seq, tm=256)
        ya = _dilated_attention(qa, ka, va, ga, batch, seq)
        yb = _rg_lru(bx, bg, conv_w[l], conv_b[l].reshape(1, W_LRU),
                     _block_diagonal(gate_a_w[l]).astype(BF16), gate_a_b[l].reshape(1, W_LRU),
                     _block_diagonal(gate_x_w[l]).astype(BF16), gate_x_b[l].reshape(1, W_LRU),
                     lru_lambda[l].reshape(1, W_LRU), batch, seq, tc=512)
        yc = _stick_breaking_attention(qc, kc, vc, gc, u, batch, seq)
        h = _out_projection(h, ya, yb, yc, w_out[l].astype(BF16), final_gain.reshape(1, d),
                            final_norm=(l == depth - 1), tm=512)
    return h.reshape(batch, seq, d)
```

```python
import functools
import math

import jax
import jax.numpy as jnp
import numpy as np
from jax import lax
from jax.experimental import pallas as pl
from jax.experimental.pallas import tpu as pltpu

F32 = jnp.float32
BF16 = jnp.bfloat16

D_MODEL = 1024
HEAD_DIM = 64
W_DIL = 512
W_LRU = 256
W_SB = 256
D_IN = 4 * W_DIL + 2 * W_LRU + 4 * W_SB
ROPE_DIM = 16
ROPE_THETA = 500000.0
DILATED_PATTERNS = ((128, 1), (512, 4), (2048, 16))
LRU_C = 8.0
EPS = 1e-6

LANES = 128
QB = 128
NEG_BIG = -1e30
SB_DEAD_LOG = -104.0

VMEM_LIMIT = 56 * 1024 * 1024


def _cparams(sem):
    return pltpu.CompilerParams(dimension_semantics=sem, vmem_limit_bytes=VMEM_LIMIT)


def _inproj_kernel(x_ref, gain_ref, w_ref, cos_ref, s1_ref, s2_ref,
                   qa_ref, ka_ref, va_ref, ga_ref, bx_ref, bg_ref,
                   qc_ref, kc_ref, vc_ref, gc_ref):
    x = x_ref[...]
    ms = jnp.mean(x * x, axis=-1, keepdims=True)
    xn = ((x * lax.rsqrt(ms + EPS)) * gain_ref[...]).astype(BF16)

    def proj(lo, width):
        return jnp.dot(xn, w_ref[:, lo:lo + width], preferred_element_type=F32)

    cos = cos_ref[...]
    s1 = s1_ref[...]
    s2 = s2_ref[...]

    def rope(t):
        parts = []
        for c in range(t.shape[1] // LANES):
            xc = t[:, c * LANES:(c + 1) * LANES]
            parts.append(xc * cos + pltpu.roll(xc, LANES - ROPE_DIM // 2, 1) * s1
                         + pltpu.roll(xc, ROPE_DIM // 2, 1) * s2)
        return jnp.concatenate(parts, axis=1)

    def silu(t):
        return t * jax.nn.sigmoid(t)

    scale = 1.0 / math.sqrt(HEAD_DIM)
    o = 0
    qa_ref[...] = rope(proj(o, W_DIL)) * (scale * math.log2(math.e)); o += W_DIL
    ka_ref[...] = rope(proj(o, W_DIL)); o += W_DIL
    va_ref[...] = proj(o, W_DIL); o += W_DIL
    ga_ref[...] = silu(proj(o, W_DIL)); o += W_DIL
    bx_ref[...] = proj(o, W_LRU); o += W_LRU
    bg_ref[...] = silu(proj(o, W_LRU)); o += W_LRU
    qc_ref[...] = (proj(o, W_SB) * scale).astype(BF16); o += W_SB
    kc_ref[...] = proj(o, W_SB).astype(BF16); o += W_SB
    vc_ref[...] = proj(o, W_SB).astype(BF16); o += W_SB
    gc_ref[...] = silu(proj(o, W_SB))


def _in_projection(h, gain, w_bf16, cos_t, s1_t, s2_t, seq, tm):
    n = h.shape[0]
    blocks_per_seq = seq // tm
    row = lambda i: (i, 0)
    tab = lambda i: (i % blocks_per_seq, 0)
    const = lambda i: (0, 0)
    widths = [(W_DIL, F32), (W_DIL, F32), (W_DIL, F32), (W_DIL, F32), (W_LRU, F32),
              (W_LRU, F32), (W_SB, BF16), (W_SB, BF16), (W_SB, BF16), (W_SB, F32)]
    return pl.pallas_call(
        _inproj_kernel,
        grid=(n // tm,),
        in_specs=[pl.BlockSpec((tm, D_MODEL), row),
                  pl.BlockSpec((1, D_MODEL), const),
                  pl.BlockSpec((D_MODEL, D_IN), const),
                  pl.BlockSpec((tm, LANES), tab),
                  pl.BlockSpec((tm, LANES), tab),
                  pl.BlockSpec((tm, LANES), tab)],
        out_specs=[pl.BlockSpec((tm, w), row) for w, _ in widths],
        out_shape=[jax.ShapeDtypeStruct((n, w), dt) for w, dt in widths],
        compiler_params=_cparams(("parallel",)),
        name="in_projection",
    )(h, gain, w_bf16, cos_t, s1_t, s2_t)


def _head_masks():
    lane = lax.broadcasted_iota(jnp.int32, (QB, LANES), 1)
    return lane < HEAD_DIM


SUPER = DILATED_PATTERNS[-1][0]
GROUP = 2


def _dilated_kernel(q_ref, k_ref, v_ref, g_ref, o_ref, k1, v1, k4, v4, k16, v16, stage, m_sc, acc_sc):
    span = pl.program_id(2)
    seq = k_ref.shape[0]
    first_head = _head_masks()

    @pl.when(span == 0)
    def _():
        quarter = seq // 4

        def split(j, tiles_per_residue):
            bits = tiles_per_residue.bit_length() - 1
            assert tiles_per_residue == 1 << bits
            return lax.shift_right_logical(j, bits), lax.bitwise_and(j, tiles_per_residue - 1)

        for src_ref, c1, c4, c16 in ((k_ref, k1, k4, k16), (v_ref, v1, v4, v16)):
            def by_four(j, carry):
                dst = pl.ds(pl.multiple_of(j * QB, QB), QB)
                r, tile = split(j, quarter // QB)
                u0 = tile * QB
                regrouped = src_ref[pl.ds(4 * u0 + r, QB, stride=4), :]
                c1[dst, :] = src_ref[dst, :].astype(BF16)
                stage[dst, :] = regrouped
                c4[dst, :] = regrouped.astype(BF16)
                return carry
            lax.fori_loop(0, seq // QB, by_four, 0, unroll=4)

            def by_sixteen(j, carry):
                dst = pl.ds(pl.multiple_of(j * QB, QB), QB)
                r16, tile = split(j, seq // 16 // QB)
                high, low = split(r16, 4)
                src = pl.ds(low * quarter + high + 4 * QB * tile, QB, stride=4)
                c16[dst, :] = stage[src, :].astype(BF16)
                return carry
            lax.fori_loop(0, seq // QB, by_sixteen, 0, unroll=4)

    row = lax.broadcasted_iota(jnp.int32, (QB, 2 * QB), 0)
    col = lax.broadcasted_iota(jnp.int32, (QB, 2 * QB), 1) % QB
    bias_cur = jnp.where(col <= row, 0.0, NEG_BIG)
    bias_prev = jnp.where(col >= row, 0.0, NEG_BIG)
    head_sel = (jnp.where(first_head, 1.0, 0.0).astype(BF16), jnp.where(first_head, 0.0, 1.0).astype(BF16))

    groups = range(GROUP)

    def score_stage(kc, rows, cur_start, prev_valid):
        qt = [q_ref[rows[g], :].astype(BF16) for g in groups]
        starts, biases = [], []
        for g in groups:
            cur = pl.multiple_of(cur_start[g], QB)
            prev = pl.multiple_of(jnp.maximum(cur_start[g] - QB, 0), QB)
            starts.append((cur, prev))
            known = isinstance(prev_valid[g], bool)
            assert not known or prev_valid[g]
            biases.append((bias_cur, bias_prev if known else bias_prev + jnp.where(prev_valid[g], 0.0, NEG_BIG)))
        k_bd = [[jnp.concatenate([kblk * head_sel[0], kblk * head_sel[1]], axis=0)
                 for kblk in (kc[pl.ds(s, QB), :] for s in starts[g])] for g in groups]
        scores = [[lax.dot_general(qt[g], k_bd[g][t], (((1,), (1,)), ((), ())),
                                   preferred_element_type=F32) + biases[g][t] for t in range(2)]
                  for g in groups]
        return starts, scores

    def softmax_stage(vc, rows, starts, scores, fresh, final):
        if not fresh:
            old = [(m_sc[0, rows[g], :], m_sc[1, rows[g], :],
                    jnp.concatenate([acc_sc[0, rows[g], :], acc_sc[1, rows[g], :]], axis=1)) for g in groups]
        top = [jnp.maximum(scores[g][0], scores[g][1]) for g in groups]
        m_new = []
        for g in groups:
            pair = []
            for h in range(2):
                m_h = jnp.max(top[g][:, h * QB:(h + 1) * QB], axis=-1, keepdims=True)
                if not fresh:
                    m_h = jnp.maximum(old[g][h], m_h)
                pair.append(jnp.broadcast_to(m_h, (QB, QB)))
            m_new.append(pair)
        p = [[jnp.exp2(scores[g][t] - jnp.concatenate(m_new[g], axis=1)).astype(BF16) for t in range(2)]
             for g in groups]
        v_ext = [[jnp.concatenate([jnp.concatenate([vblk * head_sel[0], head_sel[0]], axis=1),
                                   jnp.concatenate([vblk * head_sel[1], head_sel[1]], axis=1)], axis=0)
                  for vblk in (vc[pl.ds(s, QB), :] for s in starts[g])] for g in groups]
        pv = [jnp.dot(p[g][0], v_ext[g][0], preferred_element_type=F32)
              + jnp.dot(p[g][1], v_ext[g][1], preferred_element_type=F32) for g in groups]
        if not fresh:
            alpha = [jnp.exp2(jnp.where(first_head, old[g][0] - m_new[g][0], old[g][1] - m_new[g][1]))
                     for g in groups]
            pv = [pv[g] + jnp.concatenate([alpha[g], alpha[g]], axis=1) * old[g][2] for g in groups]
        for g in groups:
            if final:
                o_ref[rows[g], :] = (pv[g][:, :LANES] / pv[g][:, LANES:] * g_ref[rows[g], :]).astype(BF16)
            else:
                m_sc[0, rows[g], :] = m_new[g][0]
                m_sc[1, rows[g], :] = m_new[g][1]
                acc_sc[0, rows[g], :] = pv[g][:, :LANES]
                acc_sc[1, rows[g], :] = pv[g][:, LANES:]

    blocks = SUPER // QB

    def pattern16(r):
        return pl.ds(r, QB, stride=16), r * (seq // 16) + span * QB, span > 0

    def pattern4(idx):
        r, ub = idx // 4, idx % 4
        return (pl.ds(ub * (4 * QB) + r, QB, stride=4), r * (seq // 4) + (span * 4 + ub) * QB,
                True if ub > 0 else span > 0)

    def pattern1(i):
        return pl.ds(i * QB, QB), (span * blocks + i) * QB, True if i > 0 else span > 0

    items = [(pattern, kc, vc, fresh, final, first)
             for pattern, kc, vc, fresh, final in ((pattern16, k16, v16, True, False),
                                                   (pattern4, k4, v4, False, False),
                                                   (pattern1, k1, v1, False, True))
             for first in range(0, blocks, GROUP)]

    def scores_of(item):
        pattern, kc, _, _, _, first = item
        rows, cur_start, prev_valid = zip(*[pattern(first + g) for g in groups])
        return (rows,) + score_stage(kc, rows, cur_start, prev_valid)

    ahead = scores_of(items[0])
    for k, item in enumerate(items):
        rows, starts, scores = ahead
        if k + 1 < len(items):
            ahead = scores_of(items[k + 1])
        softmax_stage(item[2], rows, starts, scores, item[3], item[4])


def _dilated_attention(qa, ka, va, ga, batch, seq):
    n = qa.shape[0]
    pairs = W_DIL // LANES
    spans = seq // SUPER
    qmap = lambda b, p, i: (b * spans + i, p)
    kvmap = lambda b, p, i: (b, p)
    copy = pltpu.VMEM((seq, LANES), BF16)
    return pl.pallas_call(
        _dilated_kernel,
        grid=(batch, pairs, spans),
        in_specs=[pl.BlockSpec((SUPER, LANES), qmap),
                  pl.BlockSpec((seq, LANES), kvmap),
                  pl.BlockSpec((seq, LANES), kvmap),
                  pl.BlockSpec((SUPER, LANES), qmap)],
        out_specs=pl.BlockSpec((SUPER, LANES), qmap),
        out_shape=jax.ShapeDtypeStruct((n, W_DIL), BF16),
        scratch_shapes=[copy] * 6 + [pltpu.VMEM((seq, LANES), F32)] + [pltpu.VMEM((2, SUPER, LANES), F32)] * 2,
        compiler_params=_cparams(("parallel", "parallel", "arbitrary")),
        name="dilated_attention",
    )(qa, ka, va, ga)


def _suffix_count_matrix():
    j = np.arange(QB)[:, None]
    s = np.arange(QB)[None, :]
    u = (j > s).astype(np.float32)
    z = np.zeros_like(u)
    return np.block([[u, z], [z, u]])


def _log_keep_and_beta(z):
    log_keep = jnp.log(1.0 / (1.0 + jnp.exp(-jnp.abs(z)))) - jnp.maximum(z, 0.0)
    return log_keep, z + log_keep


SB_GROUP = 8


def _stick_kernel(q_ref, k_ref, v_ref, g_ref, u_ref, o_ref):
    step = pl.program_id(2)
    first_head = _head_masks()
    head_sel = (jnp.where(first_head, 1.0, 0.0).astype(BF16), jnp.where(first_head, 0.0, 1.0).astype(BF16))
    u = u_ref[...]
    row = lax.broadcasted_iota(jnp.int32, (QB, 2 * QB), 0)
    col = lax.broadcasted_iota(jnp.int32, (QB, 2 * QB), 1) % QB
    strictly_causal = col < row

    groups = range(SB_GROUP)

    def key_block(g, back):
        kb = step * SB_GROUP + g - back
        return kb, pl.multiple_of(jnp.maximum(kb, 0) * QB, QB)


    def scores(back):
        z = []
        for g in groups:
            kblk = k_ref[pl.ds(key_block(g, back)[1], QB), :]
            k_bd = jnp.concatenate([kblk * head_sel[0], kblk * head_sel[1]], axis=0)
            z.append(lax.dot_general(q_ref[g * QB:(g + 1) * QB, :], k_bd, (((1,), (1,)), ((), ())),
                                     preferred_element_type=F32))
        return z

    def logs(z, own_block):
        parts = []
        for g in groups:
            log_keep, log_beta = _log_keep_and_beta(z[g])
            if own_block:
                log_keep = jnp.where(strictly_causal, log_keep, 0.0)
            hi = log_keep.astype(BF16)
            parts.append((log_keep, log_beta, hi, (log_keep - hi.astype(F32)).astype(BF16)))
        return parts

    def fold(back, parts, passed, acc, own_block):
        between, new_acc, new_passed, tops = [], [], [], []
        for g in groups:
            passed_g = passed[g]
            if not own_block:
                passed_g = passed_g + jnp.where(key_block(g, back)[0] >= 0, 0.0, NEG_BIG)
            between.append(passed_g + (jnp.dot(parts[g][2], u, preferred_element_type=F32)
                                       + jnp.dot(parts[g][3], u, preferred_element_type=F32)))
        for g in groups:
            w = jnp.exp(parts[g][1] + between[g])
            if own_block:
                w = jnp.where(strictly_causal, w, 0.0)
            vblk = v_ref[pl.ds(key_block(g, back)[1], QB), :]
            v_bd = jnp.concatenate([vblk * head_sel[0], vblk * head_sel[1]], axis=0)
            new_acc.append(acc[g] + jnp.dot(w.astype(BF16), v_bd, preferred_element_type=F32))
        for g in groups:
            total = between[g] + parts[g][0]
            total = (total[:, 0:1], total[:, QB:QB + 1])
            new_passed.append(jnp.concatenate([jnp.broadcast_to(t, (QB, QB)) for t in total], axis=1))
            tops.append(jnp.maximum(total[0], total[1]))
        live = jnp.max(functools.reduce(jnp.maximum, tops)) > SB_DEAD_LOG
        return tuple(new_passed), tuple(new_acc), live

    z_own, z_prev = scores(0), scores(1)
    parts_own, parts_prev = logs(z_own, True), logs(z_prev, False)
    passed, acc, _ = fold(0, parts_own, [jnp.zeros((QB, 2 * QB), F32)] * SB_GROUP,
                          [jnp.zeros((QB, LANES), F32)] * SB_GROUP, True)
    passed, acc, live = fold(1, parts_prev, passed, acc, False)

    def cond(c):
        back, live = c[:2]
        return jnp.logical_and(back <= step * SB_GROUP + SB_GROUP - 1, live)

    def body(c):
        back, _, passed, acc = c
        passed, acc, live = fold(back, logs(scores(back), False), passed, acc, False)
        return back + 1, live, passed, acc

    acc = lax.while_loop(cond, body, (2, live, passed, acc))[-1]
    for g in groups:
        rows = slice(g * QB, (g + 1) * QB)
        o_ref[rows, :] = (acc[g] * g_ref[rows, :]).astype(BF16)


def _stick_breaking_attention(qc, kc, vc, gc, u, batch, seq):
    n = qc.shape[0]
    pairs = W_SB // LANES
    rows = SB_GROUP * QB
    steps = seq // rows
    qmap = lambda b, p, i: (b * steps + i, p)
    kvmap = lambda b, p, i: (b, p)
    return pl.pallas_call(
        _stick_kernel,
        grid=(batch, pairs, steps),
        in_specs=[pl.BlockSpec((rows, LANES), qmap),
                  pl.BlockSpec((seq, LANES), kvmap),
                  pl.BlockSpec((seq, LANES), kvmap),
                  pl.BlockSpec((rows, LANES), qmap),
                  pl.BlockSpec(u.shape, lambda b, p, i: (0, 0))],
        out_specs=pl.BlockSpec((rows, LANES), qmap),
        out_shape=jax.ShapeDtypeStruct((n, W_SB), BF16),
        compiler_params=_cparams(("parallel", "parallel", "arbitrary")),
        name="stick_breaking_attention",
    )(qc, kc, vc, gc, u)


SUBLANES = 8
CONV_WIDTH = 4


def _rglru_kernel(x_ref, g_ref, cw_ref, cb_ref, wa_ref, ba_ref, wx_ref, bxg_ref, lam_ref, o_ref,
                  xbuf, a_sc, u_sc, h_sc, hcar):
    tc = x_ref.shape[0]

    @pl.when(pl.program_id(1) == 0)
    def _():
        xbuf[0:SUBLANES, :] = jnp.zeros((SUBLANES, W_LRU), F32)
        hcar[...] = jnp.zeros_like(hcar)

    x = x_ref[...]
    xbuf[SUBLANES:SUBLANES + tc, :] = x
    xc = x * cw_ref[CONV_WIDTH - 1:CONV_WIDTH, :] + cb_ref[...]
    for back in range(1, CONV_WIDTH):
        xc = xc + xbuf[SUBLANES - back:SUBLANES - back + tc, :] * cw_ref[CONV_WIDTH - 1 - back:CONV_WIDTH - back, :]
    xbuf[0:SUBLANES, :] = xbuf[tc:tc + SUBLANES, :]

    xcb = xc.astype(BF16)
    r = jax.nn.sigmoid(jnp.dot(xcb, wa_ref[...], preferred_element_type=F32) + ba_ref[...])
    i = jax.nn.sigmoid(jnp.dot(xcb, wx_ref[...], preferred_element_type=F32) + bxg_ref[...])
    lam = lam_ref[...]
    log_sig_lam = -(jnp.maximum(-lam, 0.0) + jnp.log1p(jnp.exp(-jnp.abs(lam))))
    log_a = LRU_C * r * log_sig_lam
    a = jnp.exp(log_a)
    a_sc[...] = a
    u_sc[...] = jnp.sqrt(jnp.tanh(-log_a) * (a * a + 1.0)) * (i * xc)

    rows = lax.broadcasted_iota(jnp.int32, (SUBLANES, W_LRU), 0)

    def step(j, h_prev):
        start = pl.multiple_of(j * SUBLANES, SUBLANES)
        a = a_sc[pl.ds(start, SUBLANES), :]
        b = u_sc[pl.ds(start, SUBLANES), :]
        for shift in (1, 2, 4):
            keep = rows >= shift
            a_prev = jnp.where(keep, pltpu.roll(a, shift, 0), 1.0)
            b_prev = jnp.where(keep, pltpu.roll(b, shift, 0), 0.0)
            b = a * b_prev + b
            a = a * a_prev
        h = a * h_prev + b
        h_sc[pl.ds(start, SUBLANES), :] = h
        return jnp.broadcast_to(h[SUBLANES - 1:SUBLANES, :], (SUBLANES, W_LRU))

    hcar[...] = lax.fori_loop(0, tc // SUBLANES, step, hcar[...])
    o_ref[...] = (h_sc[...] * g_ref[...]).astype(BF16)


def _rg_lru(bx, bg, cw, cb, wa, ba, wx, bxg, lam, batch, seq, tc):
    n = bx.shape[0]
    nchunks = seq // tc
    row = lambda b, c: (b * nchunks + c, 0)
    const = lambda b, c: (0, 0)
    return pl.pallas_call(
        _rglru_kernel,
        grid=(batch, nchunks),
        in_specs=[pl.BlockSpec((tc, W_LRU), row),
                  pl.BlockSpec((tc, W_LRU), row),
                  pl.BlockSpec((CONV_WIDTH, W_LRU), const),
                  pl.BlockSpec((1, W_LRU), const),
                  pl.BlockSpec((W_LRU, W_LRU), const),
                  pl.BlockSpec((1, W_LRU), const),
                  pl.BlockSpec((W_LRU, W_LRU), const),
                  pl.BlockSpec((1, W_LRU), const),
                  pl.BlockSpec((1, W_LRU), const)],
        out_specs=pl.BlockSpec((tc, W_LRU), row),
        out_shape=jax.ShapeDtypeStruct((n, W_LRU), BF16),
        scratch_shapes=[pltpu.VMEM((tc + SUBLANES, W_LRU), F32),
                        pltpu.VMEM((tc, W_LRU), F32),
                        pltpu.VMEM((tc, W_LRU), F32),
                        pltpu.VMEM((tc, W_LRU), F32),
                        pltpu.VMEM((SUBLANES, W_LRU), F32)],
        compiler_params=_cparams(("arbitrary", "arbitrary")),
        name="rg_lru",
    )(bx, bg, cw, cb, wa, ba, wx, bxg, lam)


def _outproj_kernel(x_ref, ya_ref, yb_ref, yc_ref, w_ref, fg_ref, o_ref, *, final_norm):
    acc = x_ref[...]
    acc = acc + jnp.dot(ya_ref[...], w_ref[0:W_DIL, :], preferred_element_type=F32)
    acc = acc + jnp.dot(yb_ref[...], w_ref[W_DIL:W_DIL + W_LRU, :], preferred_element_type=F32)
    acc = acc + jnp.dot(yc_ref[...], w_ref[W_DIL + W_LRU:, :], preferred_element_type=F32)
    if final_norm:
        ms = jnp.mean(acc * acc, axis=-1, keepdims=True)
        acc = (acc * lax.rsqrt(ms + EPS)) * fg_ref[...]
    o_ref[...] = acc


def _out_projection(h, ya, yb, yc, w_bf16, final_gain, final_norm, tm):
    n = h.shape[0]
    row = lambda i: (i, 0)
    const = lambda i: (0, 0)
    return pl.pallas_call(
        functools.partial(_outproj_kernel, final_norm=final_norm),
        grid=(n // tm,),
        in_specs=[pl.BlockSpec((tm, D_MODEL), row),
                  pl.BlockSpec((tm, W_DIL), row),
                  pl.BlockSpec((tm, W_LRU), row),
                  pl.BlockSpec((tm, W_SB), row),
                  pl.BlockSpec((D_MODEL, D_MODEL), const),
                  pl.BlockSpec((1, D_MODEL), const)],
        out_specs=pl.BlockSpec((tm, D_MODEL), row),
        out_shape=jax.ShapeDtypeStruct((n, D_MODEL), F32),
        compiler_params=_cparams(("parallel",)),
        name="out_projection",
    )(h, ya, yb, yc, w_bf16, final_gain)


def _rope_tables(seq):
    pos = jnp.arange(seq, dtype=F32)
    inv_freq = ROPE_THETA ** (-jnp.arange(0, ROPE_DIM, 2, dtype=F32) / ROPE_DIM)
    ang = pos[:, None] * inv_freq[None, :]
    cos, sin = jnp.cos(ang), jnp.sin(ang)
    half = ROPE_DIM // 2
    lane = np.arange(LANES) % HEAD_DIM
    freq = lane % half
    cos_l, sin_l = cos[:, freq], sin[:, freq]
    cos_t = jnp.where(lane < ROPE_DIM, cos_l, 1.0)
    s1_t = jnp.where(lane < half, -sin_l, 0.0)
    s2_t = jnp.where((lane >= half) & (lane < ROPE_DIM), sin_l, 0.0)
    return cos_t, s1_t, s2_t


def _block_diagonal(w):
    nb, d, _ = w.shape
    out = jnp.zeros((nb * d, nb * d), w.dtype)
    for i in range(nb):
        out = lax.dynamic_update_slice(out, w[i], (i * d, i * d))
    return out


def kernel(x, norm_gain, w_in, conv_w, conv_b, gate_a_w, gate_a_b, gate_x_w, gate_x_b, lru_lambda, w_out, final_gain):
    batch, seq, d = x.shape
    depth = w_in.shape[0]
    n = batch * seq
    cos_t, s1_t, s2_t = _rope_tables(seq)
    u = jnp.asarray(_suffix_count_matrix(), dtype=BF16)
    h = x.reshape(n, d)
    for l in range(depth):
        qa, ka, va, ga, bx, bg, qc, kc, vc, gc = _in_projection(
            h, norm_gain[l].reshape(1, d), w_in[l].astype(BF16), cos_t, s1_t, s2_t, seq, tm=256)
        ya = _dilated_attention(qa, ka, va, ga, batch, seq)
        yb = _rg_lru(bx, bg, conv_w[l], conv_b[l].reshape(1, W_LRU),
                     _block_diagonal(gate_a_w[l]).astype(BF16), gate_a_b[l].reshape(1, W_LRU),
                     _block_diagonal(gate_x_w[l]).astype(BF16), gate_x_b[l].reshape(1, W_LRU),
                     lru_lambda[l].reshape(1, W_LRU), batch, seq, tc=512)
        yc = _stick_breaking_attention(qc, kc, vc, gc, u, batch, seq)
        h = _out_projection(h, ya, yb, yc, w_out[l].astype(BF16), final_gain.reshape(1, d),
                            final_norm=(l == depth - 1), tm=512)
    return h.reshape(batch, seq, d)
```

```python
import functools
import math

import jax
import jax.numpy as jnp
import numpy as np
from jax import lax
from jax.experimental import pallas as pl
from jax.experimental.pallas import tpu as pltpu

F32 = jnp.float32
BF16 = jnp.bfloat16

D_MODEL = 1024
HEAD_DIM = 64
W_DIL = 512
W_LRU = 256
W_SB = 256
D_IN = 4 * W_DIL + 2 * W_LRU + 4 * W_SB
ROPE_DIM = 16
ROPE_THETA = 500000.0
DILATED_PATTERNS = ((128, 1), (512, 4), (2048, 16))
LRU_C = 8.0
EPS = 1e-6

LANES = 128
QB = 128
NEG_BIG = -1e30
SB_DEAD_LOG = -104.0

VMEM_LIMIT = 56 * 1024 * 1024


def _cparams(sem):
    return pltpu.CompilerParams(dimension_semantics=sem, vmem_limit_bytes=VMEM_LIMIT)


def _inproj_kernel(x_ref, gain_ref, w_ref, cos_ref, s1_ref, s2_ref,
                   qa_ref, ka_ref, va_ref, ga_ref, bx_ref, bg_ref,
                   qc_ref, kc_ref, vc_ref, gc_ref):
    x = x_ref[...]
    ms = jnp.mean(x * x, axis=-1, keepdims=True)
    xn = ((x * lax.rsqrt(ms + EPS)) * gain_ref[...]).astype(BF16)

    def proj(lo, width):
        return jnp.dot(xn, w_ref[:, lo:lo + width], preferred_element_type=F32)

    cos = cos_ref[...]
    s1 = s1_ref[...]
    s2 = s2_ref[...]

    def rope(t):
        parts = []
        for c in range(t.shape[1] // LANES):
            xc = t[:, c * LANES:(c + 1) * LANES]
            parts.append(xc * cos + pltpu.roll(xc, LANES - ROPE_DIM // 2, 1) * s1
                         + pltpu.roll(xc, ROPE_DIM // 2, 1) * s2)
        return jnp.concatenate(parts, axis=1)

    def silu(t):
        return t * jax.nn.sigmoid(t)

    scale = 1.0 / math.sqrt(HEAD_DIM)
    o = 0
    qa_ref[...] = rope(proj(o, W_DIL)) * (scale * math.log2(math.e)); o += W_DIL
    ka_ref[...] = rope(proj(o, W_DIL)); o += W_DIL
    va_ref[...] = proj(o, W_DIL); o += W_DIL
    ga_ref[...] = silu(proj(o, W_DIL)); o += W_DIL
    bx_ref[...] = proj(o, W_LRU); o += W_LRU
    bg_ref[...] = silu(proj(o, W_LRU)); o += W_LRU
    qc_ref[...] = (proj(o, W_SB) * scale).astype(BF16); o += W_SB
    kc_ref[...] = proj(o, W_SB).astype(BF16); o += W_SB
    vc_ref[...] = proj(o, W_SB).astype(BF16); o += W_SB
    gc_ref[...] = silu(proj(o, W_SB))


def _in_projection(h, gain, w_bf16, cos_t, s1_t, s2_t, seq, tm):
    n = h.shape[0]
    blocks_per_seq = seq // tm
    row = lambda i: (i, 0)
    tab = lambda i: (i % blocks_per_seq, 0)
    const = lambda i: (0, 0)
    widths = [(W_DIL, F32), (W_DIL, F32), (W_DIL, F32), (W_DIL, F32), (W_LRU, F32),
              (W_LRU, F32), (W_SB, BF16), (W_SB, BF16), (W_SB, BF16), (W_SB, F32)]
    return pl.pallas_call(
        _inproj_kernel,
        grid=(n // tm,),
        in_specs=[pl.BlockSpec((tm, D_MODEL), row),
                  pl.BlockSpec((1, D_MODEL), const),
                  pl.BlockSpec((D_MODEL, D_IN), const),
                  pl.BlockSpec((tm, LANES), tab),
                  pl.BlockSpec((tm, LANES), tab),
                  pl.BlockSpec((tm, LANES), tab)],
        out_specs=[pl.BlockSpec((tm, w), row) for w, _ in widths],
        out_shape=[jax.ShapeDtypeStruct((n, w), dt) for w, dt in widths],
        compiler_params=_cparams(("parallel",)),
        name="in_projection",
    )(h, gain, w_bf16, cos_t, s1_t, s2_t)


def _head_masks():
    lane = lax.broadcasted_iota(jnp.int32, (QB, LANES), 1)
    return lane < HEAD_DIM


SUPER = DILATED_PATTERNS[-1][0]
GROUP = 2


def _dilated_kernel(q_ref, k_ref, v_ref, g_ref, o_ref, k1, v1, k4, v4, k16, v16, stage, m_sc, acc_sc):
    span = pl.program_id(2)
    seq = k_ref.shape[0]
    first_head = _head_masks()

    @pl.when(span == 0)
    def _():
        quarter = seq // 4

        def split(j, tiles_per_residue):
            bits = tiles_per_residue.bit_length() - 1
            assert tiles_per_residue == 1 << bits
            return lax.shift_right_logical(j, bits), lax.bitwise_and(j, tiles_per_residue - 1)

        for src_ref, c1, c4, c16 in ((k_ref, k1, k4, k16), (v_ref, v1, v4, v16)):
            def by_four(j, carry):
                dst = pl.ds(pl.multiple_of(j * QB, QB), QB)
                r, tile = split(j, quarter // QB)
                u0 = tile * QB
                regrouped = src_ref[pl.ds(4 * u0 + r, QB, stride=4), :]
                c1[dst, :] = src_ref[dst, :].astype(BF16)
                stage[dst, :] = regrouped
                c4[dst, :] = regrouped.astype(BF16)
                return carry
            lax.fori_loop(0, seq // QB, by_four, 0, unroll=4)

            def by_sixteen(j, carry):
                dst = pl.ds(pl.multiple_of(j * QB, QB), QB)
                r16, tile = split(j, seq // 16 // QB)
                high, low = split(r16, 4)
                src = pl.ds(low * quarter + high + 4 * QB * tile, QB, stride=4)
                c16[dst, :] = stage[src, :].astype(BF16)
                return carry
            lax.fori_loop(0, seq // QB, by_sixteen, 0, unroll=4)

    row = lax.broadcasted_iota(jnp.int32, (QB, 2 * QB), 0)
    col = lax.broadcasted_iota(jnp.int32, (QB, 2 * QB), 1) % QB
    bias_cur = jnp.where(col <= row, 0.0, NEG_BIG)
    bias_prev = jnp.where(col >= row, 0.0, NEG_BIG)
    head_sel = (jnp.where(first_head, 1.0, 0.0).astype(BF16), jnp.where(first_head, 0.0, 1.0).astype(BF16))

    groups = range(GROUP)

    def score_stage(kc, rows, cur_start, prev_valid):
        qt = [q_ref[rows[g], :].astype(BF16) for g in groups]
        starts, biases = [], []
        for g in groups:
            cur = pl.multiple_of(cur_start[g], QB)
            prev = pl.multiple_of(jnp.maximum(cur_start[g] - QB, 0), QB)
            starts.append((cur, prev))
            known = isinstance(prev_valid[g], bool)
            assert not known or prev_valid[g]
            biases.append((bias_cur, bias_prev if known else bias_prev + jnp.where(prev_valid[g], 0.0, NEG_BIG)))
        k_bd = [[jnp.concatenate([kblk * head_sel[0], kblk * head_sel[1]], axis=0)
                 for kblk in (kc[pl.ds(s, QB), :] for s in starts[g])] for g in groups]
        scores = [[lax.dot_general(qt[g], k_bd[g][t], (((1,), (1,)), ((), ())),
                                   preferred_element_type=F32) + biases[g][t] for t in range(2)]
                  for g in groups]
        return starts, scores

    def softmax_stage(vc, rows, starts, scores, fresh, final):
        if not fresh:
            old = [(m_sc[0, rows[g], :], m_sc[1, rows[g], :],
                    jnp.concatenate([acc_sc[0, rows[g], :], acc_sc[1, rows[g], :]], axis=1)) for g in groups]
        top = [jnp.maximum(scores[g][0], scores[g][1]) for g in groups]
        m_new = []
        for g in groups:
            pair = []
            for h in range(2):
                m_h = jnp.max(top[g][:, h * QB:(h + 1) * QB], axis=-1, keepdims=True)
                if not fresh:
                    m_h = jnp.maximum(old[g][h], m_h)
                pair.append(jnp.broadcast_to(m_h, (QB, QB)))
            m_new.append(pair)
        p = [[jnp.exp2(scores[g][t] - jnp.concatenate(m_new[g], axis=1)).astype(BF16) for t in range(2)]
             for g in groups]
        v_ext = [[jnp.concatenate([jnp.concatenate([vblk * head_sel[0], head_sel[0]], axis=1),
                                   jnp.concatenate([vblk * head_sel[1], head_sel[1]], axis=1)], axis=0)
                  for vblk in (vc[pl.ds(s, QB), :] for s in starts[g])] for g in groups]
        pv = [jnp.dot(p[g][0], v_ext[g][0], preferred_element_type=F32)
              + jnp.dot(p[g][1], v_ext[g][1], preferred_element_type=F32) for g in groups]
        if not fresh:
            alpha = [jnp.exp2(jnp.where(first_head, old[g][0] - m_new[g][0], old[g][1] - m_new[g][1]))
                     for g in groups]
            pv = [pv[g] + jnp.concatenate([alpha[g], alpha[g]], axis=1) * old[g][2] for g in groups]
        for g in groups:
            if final:
                o_ref[rows[g], :] = (pv[g][:, :LANES] / pv[g][:, LANES:] * g_ref[rows[g], :]).astype(BF16)
            else:
                m_sc[0, rows[g], :] = m_new[g][0]
                m_sc[1, rows[g], :] = m_new[g][1]
                acc_sc[0, rows[g], :] = pv[g][:, :LANES]
                acc_sc[1, rows[g], :] = pv[g][:, LANES:]

    blocks = SUPER // QB

    def pattern16(r):
        return pl.ds(r, QB, stride=16), r * (seq // 16) + span * QB, span > 0

    def pattern4(idx):
        r, ub = idx // 4, idx % 4
        return (pl.ds(ub * (4 * QB) + r, QB, stride=4), r * (seq // 4) + (span * 4 + ub) * QB,
                True if ub > 0 else span > 0)

    def pattern1(i):
        return pl.ds(i * QB, QB), (span * blocks + i) * QB, True if i > 0 else span > 0

    items = [(pattern, kc, vc, fresh, final, first)
             for pattern, kc, vc, fresh, final in ((pattern16, k16, v16, True, False),
                                                   (pattern4, k4, v4, False, False),
                                                   (pattern1, k1, v1, False, True))
             for first in range(0, blocks, GROUP)]

    def scores_of(item):
        pattern, kc, _, _, _, first = item
        rows, cur_start, prev_valid = zip(*[pattern(first + g) for g in groups])
        return (rows,) + score_stage(kc, rows, cur_start, prev_valid)

    ahead = scores_of(items[0])
    for k, item in enumerate(items):
        rows, starts, scores = ahead
        if k + 1 < len(items):
            ahead = scores_of(items[k + 1])
        softmax_stage(item[2], rows, starts, scores, item[3], item[4])


def _dilated_attention(qa, ka, va, ga, batch, seq):
    n = qa.shape[0]
    pairs = W_DIL // LANES
    spans = seq // SUPER
    qmap = lambda b, p, i: (b * spans + i, p)
    kvmap = lambda b, p, i: (b, p)
    copy = pltpu.VMEM((seq, LANES), BF16)
    return pl.pallas_call(
        _dilated_kernel,
        grid=(batch, pairs, spans),
        in_specs=[pl.BlockSpec((SUPER, LANES), qmap),
                  pl.BlockSpec((seq, LANES), kvmap),
                  pl.BlockSpec((seq, LANES), kvmap),
                  pl.BlockSpec((SUPER, LANES), qmap)],
        out_specs=pl.BlockSpec((SUPER, LANES), qmap),
        out_shape=jax.ShapeDtypeStruct((n, W_DIL), BF16),
        scratch_shapes=[copy] * 6 + [pltpu.VMEM((seq, LANES), F32)] + [pltpu.VMEM((2, SUPER, LANES), F32)] * 2,
        compiler_params=_cparams(("parallel", "parallel", "arbitrary")),
        name="dilated_attention",
    )(qa, ka, va, ga)


def _suffix_count_matrix():
    j = np.arange(QB)[:, None]
    s = np.arange(QB)[None, :]
    u = (j > s).astype(np.float32)
    z = np.zeros_like(u)
    return np.block([[u, z], [z, u]])


def _log_keep_and_beta(z):
    log_keep = jnp.log(1.0 / (1.0 + jnp.exp(-jnp.abs(z)))) - jnp.maximum(z, 0.0)
    return log_keep, z + log_keep


SB_GROUP = 8


def _stick_kernel(q_ref, k_ref, v_ref, g_ref, u_ref, o_ref):
    step = pl.program_id(2)
    first_head = _head_masks()
    head_sel = (jnp.where(first_head, 1.0, 0.0).astype(BF16), jnp.where(first_head, 0.0, 1.0).astype(BF16))
    u = u_ref[...]
    row = lax.broadcasted_iota(jnp.int32, (QB, 2 * QB), 0)
    col = lax.broadcasted_iota(jnp.int32, (QB, 2 * QB), 1) % QB
    strictly_causal = col < row

    groups = range(SB_GROUP)

    def key_block(g, back):
        kb = step * SB_GROUP + g - back
        return kb, pl.multiple_of(jnp.maximum(kb, 0) * QB, QB)


    def scores(back):
        z = []
        for g in groups:
            kblk = k_ref[pl.ds(key_block(g, back)[1], QB), :]
            k_bd = jnp.concatenate([kblk * head_sel[0], kblk * head_sel[1]], axis=0)
            z.append(lax.dot_general(q_ref[g * QB:(g + 1) * QB, :], k_bd, (((1,), (1,)), ((), ())),
                                     preferred_element_type=F32))
        return z

    def logs(z, own_block):
        parts = []
        for g in groups:
            log_keep, log_beta = _log_keep_and_beta(z[g])
            if own_block:
                log_keep = jnp.where(strictly_causal, log_keep, 0.0)
            hi = log_keep.astype(BF16)
            parts.append((log_keep, log_beta, hi, (log_keep - hi.astype(F32)).astype(BF16)))
        return parts

    def fold(back, parts, passed, acc, own_block):
        between, new_acc, new_passed, tops = [], [], [], []
        for g in groups:
            passed_g = passed[g]
            if not own_block:
                passed_g = passed_g + jnp.where(key_block(g, back)[0] >= 0, 0.0, NEG_BIG)
            between.append(passed_g + (jnp.dot(parts[g][2], u, preferred_element_type=F32)
                                       + jnp.dot(parts[g][3], u, preferred_element_type=F32)))
        for g in groups:
            w = jnp.exp(parts[g][1] + between[g])
            if own_block:
                w = jnp.where(strictly_causal, w, 0.0)
            vblk = v_ref[pl.ds(key_block(g, back)[1], QB), :]
            v_bd = jnp.concatenate([vblk * head_sel[0], vblk * head_sel[1]], axis=0)
            new_acc.append(acc[g] + jnp.dot(w.astype(BF16), v_bd, preferred_element_type=F32))
        for g in groups:
            total = between[g] + parts[g][0]
            total = (total[:, 0:1], total[:, QB:QB + 1])
            new_passed.append(jnp.concatenate([jnp.broadcast_to(t, (QB, QB)) for t in total], axis=1))
            tops.append(jnp.maximum(total[0], total[1]))
        live = jnp.max(functools.reduce(jnp.maximum, tops)) > SB_DEAD_LOG
        return tuple(new_passed), tuple(new_acc), live

    z_own, z_prev = scores(0), scores(1)
    parts_own, parts_prev = logs(z_own, True), logs(z_prev, False)
    passed, acc, _ = fold(0, parts_own, [jnp.zeros((QB, 2 * QB), F32)] * SB_GROUP,
                          [jnp.zeros((QB, LANES), F32)] * SB_GROUP, True)
    passed, acc, live = fold(1, parts_prev, passed, acc, False)

    def cond(c):
        back, live = c[:2]
        return jnp.logical_and(back <= step * SB_GROUP + SB_GROUP - 1, live)

    def body(c):
        back, _, passed, acc = c
        passed, acc, live = fold(back, logs(scores(back), False), passed, acc, False)
        return back + 1, live, passed, acc

    acc = lax.while_loop(cond, body, (2, live, passed, acc))[-1]
    for g in groups:
        rows = slice(g * QB, (g + 1) * QB)
        o_ref[rows, :] = (acc[g] * g_ref[rows, :]).astype(BF16)


def _stick_breaking_attention(qc, kc, vc, gc, u, batch, seq):
    n = qc.shape[0]
    pairs = W_SB // LANES
    rows = SB_GROUP * QB
    steps = seq // rows
    qmap = lambda b, p, i: (b * steps + i, p)
    kvmap = lambda b, p, i: (b, p)
    return pl.pallas_call(
        _stick_kernel,
        grid=(batch, pairs, steps),
        in_specs=[pl.BlockSpec((rows, LANES), qmap),
                  pl.BlockSpec((seq, LANES), kvmap),
                  pl.BlockSpec((seq, LANES), kvmap),
                  pl.BlockSpec((rows, LANES), qmap),
                  pl.BlockSpec(u.shape, lambda b, p, i: (0, 0))],
        out_specs=pl.BlockSpec((rows, LANES), qmap),
        out_shape=jax.ShapeDtypeStruct((n, W_SB), BF16),
        compiler_params=_cparams(("parallel", "parallel", "arbitrary")),
        name="stick_breaking_attention",
    )(qc, kc, vc, gc, u)


SUBLANES = 8
CONV_WIDTH = 4


def _rglru_kernel(x_ref, g_ref, cw_ref, cb_ref, wa_ref, ba_ref, wx_ref, bxg_ref, lam_ref, o_ref,
                  xbuf, a_sc, u_sc, h_sc, hcar):
    tc = x_ref.shape[0]

    @pl.when(pl.program_id(1) == 0)
    def _():
        xbuf[0:SUBLANES, :] = jnp.zeros((SUBLANES, W_LRU), F32)
        hcar[...] = jnp.zeros_like(hcar)

    x = x_ref[...]
    xbuf[SUBLANES:SUBLANES + tc, :] = x
    xc = x * cw_ref[CONV_WIDTH - 1:CONV_WIDTH, :] + cb_ref[...]
    for back in range(1, CONV_WIDTH):
        xc = xc + xbuf[SUBLANES - back:SUBLANES - back + tc, :] * cw_ref[CONV_WIDTH - 1 - back:CONV_WIDTH - back, :]
    xbuf[0:SUBLANES, :] = xbuf[tc:tc + SUBLANES, :]

    xcb = xc.astype(BF16)
    r = jax.nn.sigmoid(jnp.dot(xcb, wa_ref[...], preferred_element_type=F32) + ba_ref[...])
    i = jax.nn.sigmoid(jnp.dot(xcb, wx_ref[...], preferred_element_type=F32) + bxg_ref[...])
    lam = lam_ref[...]
    log_sig_lam = -(jnp.maximum(-lam, 0.0) + jnp.log1p(jnp.exp(-jnp.abs(lam))))
    log_a = LRU_C * r * log_sig_lam
    a = jnp.exp(log_a)
    a_sc[...] = a
    one_minus_a2 = jnp.tanh(-log_a) * (a * a + 1.0)
    root = jnp.where(one_minus_a2 > 0.0, one_minus_a2 * lax.rsqrt(one_minus_a2), 0.0)
    u_sc[...] = root * (i * xc)

    rows = lax.broadcasted_iota(jnp.int32, (SUBLANES, W_LRU), 0)

    def step(j, h_prev):
        start = pl.multiple_of(j * SUBLANES, SUBLANES)
        a = a_sc[pl.ds(start, SUBLANES), :]
        b = u_sc[pl.ds(start, SUBLANES), :]
        for shift in (1, 2, 4):
            keep = rows >= shift
            a_prev = jnp.where(keep, pltpu.roll(a, shift, 0), 1.0)
            b_prev = jnp.where(keep, pltpu.roll(b, shift, 0), 0.0)
            b = a * b_prev + b
            a = a * a_prev
        h = a * h_prev + b
        h_sc[pl.ds(start, SUBLANES), :] = h
        return jnp.broadcast_to(h[SUBLANES - 1:SUBLANES, :], (SUBLANES, W_LRU))

    hcar[...] = lax.fori_loop(0, tc // SUBLANES, step, hcar[...], unroll=8)
    o_ref[...] = (h_sc[...] * g_ref[...]).astype(BF16)


def _rg_lru(bx, bg, cw, cb, wa, ba, wx, bxg, lam, batch, seq, tc):
    n = bx.shape[0]
    nchunks = seq // tc
    row = lambda b, c: (b * nchunks + c, 0)
    const = lambda b, c: (0, 0)
    return pl.pallas_call(
        _rglru_kernel,
        grid=(batch, nchunks),
        in_specs=[pl.BlockSpec((tc, W_LRU), row),
                  pl.BlockSpec((tc, W_LRU), row),
                  pl.BlockSpec((CONV_WIDTH, W_LRU), const),
                  pl.BlockSpec((1, W_LRU), const),
                  pl.BlockSpec((W_LRU, W_LRU), const),
                  pl.BlockSpec((1, W_LRU), const),
                  pl.BlockSpec((W_LRU, W_LRU), const),
                  pl.BlockSpec((1, W_LRU), const),
                  pl.BlockSpec((1, W_LRU), const)],
        out_specs=pl.BlockSpec((tc, W_LRU), row),
        out_shape=jax.ShapeDtypeStruct((n, W_LRU), BF16),
        scratch_shapes=[pltpu.VMEM((tc + SUBLANES, W_LRU), F32),
                        pltpu.VMEM((tc, W_LRU), F32),
                        pltpu.VMEM((tc, W_LRU), F32),
                        pltpu.VMEM((tc, W_LRU), F32),
                        pltpu.VMEM((SUBLANES, W_LRU), F32)],
        compiler_params=_cparams(("arbitrary", "arbitrary")),
        name="rg_lru",
    )(bx, bg, cw, cb, wa, ba, wx, bxg, lam)


def _outproj_kernel(x_ref, ya_ref, yb_ref, yc_ref, w_ref, fg_ref, o_ref, *, final_norm):
    acc = x_ref[...]
    acc = acc + jnp.dot(ya_ref[...], w_ref[0:W_DIL, :], preferred_element_type=F32)
    acc = acc + jnp.dot(yb_ref[...], w_ref[W_DIL:W_DIL + W_LRU, :], preferred_element_type=F32)
    acc = acc + jnp.dot(yc_ref[...], w_ref[W_DIL + W_LRU:, :], preferred_element_type=F32)
    if final_norm:
        ms = jnp.mean(acc * acc, axis=-1, keepdims=True)
        acc = (acc * lax.rsqrt(ms + EPS)) * fg_ref[...]
    o_ref[...] = acc


def _out_projection(h, ya, yb, yc, w_bf16, final_gain, final_norm, tm):
    n = h.shape[0]
    row = lambda i: (i, 0)
    const = lambda i: (0, 0)
    return pl.pallas_call(
        functools.partial(_outproj_kernel, final_norm=final_norm),
        grid=(n // tm,),
        in_specs=[pl.BlockSpec((tm, D_MODEL), row),
                  pl.BlockSpec((tm, W_DIL), row),
                  pl.BlockSpec((tm, W_LRU), row),
                  pl.BlockSpec((tm, W_SB), row),
                  pl.BlockSpec((D_MODEL, D_MODEL), const),
                  pl.BlockSpec((1, D_MODEL), const)],
        out_specs=pl.BlockSpec((tm, D_MODEL), row),
        out_shape=jax.ShapeDtypeStruct((n, D_MODEL), F32),
        compiler_params=_cparams(("parallel",)),
        name="out_projection",
    )(h, ya, yb, yc, w_bf16, final_gain)


def _rope_tables(seq):
    pos = np.arange(seq, dtype=np.float64)
    inv_freq = ROPE_THETA ** (-np.arange(0, ROPE_DIM, 2, dtype=np.float64) / ROPE_DIM)
    ang = pos[:, None] * inv_freq[None, :]
    cos, sin = np.cos(ang), np.sin(ang)
    half = ROPE_DIM // 2
    lane = np.arange(LANES) % HEAD_DIM
    freq = lane % half
    cos_l, sin_l = cos[:, freq], sin[:, freq]
    cos_t = np.where(lane < ROPE_DIM, cos_l, 1.0)
    s1_t = np.where(lane < half, -sin_l, 0.0)
    s2_t = np.where((lane >= half) & (lane < ROPE_DIM), sin_l, 0.0)
    return tuple(jnp.asarray(t, dtype=F32) for t in (cos_t, s1_t, s2_t))


def _block_diagonal(w):
    nb, d, _ = w.shape
    out = jnp.zeros((nb * d, nb * d), w.dtype)
    for i in range(nb):
        out = lax.dynamic_update_slice(out, w[i], (i * d, i * d))
    return out


def kernel(x, norm_gain, w_in, conv_w, conv_b, gate_a_w, gate_a_b, gate_x_w, gate_x_b, lru_lambda, w_out, final_gain):
    batch, seq, d = x.shape
    depth = w_in.shape[0]
    n = batch * seq
    cos_t, s1_t, s2_t = _rope_tables(seq)
    u = jnp.asarray(_suffix_count_matrix(), dtype=BF16)
    h = x.reshape(n, d)
    for l in range(depth):
        qa, ka, va, ga, bx, bg, qc, kc, vc, gc = _in_projection(
            h, norm_gain[l].reshape(1, d), w_in[l].astype(BF16), cos_t, s1_t, s2_t, seq, tm=256)
        ya = _dilated_attention(qa, ka, va, ga, batch, seq)
        yb = _rg_lru(bx, bg, conv_w[l], conv_b[l].reshape(1, W_LRU),
                     _block_diagonal(gate_a_w[l]).astype(BF16), gate_a_b[l].reshape(1, W_LRU),
                     _block_diagonal(gate_x_w[l]).astype(BF16), gate_x_b[l].reshape(1, W_LRU),
                     lru_lambda[l].reshape(1, W_LRU), batch, seq, tc=512)
        yc = _stick_breaking_attention(qc, kc, vc, gc, u, batch, seq)
        h = _out_projection(h, ya, yb, yc, w_out[l].astype(BF16), final_gain.reshape(1, d),
                            final_norm=(l == depth - 1), tm=512)
    return h.reshape(batch, seq, d)
```

```python
import functools
import math

import jax
import jax.numpy as jnp
import numpy as np
from jax import lax
from jax.experimental import pallas as pl
from jax.experimental.pallas import tpu as pltpu

F32 = jnp.float32
BF16 = jnp.bfloat16

D_MODEL = 1024
HEAD_DIM = 64
W_DIL = 512
W_LRU = 256
W_SB = 256
D_IN = 4 * W_DIL + 2 * W_LRU + 4 * W_SB
ROPE_DIM = 16
ROPE_THETA = 500000.0
DILATED_PATTERNS = ((128, 1), (512, 4), (2048, 16))
LRU_C = 8.0
EPS = 1e-6

LANES = 128
QB = 128
NEG_BIG = -1e30
SB_DEAD_LOG = -104.0

VMEM_LIMIT = 56 * 1024 * 1024


def _cparams(sem):
    return pltpu.CompilerParams(dimension_semantics=sem, vmem_limit_bytes=VMEM_LIMIT)


def _inproj_kernel(x_ref, *refs):
    _project_in(x_ref[...], *refs)


def _project_in(x, gain_ref, w_ref, cos_ref, s1_ref, s2_ref,
                qa_ref, ka_ref, va_ref, ga_ref, bx_ref, bg_ref,
                qc_ref, kc_ref, vc_ref, gc_ref):
    ms = jnp.mean(x * x, axis=-1, keepdims=True)
    xn = ((x * lax.rsqrt(ms + EPS)) * gain_ref[...]).astype(BF16)

    def proj(lo, width):
        return jnp.dot(xn, w_ref[:, lo:lo + width], preferred_element_type=F32)

    cos = cos_ref[...]
    s1 = s1_ref[...]
    s2 = s2_ref[...]

    def rope(t):
        parts = []
        for c in range(t.shape[1] // LANES):
            xc = t[:, c * LANES:(c + 1) * LANES]
            parts.append(xc * cos + pltpu.roll(xc, LANES - ROPE_DIM // 2, 1) * s1
                         + pltpu.roll(xc, ROPE_DIM // 2, 1) * s2)
        return jnp.concatenate(parts, axis=1)

    def silu(t):
        return t * jax.nn.sigmoid(t)

    scale = 1.0 / math.sqrt(HEAD_DIM)
    o = 0
    qa_ref[...] = rope(proj(o, W_DIL)) * (scale * math.log2(math.e)); o += W_DIL
    ka_ref[...] = rope(proj(o, W_DIL)); o += W_DIL
    va_ref[...] = proj(o, W_DIL); o += W_DIL
    ga_ref[...] = silu(proj(o, W_DIL)); o += W_DIL
    bx_ref[...] = proj(o, W_LRU); o += W_LRU
    bg_ref[...] = silu(proj(o, W_LRU)); o += W_LRU
    qc_ref[...] = (proj(o, W_SB) * scale).astype(BF16); o += W_SB
    kc_ref[...] = proj(o, W_SB).astype(BF16); o += W_SB
    vc_ref[...] = proj(o, W_SB).astype(BF16); o += W_SB
    gc_ref[...] = silu(proj(o, W_SB))


_ROW = lambda i: (i, 0)
_CONST = lambda i: (0, 0)


def _inproj_specs(n, seq, tm):
    blocks_per_seq = seq // tm
    tab = lambda i: (i % blocks_per_seq, 0)
    widths = [(W_DIL, F32), (W_DIL, F32), (W_DIL, F32), (W_DIL, F32), (W_LRU, F32),
              (W_LRU, F32), (W_SB, BF16), (W_SB, BF16), (W_SB, BF16), (W_SB, F32)]
    in_specs = [pl.BlockSpec((1, D_MODEL), _CONST),
                pl.BlockSpec((D_MODEL, D_IN), _CONST),
                pl.BlockSpec((tm, LANES), tab),
                pl.BlockSpec((tm, LANES), tab),
                pl.BlockSpec((tm, LANES), tab)]
    return (in_specs, [pl.BlockSpec((tm, w), _ROW) for w, _ in widths],
            [jax.ShapeDtypeStruct((n, w), dt) for w, dt in widths])


def _in_projection(h, gain, w_bf16, tables, seq, tm):
    n = h.shape[0]
    in_specs, out_specs, out_shapes = _inproj_specs(n, seq, tm)
    return pl.pallas_call(
        _inproj_kernel,
        grid=(n // tm,),
        in_specs=[pl.BlockSpec((tm, D_MODEL), _ROW)] + in_specs,
        out_specs=out_specs,
        out_shape=out_shapes,
        compiler_params=_cparams(("parallel",)),
        name="in_projection",
    )(h, gain, w_bf16, *tables)


def _head_masks():
    lane = lax.broadcasted_iota(jnp.int32, (QB, LANES), 1)
    return lane < HEAD_DIM


SUPER = DILATED_PATTERNS[-1][0]
GROUP = 2


def _dilated_kernel(q_ref, k_ref, v_ref, g_ref, o_ref, k1, v1, k4, v4, k16, v16, stage, m_sc, acc_sc):
    span = pl.program_id(2)
    seq = k_ref.shape[0]
    first_head = _head_masks()

    @pl.when(span == 0)
    def _():
        quarter = seq // 4

        def split(j, tiles_per_residue):
            bits = tiles_per_residue.bit_length() - 1
            assert tiles_per_residue == 1 << bits
            return lax.shift_right_logical(j, bits), lax.bitwise_and(j, tiles_per_residue - 1)

        for src_ref, c1, c4, c16 in ((k_ref, k1, k4, k16), (v_ref, v1, v4, v16)):
            def by_four(j, carry):
                dst = pl.ds(pl.multiple_of(j * QB, QB), QB)
                r, tile = split(j, quarter // QB)
                u0 = tile * QB
                regrouped = src_ref[pl.ds(4 * u0 + r, QB, stride=4), :]
                c1[dst, :] = src_ref[dst, :].astype(BF16)
                stage[dst, :] = regrouped
                c4[dst, :] = regrouped.astype(BF16)
                return carry
            lax.fori_loop(0, seq // QB, by_four, 0, unroll=4)

            def by_sixteen(j, carry):
                dst = pl.ds(pl.multiple_of(j * QB, QB), QB)
                r16, tile = split(j, seq // 16 // QB)
                high, low = split(r16, 4)
                src = pl.ds(low * quarter + high + 4 * QB * tile, QB, stride=4)
                c16[dst, :] = stage[src, :].astype(BF16)
                return carry
            lax.fori_loop(0, seq // QB, by_sixteen, 0, unroll=4)

    row = lax.broadcasted_iota(jnp.int32, (QB, 2 * QB), 0)
    col = lax.broadcasted_iota(jnp.int32, (QB, 2 * QB), 1) % QB
    bias_cur = jnp.where(col <= row, 0.0, NEG_BIG)
    bias_prev = jnp.where(col >= row, 0.0, NEG_BIG)
    head_sel = (jnp.where(first_head, 1.0, 0.0).astype(BF16), jnp.where(first_head, 0.0, 1.0).astype(BF16))

    groups = range(GROUP)

    def score_stage(kc, rows, cur_start, prev_valid):
        qt = [q_ref[rows[g], :].astype(BF16) for g in groups]
        starts, biases = [], []
        for g in groups:
            cur = pl.multiple_of(cur_start[g], QB)
            prev = pl.multiple_of(jnp.maximum(cur_start[g] - QB, 0), QB)
            starts.append((cur, prev))
            known = isinstance(prev_valid[g], bool)
            assert not known or prev_valid[g]
            biases.append((bias_cur, bias_prev if known else bias_prev + jnp.where(prev_valid[g], 0.0, NEG_BIG)))
        k_bd = [[jnp.concatenate([kblk * head_sel[0], kblk * head_sel[1]], axis=0)
                 for kblk in (kc[pl.ds(s, QB), :] for s in starts[g])] for g in groups]
        scores = [[lax.dot_general(qt[g], k_bd[g][t], (((1,), (1,)), ((), ())),
                                   preferred_element_type=F32) + biases[g][t] for t in range(2)]
                  for g in groups]
        return starts, scores

    def softmax_stage(vc, rows, starts, scores, fresh, final):
        if not fresh:
            old = [(m_sc[0, rows[g], :], m_sc[1, rows[g], :],
                    jnp.concatenate([acc_sc[0, rows[g], :], acc_sc[1, rows[g], :]], axis=1)) for g in groups]
        top = [jnp.maximum(scores[g][0], scores[g][1]) for g in groups]
        m_new = []
        for g in groups:
            pair = []
            for h in range(2):
                m_h = jnp.max(top[g][:, h * QB:(h + 1) * QB], axis=-1, keepdims=True)
                if not fresh:
                    m_h = jnp.maximum(old[g][h], m_h)
                pair.append(jnp.broadcast_to(m_h, (QB, QB)))
            m_new.append(pair)
        p = [[jnp.exp2(scores[g][t] - jnp.concatenate(m_new[g], axis=1)).astype(BF16) for t in range(2)]
             for g in groups]
        v_ext = [[jnp.concatenate([jnp.concatenate([vblk * head_sel[0], head_sel[0]], axis=1),
                                   jnp.concatenate([vblk * head_sel[1], head_sel[1]], axis=1)], axis=0)
                  for vblk in (vc[pl.ds(s, QB), :] for s in starts[g])] for g in groups]
        pv = [jnp.dot(p[g][0], v_ext[g][0], preferred_element_type=F32)
              + jnp.dot(p[g][1], v_ext[g][1], preferred_element_type=F32) for g in groups]
        if not fresh:
            alpha = [jnp.exp2(jnp.where(first_head, old[g][0] - m_new[g][0], old[g][1] - m_new[g][1]))
                     for g in groups]
            pv = [pv[g] + jnp.concatenate([alpha[g], alpha[g]], axis=1) * old[g][2] for g in groups]
        for g in groups:
            if final:
                o_ref[rows[g], :] = (pv[g][:, :LANES] / pv[g][:, LANES:] * g_ref[rows[g], :]).astype(BF16)
            else:
                m_sc[0, rows[g], :] = m_new[g][0]
                m_sc[1, rows[g], :] = m_new[g][1]
                acc_sc[0, rows[g], :] = pv[g][:, :LANES]
                acc_sc[1, rows[g], :] = pv[g][:, LANES:]

    blocks = SUPER // QB

    def pattern16(r):
        return pl.ds(r, QB, stride=16), r * (seq // 16) + span * QB, span > 0

    def pattern4(idx):
        r, ub = idx // 4, idx % 4
        return (pl.ds(ub * (4 * QB) + r, QB, stride=4), r * (seq // 4) + (span * 4 + ub) * QB,
                True if ub > 0 else span > 0)

    def pattern1(i):
        return pl.ds(i * QB, QB), (span * blocks + i) * QB, True if i > 0 else span > 0

    items = [(pattern, kc, vc, fresh, final, first)
             for pattern, kc, vc, fresh, final in ((pattern16, k16, v16, True, False),
                                                   (pattern4, k4, v4, False, False),
                                                   (pattern1, k1, v1, False, True))
             for first in range(0, blocks, GROUP)]

    def scores_of(item):
        pattern, kc, _, _, _, first = item
        rows, cur_start, prev_valid = zip(*[pattern(first + g) for g in groups])
        return (rows,) + score_stage(kc, rows, cur_start, prev_valid)

    ahead = scores_of(items[0])
    for k, item in enumerate(items):
        rows, starts, scores = ahead
        if k + 1 < len(items):
            ahead = scores_of(items[k + 1])
        softmax_stage(item[2], rows, starts, scores, item[3], item[4])


def _dilated_attention(qa, ka, va, ga, batch, seq):
    n = qa.shape[0]
    pairs = W_DIL // LANES
    spans = seq // SUPER
    qmap = lambda b, p, i: (b * spans + i, p)
    kvmap = lambda b, p, i: (b, p)
    copy = pltpu.VMEM((seq, LANES), BF16)
    return pl.pallas_call(
        _dilated_kernel,
        grid=(batch, pairs, spans),
        in_specs=[pl.BlockSpec((SUPER, LANES), qmap),
                  pl.BlockSpec((seq, LANES), kvmap),
                  pl.BlockSpec((seq, LANES), kvmap),
                  pl.BlockSpec((SUPER, LANES), qmap)],
        out_specs=pl.BlockSpec((SUPER, LANES), qmap),
        out_shape=jax.ShapeDtypeStruct((n, W_DIL), BF16),
        scratch_shapes=[copy] * 6 + [pltpu.VMEM((seq, LANES), F32)] + [pltpu.VMEM((2, SUPER, LANES), F32)] * 2,
        compiler_params=_cparams(("parallel", "parallel", "arbitrary")),
        name="dilated_attention",
    )(qa, ka, va, ga)


def _suffix_count_matrix():
    j = np.arange(QB)[:, None]
    s = np.arange(QB)[None, :]
    u = (j > s).astype(np.float32)
    z = np.zeros_like(u)
    return np.block([[u, z], [z, u]])


def _log_keep_and_beta(z):
    log_keep = jnp.log(1.0 / (1.0 + jnp.exp(-jnp.abs(z)))) - jnp.maximum(z, 0.0)
    return log_keep, z + log_keep


SB_GROUP = 8


def _stick_kernel(q_ref, k_ref, v_ref, g_ref, u_ref, o_ref):
    step = pl.program_id(2)
    first_head = _head_masks()
    head_sel = (jnp.where(first_head, 1.0, 0.0).astype(BF16), jnp.where(first_head, 0.0, 1.0).astype(BF16))
    u = u_ref[...]
    row = lax.broadcasted_iota(jnp.int32, (QB, 2 * QB), 0)
    col = lax.broadcasted_iota(jnp.int32, (QB, 2 * QB), 1) % QB
    strictly_causal = col < row

    groups = range(SB_GROUP)

    def key_block(g, back):
        kb = step * SB_GROUP + g - back
        return kb, pl.multiple_of(jnp.maximum(kb, 0) * QB, QB)


    def scores(back):
        z = []
        for g in groups:
            kblk = k_ref[pl.ds(key_block(g, back)[1], QB), :]
            k_bd = jnp.concatenate([kblk * head_sel[0], kblk * head_sel[1]], axis=0)
            z.append(lax.dot_general(q_ref[g * QB:(g + 1) * QB, :], k_bd, (((1,), (1,)), ((), ())),
                                     preferred_element_type=F32))
        return z

    def logs(z, own_block):
        parts = []
        for g in groups:
            log_keep, log_beta = _log_keep_and_beta(z[g])
            if own_block:
                log_keep = jnp.where(strictly_causal, log_keep, 0.0)
            hi = log_keep.astype(BF16)
            parts.append((log_keep, log_beta, hi, (log_keep - hi.astype(F32)).astype(BF16)))
        return parts

    def fold(back, parts, passed, acc, own_block):
        between, new_acc, new_passed, tops = [], [], [], []
        for g in groups:
            passed_g = passed[g]
            if not own_block:
                passed_g = passed_g + jnp.where(key_block(g, back)[0] >= 0, 0.0, NEG_BIG)
            between.append(passed_g + (jnp.dot(parts[g][2], u, preferred_element_type=F32)
                                       + jnp.dot(parts[g][3], u, preferred_element_type=F32)))
        for g in groups:
            w = jnp.exp(parts[g][1] + between[g])
            if own_block:
                w = jnp.where(strictly_causal, w, 0.0)
            vblk = v_ref[pl.ds(key_block(g, back)[1], QB), :]
            v_bd = jnp.concatenate([vblk * head_sel[0], vblk * head_sel[1]], axis=0)
            new_acc.append(acc[g] + jnp.dot(w.astype(BF16), v_bd, preferred_element_type=F32))
        for g in groups:
            total = between[g] + parts[g][0]
            total = (total[:, 0:1], total[:, QB:QB + 1])
            new_passed.append(jnp.concatenate([jnp.broadcast_to(t, (QB, QB)) for t in total], axis=1))
            tops.append(jnp.maximum(total[0], total[1]))
        live = jnp.max(functools.reduce(jnp.maximum, tops)) > SB_DEAD_LOG
        return tuple(new_passed), tuple(new_acc), live

    z_own, z_prev = scores(0), scores(1)
    parts_own, parts_prev = logs(z_own, True), logs(z_prev, False)
    passed, acc, _ = fold(0, parts_own, [jnp.zeros((QB, 2 * QB), F32)] * SB_GROUP,
                          [jnp.zeros((QB, LANES), F32)] * SB_GROUP, True)
    passed, acc, live = fold(1, parts_prev, passed, acc, False)

    def cond(c):
        back, live = c[:2]
        return jnp.logical_and(back <= step * SB_GROUP + SB_GROUP - 1, live)

    def body(c):
        back, _, passed, acc = c
        passed, acc, live = fold(back, logs(scores(back), False), passed, acc, False)
        return back + 1, live, passed, acc

    acc = lax.while_loop(cond, body, (2, live, passed, acc))[-1]
    for g in groups:
        rows = slice(g * QB, (g + 1) * QB)
        o_ref[rows, :] = (acc[g] * g_ref[rows, :]).astype(BF16)


def _stick_breaking_attention(qc, kc, vc, gc, u, batch, seq):
    n = qc.shape[0]
    pairs = W_SB // LANES
    rows = SB_GROUP * QB
    steps = seq // rows
    qmap = lambda b, p, i: (b * steps + i, p)
    kvmap = lambda b, p, i: (b, p)
    return pl.pallas_call(
        _stick_kernel,
        grid=(batch, pairs, steps),
        in_specs=[pl.BlockSpec((rows, LANES), qmap),
                  pl.BlockSpec((seq, LANES), kvmap),
                  pl.BlockSpec((seq, LANES), kvmap),
                  pl.BlockSpec((rows, LANES), qmap),
                  pl.BlockSpec(u.shape, lambda b, p, i: (0, 0))],
        out_specs=pl.BlockSpec((rows, LANES), qmap),
        out_shape=jax.ShapeDtypeStruct((n, W_SB), BF16),
        compiler_params=_cparams(("parallel", "parallel", "arbitrary")),
        name="stick_breaking_attention",
    )(qc, kc, vc, gc, u)


SUBLANES = 8
CONV_WIDTH = 4


def _rglru_kernel(x_ref, g_ref, cw_ref, cb_ref, wa_ref, ba_ref, wx_ref, bxg_ref, lam_ref, o_ref,
                  xbuf, a_sc, u_sc, h_sc, hcar):
    tc = x_ref.shape[0]

    @pl.when(pl.program_id(1) == 0)
    def _():
        xbuf[0:SUBLANES, :] = jnp.zeros((SUBLANES, W_LRU), F32)
        hcar[...] = jnp.zeros_like(hcar)

    x = x_ref[...]
    xbuf[SUBLANES:SUBLANES + tc, :] = x
    xc = x * cw_ref[CONV_WIDTH - 1:CONV_WIDTH, :] + cb_ref[...]
    for back in range(1, CONV_WIDTH):
        xc = xc + xbuf[SUBLANES - back:SUBLANES - back + tc, :] * cw_ref[CONV_WIDTH - 1 - back:CONV_WIDTH - back, :]
    xbuf[0:SUBLANES, :] = xbuf[tc:tc + SUBLANES, :]

    xcb = xc.astype(BF16)
    r = jax.nn.sigmoid(jnp.dot(xcb, wa_ref[...], preferred_element_type=F32) + ba_ref[...])
    i = jax.nn.sigmoid(jnp.dot(xcb, wx_ref[...], preferred_element_type=F32) + bxg_ref[...])
    lam = lam_ref[...]
    log_sig_lam = -(jnp.maximum(-lam, 0.0) + jnp.log1p(jnp.exp(-jnp.abs(lam))))
    log_a = LRU_C * r * log_sig_lam
    a = jnp.exp(log_a)
    a_sc[...] = a
    one_minus_a2 = jnp.tanh(-log_a) * (a * a + 1.0)
    root = jnp.where(one_minus_a2 > 0.0, one_minus_a2 * lax.rsqrt(one_minus_a2), 0.0)
    u_sc[...] = root * (i * xc)

    rows = lax.broadcasted_iota(jnp.int32, (SUBLANES, W_LRU), 0)

    def step(j, h_prev):
        start = pl.multiple_of(j * SUBLANES, SUBLANES)
        a = a_sc[pl.ds(start, SUBLANES), :]
        b = u_sc[pl.ds(start, SUBLANES), :]
        for shift in (1, 2, 4):
            keep = rows >= shift
            a_prev = jnp.where(keep, pltpu.roll(a, shift, 0), 1.0)
            b_prev = jnp.where(keep, pltpu.roll(b, shift, 0), 0.0)
            b = a * b_prev + b
            a = a * a_prev
        h = a * h_prev + b
        h_sc[pl.ds(start, SUBLANES), :] = h
        return jnp.broadcast_to(h[SUBLANES - 1:SUBLANES, :], (SUBLANES, W_LRU))

    hcar[...] = lax.fori_loop(0, tc // SUBLANES, step, hcar[...], unroll=8)
    o_ref[...] = (h_sc[...] * g_ref[...]).astype(BF16)


def _rg_lru(bx, bg, cw, cb, wa, ba, wx, bxg, lam, batch, seq, tc):
    n = bx.shape[0]
    nchunks = seq // tc
    row = lambda b, c: (b * nchunks + c, 0)
    const = lambda b, c: (0, 0)
    return pl.pallas_call(
        _rglru_kernel,
        grid=(batch, nchunks),
        in_specs=[pl.BlockSpec((tc, W_LRU), row),
                  pl.BlockSpec((tc, W_LRU), row),
                  pl.BlockSpec((CONV_WIDTH, W_LRU), const),
                  pl.BlockSpec((1, W_LRU), const),
                  pl.BlockSpec((W_LRU, W_LRU), const),
                  pl.BlockSpec((1, W_LRU), const),
                  pl.BlockSpec((W_LRU, W_LRU), const),
                  pl.BlockSpec((1, W_LRU), const),
                  pl.BlockSpec((1, W_LRU), const)],
        out_specs=pl.BlockSpec((tc, W_LRU), row),
        out_shape=jax.ShapeDtypeStruct((n, W_LRU), BF16),
        scratch_shapes=[pltpu.VMEM((tc + SUBLANES, W_LRU), F32),
                        pltpu.VMEM((tc, W_LRU), F32),
                        pltpu.VMEM((tc, W_LRU), F32),
                        pltpu.VMEM((tc, W_LRU), F32),
                        pltpu.VMEM((SUBLANES, W_LRU), F32)],
        compiler_params=_cparams(("arbitrary", "arbitrary")),
        name="rg_lru",
    )(bx, bg, cw, cb, wa, ba, wx, bxg, lam)


def _residual_out(x_ref, ya_ref, yb_ref, yc_ref, w_ref):
    acc = x_ref[...]
    acc = acc + jnp.dot(ya_ref[...], w_ref[0:W_DIL, :], preferred_element_type=F32)
    acc = acc + jnp.dot(yb_ref[...], w_ref[W_DIL:W_DIL + W_LRU, :], preferred_element_type=F32)
    return acc + jnp.dot(yc_ref[...], w_ref[W_DIL + W_LRU:, :], preferred_element_type=F32)


def _final_kernel(x_ref, ya_ref, yb_ref, yc_ref, w_ref, fg_ref, o_ref):
    acc = _residual_out(x_ref, ya_ref, yb_ref, yc_ref, w_ref)
    ms = jnp.mean(acc * acc, axis=-1, keepdims=True)
    o_ref[...] = (acc * lax.rsqrt(ms + EPS)) * fg_ref[...]


def _boundary_kernel(x_ref, ya_ref, yb_ref, yc_ref, w_ref, *refs):
    h_ref = refs[5]
    h = _residual_out(x_ref, ya_ref, yb_ref, yc_ref, w_ref)
    h_ref[...] = h
    _project_in(h, *refs[:5], *refs[6:])


def _outproj_specs(tm):
    return [pl.BlockSpec((tm, D_MODEL), _ROW),
            pl.BlockSpec((tm, W_DIL), _ROW),
            pl.BlockSpec((tm, W_LRU), _ROW),
            pl.BlockSpec((tm, W_SB), _ROW),
            pl.BlockSpec((D_MODEL, D_MODEL), _CONST)]


def _final_projection(h, ya, yb, yc, w_bf16, final_gain, tm):
    n = h.shape[0]
    return pl.pallas_call(
        _final_kernel,
        grid=(n // tm,),
        in_specs=_outproj_specs(tm) + [pl.BlockSpec((1, D_MODEL), _CONST)],
        out_specs=pl.BlockSpec((tm, D_MODEL), _ROW),
        out_shape=jax.ShapeDtypeStruct((n, D_MODEL), F32),
        compiler_params=_cparams(("parallel",)),
        name="out_projection",
    )(h, ya, yb, yc, w_bf16, final_gain)


def _boundary_projection(h, ya, yb, yc, w_out_bf16, gain, w_in_bf16, tables, seq, tm):
    n = h.shape[0]
    in_specs, out_specs, out_shapes = _inproj_specs(n, seq, tm)
    return pl.pallas_call(
        _boundary_kernel,
        grid=(n // tm,),
        in_specs=_outproj_specs(tm) + in_specs,
        out_specs=[pl.BlockSpec((tm, D_MODEL), _ROW)] + out_specs,
        out_shape=[jax.ShapeDtypeStruct((n, D_MODEL), F32)] + out_shapes,
        compiler_params=_cparams(("parallel",)),
        name="out_in_projection",
    )(h, ya, yb, yc, w_out_bf16, gain, w_in_bf16, *tables)


def _rope_tables(seq):
    pos = np.arange(seq, dtype=np.float64)
    inv_freq = ROPE_THETA ** (-np.arange(0, ROPE_DIM, 2, dtype=np.float64) / ROPE_DIM)
    ang = pos[:, None] * inv_freq[None, :]
    cos, sin = np.cos(ang), np.sin(ang)
    half = ROPE_DIM // 2
    lane = np.arange(LANES) % HEAD_DIM
    freq = lane % half
    cos_l, sin_l = cos[:, freq], sin[:, freq]
    cos_t = np.where(lane < ROPE_DIM, cos_l, 1.0)
    s1_t = np.where(lane < half, -sin_l, 0.0)
    s2_t = np.where((lane >= half) & (lane < ROPE_DIM), sin_l, 0.0)
    return tuple(jnp.asarray(t, dtype=F32) for t in (cos_t, s1_t, s2_t))


def _block_diagonal(w):
    nb, d, _ = w.shape
    out = jnp.zeros((nb * d, nb * d), w.dtype)
    for i in range(nb):
        out = lax.dynamic_update_slice(out, w[i], (i * d, i * d))
    return out


def kernel(x, norm_gain, w_in, conv_w, conv_b, gate_a_w, gate_a_b, gate_x_w, gate_x_b, lru_lambda, w_out, final_gain):
    batch, seq, d = x.shape
    depth = w_in.shape[0]
    n = batch * seq
    tables = _rope_tables(seq)
    u = jnp.asarray(_suffix_count_matrix(), dtype=BF16)
    w_in_bf16, w_out_bf16 = w_in.astype(BF16), w_out.astype(BF16)
    tm = 512
    h = x.reshape(n, d)
    projected = _in_projection(h, norm_gain[0].reshape(1, d), w_in_bf16[0], tables, seq, tm)
    for l in range(depth):
        qa, ka, va, ga, bx, bg, qc, kc, vc, gc = projected
        ya = _dilated_attention(qa, ka, va, ga, batch, seq)
        yb = _rg_lru(bx, bg, conv_w[l], conv_b[l].reshape(1, W_LRU),
                     _block_diagonal(gate_a_w[l]).astype(BF16), gate_a_b[l].reshape(1, W_LRU),
                     _block_diagonal(gate_x_w[l]).astype(BF16), gate_x_b[l].reshape(1, W_LRU),
                     lru_lambda[l].reshape(1, W_LRU), batch, seq, tc=512)
        yc = _stick_breaking_attention(qc, kc, vc, gc, u, batch, seq)
        if l + 1 < depth:
            h, *projected = _boundary_projection(h, ya, yb, yc, w_out_bf16[l], norm_gain[l + 1].reshape(1, d),
                                                 w_in_bf16[l + 1], tables, seq, tm)
        else:
            h = _final_projection(h, ya, yb, yc, w_out_bf16[l], final_gain.reshape(1, d), tm)
    return h.reshape(batch, seq, d)
```

```python
import functools
import math

import jax
import jax.numpy as jnp
import numpy as np
from jax import lax
from jax.experimental import pallas as pl
from jax.experimental.pallas import tpu as pltpu

F32 = jnp.float32
BF16 = jnp.bfloat16

D_MODEL = 1024
HEAD_DIM = 64
W_DIL = 512
W_LRU = 256
W_SB = 256
D_IN = 4 * W_DIL + 2 * W_LRU + 4 * W_SB
ROPE_DIM = 16
ROPE_THETA = 500000.0
DILATED_PATTERNS = ((128, 1), (512, 4), (2048, 16))
LRU_C = 8.0
EPS = 1e-6

LANES = 128
QB = 128
NEG_BIG = -1e30
SB_DEAD_LOG2 = -150.5

VMEM_LIMIT = 56 * 1024 * 1024


def _cparams(sem):
    return pltpu.CompilerParams(dimension_semantics=sem, vmem_limit_bytes=VMEM_LIMIT)


def _inproj_kernel(x_ref, *refs):
    _project_in(x_ref[...], *refs)


def _project_in(x, gain_ref, w_ref, cos_ref, s1_ref, s2_ref,
                qa_ref, ka_ref, va_ref, ga_ref, bx_ref, bg_ref,
                qc_ref, kc_ref, vc_ref, gc_ref):
    ms = jnp.mean(x * x, axis=-1, keepdims=True)
    xn = ((x * lax.rsqrt(ms + EPS)) * gain_ref[...]).astype(BF16)

    def proj(lo, width):
        return jnp.dot(xn, w_ref[:, lo:lo + width], preferred_element_type=F32)

    cos = cos_ref[...]
    s1 = s1_ref[...]
    s2 = s2_ref[...]

    def rope(t):
        parts = []
        for c in range(t.shape[1] // LANES):
            xc = t[:, c * LANES:(c + 1) * LANES]
            parts.append(xc * cos + pltpu.roll(xc, LANES - ROPE_DIM // 2, 1) * s1
                         + pltpu.roll(xc, ROPE_DIM // 2, 1) * s2)
        return jnp.concatenate(parts, axis=1)

    def silu(t):
        return t * jax.nn.sigmoid(t)

    scale = 1.0 / math.sqrt(HEAD_DIM)
    o = 0
    qa_ref[...] = rope(proj(o, W_DIL)) * (scale * math.log2(math.e)); o += W_DIL
    ka_ref[...] = rope(proj(o, W_DIL)); o += W_DIL
    va_ref[...] = proj(o, W_DIL); o += W_DIL
    ga_ref[...] = silu(proj(o, W_DIL)); o += W_DIL
    bx_ref[...] = proj(o, W_LRU); o += W_LRU
    bg_ref[...] = silu(proj(o, W_LRU)); o += W_LRU
    qc_ref[...] = (proj(o, W_SB) * (scale * math.log2(math.e))).astype(BF16); o += W_SB
    kc_ref[...] = proj(o, W_SB).astype(BF16); o += W_SB
    vc_ref[...] = proj(o, W_SB).astype(BF16); o += W_SB
    gc_ref[...] = silu(proj(o, W_SB))


_ROW = lambda i: (i, 0)
_CONST = lambda i: (0, 0)


def _inproj_specs(n, seq, tm):
    blocks_per_seq = seq // tm
    tab = lambda i: (i % blocks_per_seq, 0)
    widths = [(W_DIL, F32), (W_DIL, F32), (W_DIL, F32), (W_DIL, F32), (W_LRU, F32),
              (W_LRU, F32), (W_SB, BF16), (W_SB, BF16), (W_SB, BF16), (W_SB, F32)]
    in_specs = [pl.BlockSpec((1, D_MODEL), _CONST),
                pl.BlockSpec((D_MODEL, D_IN), _CONST),
                pl.BlockSpec((tm, LANES), tab),
                pl.BlockSpec((tm, LANES), tab),
                pl.BlockSpec((tm, LANES), tab)]
    return (in_specs, [pl.BlockSpec((tm, w), _ROW) for w, _ in widths],
            [jax.ShapeDtypeStruct((n, w), dt) for w, dt in widths])


def _in_projection(h, gain, w_bf16, tables, seq, tm):
    n = h.shape[0]
    in_specs, out_specs, out_shapes = _inproj_specs(n, seq, tm)
    return pl.pallas_call(
        _inproj_kernel,
        grid=(n // tm,),
        in_specs=[pl.BlockSpec((tm, D_MODEL), _ROW)] + in_specs,
        out_specs=out_specs,
        out_shape=out_shapes,
        compiler_params=_cparams(("parallel",)),
        name="in_projection",
    )(h, gain, w_bf16, *tables)


def _head_masks():
    lane = lax.broadcasted_iota(jnp.int32, (QB, LANES), 1)
    return lane < HEAD_DIM


SUPER = DILATED_PATTERNS[-1][0]
GROUP = 2


def _dilated_kernel(q_ref, k_ref, v_ref, g_ref, o_ref, k1, v1, k4, v4, k16, v16, stage, m_sc, acc_sc):
    span = pl.program_id(2)
    seq = k_ref.shape[0]
    first_head = _head_masks()

    @pl.when(span == 0)
    def _():
        quarter = seq // 4

        def split(j, tiles_per_residue):
            bits = tiles_per_residue.bit_length() - 1
            assert tiles_per_residue == 1 << bits
            return lax.shift_right_logical(j, bits), lax.bitwise_and(j, tiles_per_residue - 1)

        for src_ref, c1, c4, c16 in ((k_ref, k1, k4, k16), (v_ref, v1, v4, v16)):
            def by_four(j, carry):
                dst = pl.ds(pl.multiple_of(j * QB, QB), QB)
                r, tile = split(j, quarter // QB)
                u0 = tile * QB
                regrouped = src_ref[pl.ds(4 * u0 + r, QB, stride=4), :]
                c1[dst, :] = src_ref[dst, :].astype(BF16)
                stage[dst, :] = regrouped
                c4[dst, :] = regrouped.astype(BF16)
                return carry
            lax.fori_loop(0, seq // QB, by_four, 0, unroll=4)

            def by_sixteen(j, carry):
                dst = pl.ds(pl.multiple_of(j * QB, QB), QB)
                r16, tile = split(j, seq // 16 // QB)
                high, low = split(r16, 4)
                src = pl.ds(low * quarter + high + 4 * QB * tile, QB, stride=4)
                c16[dst, :] = stage[src, :].astype(BF16)
                return carry
            lax.fori_loop(0, seq // QB, by_sixteen, 0, unroll=4)

    row = lax.broadcasted_iota(jnp.int32, (QB, 2 * QB), 0)
    col = lax.broadcasted_iota(jnp.int32, (QB, 2 * QB), 1) % QB
    bias_cur = jnp.where(col <= row, 0.0, NEG_BIG)
    bias_prev = jnp.where(col >= row, 0.0, NEG_BIG)
    head_sel = (jnp.where(first_head, 1.0, 0.0).astype(BF16), jnp.where(first_head, 0.0, 1.0).astype(BF16))

    groups = range(GROUP)

    def score_stage(kc, rows, cur_start, prev_valid):
        qt = [q_ref[rows[g], :].astype(BF16) for g in groups]
        starts, biases = [], []
        for g in groups:
            cur = pl.multiple_of(cur_start[g], QB)
            prev = pl.multiple_of(jnp.maximum(cur_start[g] - QB, 0), QB)
            starts.append((cur, prev))
            known = isinstance(prev_valid[g], bool)
            assert not known or prev_valid[g]
            biases.append((bias_cur, bias_prev if known else bias_prev + jnp.where(prev_valid[g], 0.0, NEG_BIG)))
        k_bd = [[jnp.concatenate([kblk * head_sel[0], kblk * head_sel[1]], axis=0)
                 for kblk in (kc[pl.ds(s, QB), :] for s in starts[g])] for g in groups]
        scores = [[lax.dot_general(qt[g], k_bd[g][t], (((1,), (1,)), ((), ())),
                                   preferred_element_type=F32) + biases[g][t] for t in range(2)]
                  for g in groups]
        return starts, scores

    def softmax_stage(vc, rows, starts, scores, fresh, final):
        if not fresh:
            old = [(m_sc[0, rows[g], :], m_sc[1, rows[g], :],
                    jnp.concatenate([acc_sc[0, rows[g], :], acc_sc[1, rows[g], :]], axis=1)) for g in groups]
        top = [jnp.maximum(scores[g][0], scores[g][1]) for g in groups]
        m_new = []
        for g in groups:
            pair = []
            for h in range(2):
                m_h = jnp.max(top[g][:, h * QB:(h + 1) * QB], axis=-1, keepdims=True)
                if not fresh:
                    m_h = jnp.maximum(old[g][h], m_h)
                pair.append(jnp.broadcast_to(m_h, (QB, QB)))
            m_new.append(pair)
        p = [[jnp.exp2(scores[g][t] - jnp.concatenate(m_new[g], axis=1)).astype(BF16) for t in range(2)]
             for g in groups]
        v_ext = [[jnp.concatenate([jnp.concatenate([vblk * head_sel[0], head_sel[0]], axis=1),
                                   jnp.concatenate([vblk * head_sel[1], head_sel[1]], axis=1)], axis=0)
                  for vblk in (vc[pl.ds(s, QB), :] for s in starts[g])] for g in groups]
        pv = [jnp.dot(p[g][0], v_ext[g][0], preferred_element_type=F32)
              + jnp.dot(p[g][1], v_ext[g][1], preferred_element_type=F32) for g in groups]
        if not fresh:
            alpha = [jnp.exp2(jnp.where(first_head, old[g][0] - m_new[g][0], old[g][1] - m_new[g][1]))
                     for g in groups]
            pv = [pv[g] + jnp.concatenate([alpha[g], alpha[g]], axis=1) * old[g][2] for g in groups]
        for g in groups:
            if final:
                o_ref[rows[g], :] = (pv[g][:, :LANES] / pv[g][:, LANES:] * g_ref[rows[g], :]).astype(BF16)
            else:
                m_sc[0, rows[g], :] = m_new[g][0]
                m_sc[1, rows[g], :] = m_new[g][1]
                acc_sc[0, rows[g], :] = pv[g][:, :LANES]
                acc_sc[1, rows[g], :] = pv[g][:, LANES:]

    blocks = SUPER // QB

    def pattern16(r):
        return pl.ds(r, QB, stride=16), r * (seq // 16) + span * QB, span > 0

    def pattern4(idx):
        r, ub = idx // 4, idx % 4
        return (pl.ds(ub * (4 * QB) + r, QB, stride=4), r * (seq // 4) + (span * 4 + ub) * QB,
                True if ub > 0 else span > 0)

    def pattern1(i):
        return pl.ds(i * QB, QB), (span * blocks + i) * QB, True if i > 0 else span > 0

    items = [(pattern, kc, vc, fresh, final, first)
             for pattern, kc, vc, fresh, final in ((pattern16, k16, v16, True, False),
                                                   (pattern4, k4, v4, False, False),
                                                   (pattern1, k1, v1, False, True))
             for first in range(0, blocks, GROUP)]

    def scores_of(item):
        pattern, kc, _, _, _, first = item
        rows, cur_start, prev_valid = zip(*[pattern(first + g) for g in groups])
        return (rows,) + score_stage(kc, rows, cur_start, prev_valid)

    ahead = scores_of(items[0])
    for k, item in enumerate(items):
        rows, starts, scores = ahead
        if k + 1 < len(items):
            ahead = scores_of(items[k + 1])
        softmax_stage(item[2], rows, starts, scores, item[3], item[4])


def _dilated_attention(qa, ka, va, ga, batch, seq):
    n = qa.shape[0]
    pairs = W_DIL // LANES
    spans = seq // SUPER
    qmap = lambda b, p, i: (b * spans + i, p)
    kvmap = lambda b, p, i: (b, p)
    copy = pltpu.VMEM((seq, LANES), BF16)
    return pl.pallas_call(
        _dilated_kernel,
        grid=(batch, pairs, spans),
        in_specs=[pl.BlockSpec((SUPER, LANES), qmap),
                  pl.BlockSpec((seq, LANES), kvmap),
                  pl.BlockSpec((seq, LANES), kvmap),
                  pl.BlockSpec((SUPER, LANES), qmap)],
        out_specs=pl.BlockSpec((SUPER, LANES), qmap),
        out_shape=jax.ShapeDtypeStruct((n, W_DIL), BF16),
        scratch_shapes=[copy] * 6 + [pltpu.VMEM((seq, LANES), F32)] + [pltpu.VMEM((2, SUPER, LANES), F32)] * 2,
        compiler_params=_cparams(("parallel", "parallel", "arbitrary")),
        name="dilated_attention",
    )(qa, ka, va, ga)


def _suffix_count_matrix(width):
    j = np.arange(width)[:, None]
    s = np.arange(width)[None, :]
    u = (j > s).astype(np.float32)
    z = np.zeros_like(u)
    return np.block([[u, z], [z, u]])


def _log_keep_and_beta(z2):
    log_keep = jnp.log(1.0 / (1.0 + jnp.exp2(-jnp.abs(z2)))) * math.log2(math.e) - jnp.maximum(z2, 0.0)
    return log_keep, z2 + log_keep


SB_GROUP = 8
SB_HALF = QB // 2


def _stick_kernel(q_ref, k_ref, v_ref, g_ref, u_ref, u_half_ref, o_ref):
    step = pl.program_id(2)
    first_head = _head_masks()
    head_sel = (jnp.where(first_head, 1.0, 0.0).astype(BF16), jnp.where(first_head, 0.0, 1.0).astype(BF16))
    suffix_count = {QB: u_ref[...], SB_HALF: u_half_ref[...]}
    row = lax.broadcasted_iota(jnp.int32, (QB, 2 * QB), 0)
    col = lax.broadcasted_iota(jnp.int32, (QB, 2 * QB), 1) % QB
    strictly_causal = col < row

    groups = range(SB_GROUP)

    def key_rows(g, offset, width):
        first = (step * SB_GROUP + g) * QB - offset
        exists = True if isinstance(offset, int) and g * QB >= offset else first >= 0
        return pl.multiple_of(jnp.maximum(first, 0), width), exists

    lane_half = lax.broadcasted_iota(jnp.int32, (SB_HALF, LANES), 1) < HEAD_DIM
    head_sel_half = (jnp.where(lane_half, 1.0, 0.0).astype(BF16), jnp.where(lane_half, 0.0, 1.0).astype(BF16))

    def block_diagonal(ref, g, offset, width):
        sel = head_sel if width == QB else head_sel_half
        blk = ref[pl.ds(key_rows(g, offset, width)[0], width), :]
        return jnp.concatenate([blk * sel[0], blk * sel[1]], axis=0)


    def scores(offset, width):
        return [lax.dot_general(q_ref[g * QB:(g + 1) * QB, :], block_diagonal(k_ref, g, offset, width),
                                (((1,), (1,)), ((), ())), preferred_element_type=F32)
                for g in groups]

    def logs(z, own_block):
        parts = []
        for g in groups:
            log_keep, log_beta = _log_keep_and_beta(z[g])
            if own_block:
                log_keep = jnp.where(strictly_causal, log_keep, 0.0)
            hi = log_keep.astype(BF16)
            parts.append((log_keep, log_beta, hi, (log_keep - hi.astype(F32)).astype(BF16)))
        return parts

    def fold(offset, width, parts, passed, acc, own_block):
        u = suffix_count[width]
        between, new_acc, new_passed, tops = [], [], [], []
        for g in groups:
            passed_g, exists = passed[g], key_rows(g, offset, width)[1]
            if exists is not True:
                passed_g = passed_g + jnp.where(exists, 0.0, NEG_BIG)
            between.append(passed_g + (jnp.dot(parts[g][2], u, preferred_element_type=F32)
                                       + jnp.dot(parts[g][3], u, preferred_element_type=F32)))
        for g in groups:
            w = jnp.exp2(parts[g][1] + between[g])
            if own_block:
                w = jnp.where(strictly_causal, w, 0.0)
            new_acc.append(acc[g] + jnp.dot(w.astype(BF16), block_diagonal(v_ref, g, offset, width),
                                            preferred_element_type=F32))
        for g in groups:
            total = between[g] + parts[g][0]
            total = (total[:, 0:1], total[:, width:width + 1])
            tops.append(jnp.maximum(total[0], total[1]))
            total = [jnp.broadcast_to(t, (QB, LANES)) for t in total]
            new_passed.append(jnp.concatenate(total, axis=1) if width == LANES
                              else jnp.where(first_head, total[0], total[1]))
        live = jnp.max(functools.reduce(jnp.maximum, tops)) > SB_DEAD_LOG2
        return tuple(new_passed), tuple(new_acc), live

    z_own, z_prev = scores(0, QB), scores(QB, QB)
    parts_own, parts_prev = logs(z_own, True), logs(z_prev, False)
    passed, acc, _ = fold(0, QB, parts_own, [jnp.zeros((QB, 2 * QB), F32)] * SB_GROUP,
                          [jnp.zeros((QB, LANES), F32)] * SB_GROUP, True)
    passed, acc, live = fold(QB, QB, parts_prev, passed, acc, False)

    last_block_start = (step * SB_GROUP + SB_GROUP - 1) * QB
    passed = tuple(jnp.where(first_head, p[:, :QB], p[:, QB:]) for p in passed)

    def cond(c):
        halves, live = c[:2]
        return jnp.logical_and(QB + SB_HALF * (halves + 1) <= last_block_start, live)

    def body(c):
        halves, _, passed, acc = c
        offset = QB + SB_HALF * (halves + 1)
        passed, acc, live = fold(offset, SB_HALF, logs(scores(offset, SB_HALF), False), passed, acc, False)
        return halves + 1, live, passed, acc

    acc = lax.while_loop(cond, body, (0, live, passed, acc))[-1]
    for g in groups:
        rows = slice(g * QB, (g + 1) * QB)
        o_ref[rows, :] = (acc[g] * g_ref[rows, :]).astype(BF16)


def _stick_breaking_attention(qc, kc, vc, gc, batch, seq):
    n = qc.shape[0]
    u, u_half = (jnp.asarray(_suffix_count_matrix(w), dtype=BF16) for w in (QB, SB_HALF))
    whole = lambda b, p, i: (0, 0)
    pairs = W_SB // LANES
    rows = SB_GROUP * QB
    steps = seq // rows
    qmap = lambda b, p, i: (b * steps + i, p)
    kvmap = lambda b, p, i: (b, p)
    return pl.pallas_call(
        _stick_kernel,
        grid=(batch, pairs, steps),
        in_specs=[pl.BlockSpec((rows, LANES), qmap),
                  pl.BlockSpec((seq, LANES), kvmap),
                  pl.BlockSpec((seq, LANES), kvmap),
                  pl.BlockSpec((rows, LANES), qmap),
                  pl.BlockSpec(u.shape, whole),
                  pl.BlockSpec(u_half.shape, whole)],
        out_specs=pl.BlockSpec((rows, LANES), qmap),
        out_shape=jax.ShapeDtypeStruct((n, W_SB), BF16),
        compiler_params=_cparams(("parallel", "parallel", "arbitrary")),
        name="stick_breaking_attention",
    )(qc, kc, vc, gc, u, u_half)


SUBLANES = 8
CONV_WIDTH = 4


def _rglru_kernel(x_ref, g_ref, cw_ref, cb_ref, wa_ref, ba_ref, wx_ref, bxg_ref, lam_ref, o_ref,
                  xbuf, a_sc, u_sc, h_sc, hcar):
    tc = x_ref.shape[0]

    @pl.when(pl.program_id(1) == 0)
    def _():
        xbuf[0:SUBLANES, :] = jnp.zeros((SUBLANES, W_LRU), F32)
        hcar[...] = jnp.zeros_like(hcar)

    x = x_ref[...]
    xbuf[SUBLANES:SUBLANES + tc, :] = x
    xc = x * cw_ref[CONV_WIDTH - 1:CONV_WIDTH, :] + cb_ref[...]
    for back in range(1, CONV_WIDTH):
        xc = xc + xbuf[SUBLANES - back:SUBLANES - back + tc, :] * cw_ref[CONV_WIDTH - 1 - back:CONV_WIDTH - back, :]
    xbuf[0:SUBLANES, :] = xbuf[tc:tc + SUBLANES, :]

    xcb = xc.astype(BF16)
    r = jax.nn.sigmoid(jnp.dot(xcb, wa_ref[...], preferred_element_type=F32) + ba_ref[...])
    i = jax.nn.sigmoid(jnp.dot(xcb, wx_ref[...], preferred_element_type=F32) + bxg_ref[...])
    lam = lam_ref[...]
    log_sig_lam = -(jnp.maximum(-lam, 0.0) + jnp.log1p(jnp.exp(-jnp.abs(lam))))
    log_a = LRU_C * r * log_sig_lam
    a = jnp.exp(log_a)
    a_sc[...] = a
    one_minus_a2 = jnp.tanh(-log_a) * (a * a + 1.0)
    root = jnp.where(one_minus_a2 > 0.0, one_minus_a2 * lax.rsqrt(one_minus_a2), 0.0)
    u_sc[...] = root * (i * xc)

    rows = lax.broadcasted_iota(jnp.int32, (SUBLANES, W_LRU), 0)

    def step(j, h_prev):
        start = pl.multiple_of(j * SUBLANES, SUBLANES)
        a = a_sc[pl.ds(start, SUBLANES), :]
        b = u_sc[pl.ds(start, SUBLANES), :]
        for shift in (1, 2, 4):
            keep = rows >= shift
            a_prev = jnp.where(keep, pltpu.roll(a, shift, 0), 1.0)
            b_prev = jnp.where(keep, pltpu.roll(b, shift, 0), 0.0)
            b = a * b_prev + b
            a = a * a_prev
        h = a * h_prev + b
        h_sc[pl.ds(start, SUBLANES), :] = h
        return jnp.broadcast_to(h[SUBLANES - 1:SUBLANES, :], (SUBLANES, W_LRU))

    hcar[...] = lax.fori_loop(0, tc // SUBLANES, step, hcar[...], unroll=8)
    o_ref[...] = (h_sc[...] * g_ref[...]).astype(BF16)


def _rg_lru(bx, bg, cw, cb, wa, ba, wx, bxg, lam, batch, seq, tc):
    n = bx.shape[0]
    nchunks = seq // tc
    row = lambda b, c: (b * nchunks + c, 0)
    const = lambda b, c: (0, 0)
    return pl.pallas_call(
        _rglru_kernel,
        grid=(batch, nchunks),
        in_specs=[pl.BlockSpec((tc, W_LRU), row),
                  pl.BlockSpec((tc, W_LRU), row),
                  pl.BlockSpec((CONV_WIDTH, W_LRU), const),
                  pl.BlockSpec((1, W_LRU), const),
                  pl.BlockSpec((W_LRU, W_LRU), const),
                  pl.BlockSpec((1, W_LRU), const),
                  pl.BlockSpec((W_LRU, W_LRU), const),
                  pl.BlockSpec((1, W_LRU), const),
                  pl.BlockSpec((1, W_LRU), const)],
        out_specs=pl.BlockSpec((tc, W_LRU), row),
        out_shape=jax.ShapeDtypeStruct((n, W_LRU), BF16),
        scratch_shapes=[pltpu.VMEM((tc + SUBLANES, W_LRU), F32),
                        pltpu.VMEM((tc, W_LRU), F32),
                        pltpu.VMEM((tc, W_LRU), F32),
                        pltpu.VMEM((tc, W_LRU), F32),
                        pltpu.VMEM((SUBLANES, W_LRU), F32)],
        compiler_params=_cparams(("arbitrary", "arbitrary")),
        name="rg_lru",
    )(bx, bg, cw, cb, wa, ba, wx, bxg, lam)


def _residual_out(x_ref, ya_ref, yb_ref, yc_ref, w_ref):
    acc = x_ref[...]
    acc = acc + jnp.dot(ya_ref[...], w_ref[0:W_DIL, :], preferred_element_type=F32)
    acc = acc + jnp.dot(yb_ref[...], w_ref[W_DIL:W_DIL + W_LRU, :], preferred_element_type=F32)
    return acc + jnp.dot(yc_ref[...], w_ref[W_DIL + W_LRU:, :], preferred_element_type=F32)


def _final_kernel(x_ref, ya_ref, yb_ref, yc_ref, w_ref, fg_ref, o_ref):
    acc = _residual_out(x_ref, ya_ref, yb_ref, yc_ref, w_ref)
    ms = jnp.mean(acc * acc, axis=-1, keepdims=True)
    o_ref[...] = (acc * lax.rsqrt(ms + EPS)) * fg_ref[...]


def _boundary_kernel(x_ref, ya_ref, yb_ref, yc_ref, w_ref, *refs):
    h_ref = refs[5]
    h = _residual_out(x_ref, ya_ref, yb_ref, yc_ref, w_ref)
    h_ref[...] = h
    _project_in(h, *refs[:5], *refs[6:])


def _outproj_specs(tm):
    return [pl.BlockSpec((tm, D_MODEL), _ROW),
            pl.BlockSpec((tm, W_DIL), _ROW),
            pl.BlockSpec((tm, W_LRU), _ROW),
            pl.BlockSpec((tm, W_SB), _ROW),
            pl.BlockSpec((D_MODEL, D_MODEL), _CONST)]


def _final_projection(h, ya, yb, yc, w_bf16, final_gain, tm):
    n = h.shape[0]
    return pl.pallas_call(
        _final_kernel,
        grid=(n // tm,),
        in_specs=_outproj_specs(tm) + [pl.BlockSpec((1, D_MODEL), _CONST)],
        out_specs=pl.BlockSpec((tm, D_MODEL), _ROW),
        out_shape=jax.ShapeDtypeStruct((n, D_MODEL), F32),
        compiler_params=_cparams(("parallel",)),
        name="out_projection",
    )(h, ya, yb, yc, w_bf16, final_gain)


def _boundary_projection(h, ya, yb, yc, w_out_bf16, gain, w_in_bf16, tables, seq, tm):
    n = h.shape[0]
    in_specs, out_specs, out_shapes = _inproj_specs(n, seq, tm)
    return pl.pallas_call(
        _boundary_kernel,
        grid=(n // tm,),
        in_specs=_outproj_specs(tm) + in_specs,
        out_specs=[pl.BlockSpec((tm, D_MODEL), _ROW)] + out_specs,
        out_shape=[jax.ShapeDtypeStruct((n, D_MODEL), F32)] + out_shapes,
        compiler_params=_cparams(("parallel",)),
        name="out_in_projection",
    )(h, ya, yb, yc, w_out_bf16, gain, w_in_bf16, *tables)


def _rope_tables(seq):
    pos = np.arange(seq, dtype=np.float64)
    inv_freq = ROPE_THETA ** (-np.arange(0, ROPE_DIM, 2, dtype=np.float64) / ROPE_DIM)
    ang = pos[:, None] * inv_freq[None, :]
    cos, sin = np.cos(ang), np.sin(ang)
    half = ROPE_DIM // 2
    lane = np.arange(LANES) % HEAD_DIM
    freq = lane % half
    cos_l, sin_l = cos[:, freq], sin[:, freq]
    cos_t = np.where(lane < ROPE_DIM, cos_l, 1.0)
    s1_t = np.where(lane < half, -sin_l, 0.0)
    s2_t = np.where((lane >= half) & (lane < ROPE_DIM), sin_l, 0.0)
    return tuple(jnp.asarray(t, dtype=F32) for t in (cos_t, s1_t, s2_t))


def _block_diagonal(w):
    nb, d, _ = w.shape
    out = jnp.zeros((nb * d, nb * d), w.dtype)
    for i in range(nb):
        out = lax.dynamic_update_slice(out, w[i], (i * d, i * d))
    return out


def kernel(x, norm_gain, w_in, conv_w, conv_b, gate_a_w, gate_a_b, gate_x_w, gate_x_b, lru_lambda, w_out, final_gain):
    batch, seq, d = x.shape
    depth = w_in.shape[0]
    n = batch * seq
    tables = _rope_tables(seq)
    w_in_bf16, w_out_bf16 = w_in.astype(BF16), w_out.astype(BF16)
    tm = 512
    h = x.reshape(n, d)
    projected = _in_projection(h, norm_gain[0].reshape(1, d), w_in_bf16[0], tables, seq, tm)
    for l in range(depth):
        qa, ka, va, ga, bx, bg, qc, kc, vc, gc = projected
        ya = _dilated_attention(qa, ka, va, ga, batch, seq)
        yb = _rg_lru(bx, bg, conv_w[l], conv_b[l].reshape(1, W_LRU),
                     _block_diagonal(gate_a_w[l]).astype(BF16), gate_a_b[l].reshape(1, W_LRU),
                     _block_diagonal(gate_x_w[l]).astype(BF16), gate_x_b[l].reshape(1, W_LRU),
                     lru_lambda[l].reshape(1, W_LRU), batch, seq, tc=512)
        yc = _stick_breaking_attention(qc, kc, vc, gc, batch, seq)
        if l + 1 < depth:
            h, *projected = _boundary_projection(h, ya, yb, yc, w_out_bf16[l], norm_gain[l + 1].reshape(1, d),
                                                 w_in_bf16[l + 1], tables, seq, tm)
        else:
            h = _final_projection(h, ya, yb, yc, w_out_bf16[l], final_gain.reshape(1, d), tm)
    return h.reshape(batch, seq, d)
```

```python
import functools
import math

import jax
import jax.numpy as jnp
import numpy as np
from jax import lax
from jax.experimental import pallas as pl
from jax.experimental.pallas import tpu as pltpu

F32 = jnp.float32
BF16 = jnp.bfloat16

D_MODEL = 1024
HEAD_DIM = 64
W_DIL = 512
W_LRU = 256
W_SB = 256
D_IN = 4 * W_DIL + 2 * W_LRU + 4 * W_SB
ROPE_DIM = 16
ROPE_THETA = 500000.0
DILATED_PATTERNS = ((128, 1), (512, 4), (2048, 16))
LRU_C = 8.0
EPS = 1e-6

LANES = 128
QB = 128
NEG_BIG = -1e30
SB_DEAD_LOG2 = -150.5

VMEM_LIMIT = 56 * 1024 * 1024


def _cparams(sem):
    return pltpu.CompilerParams(dimension_semantics=sem, vmem_limit_bytes=VMEM_LIMIT)


def _inproj_kernel(x_ref, *refs):
    _project_in(x_ref[...], *refs)


def _project_in(x, gain_ref, w_ref, cos_ref, s1_ref, s2_ref,
                qa_ref, ka_ref, va_ref, ga_ref, bx_ref, bg_ref,
                qc_ref, kc_ref, vc_ref, gc_ref):
    ms = jnp.mean(x * x, axis=-1, keepdims=True)
    xn = ((x * lax.rsqrt(ms + EPS)) * gain_ref[...]).astype(BF16)

    def proj(lo, width):
        return jnp.dot(xn, w_ref[:, lo:lo + width], preferred_element_type=F32)

    cos = cos_ref[...]
    s1 = s1_ref[...]
    s2 = s2_ref[...]

    def rope(t):
        parts = []
        for c in range(t.shape[1] // LANES):
            xc = t[:, c * LANES:(c + 1) * LANES]
            parts.append(xc * cos + pltpu.roll(xc, LANES - ROPE_DIM // 2, 1) * s1
                         + pltpu.roll(xc, ROPE_DIM // 2, 1) * s2)
        return jnp.concatenate(parts, axis=1)

    def silu(t):
        return t * jax.nn.sigmoid(t)

    scale = 1.0 / math.sqrt(HEAD_DIM)
    o = 0
    qa_ref[...] = rope(proj(o, W_DIL)) * (scale * math.log2(math.e)); o += W_DIL
    ka_ref[...] = rope(proj(o, W_DIL)); o += W_DIL
    va_ref[...] = proj(o, W_DIL); o += W_DIL
    ga_ref[...] = silu(proj(o, W_DIL)); o += W_DIL
    bx_ref[...] = proj(o, W_LRU); o += W_LRU
    bg_ref[...] = silu(proj(o, W_LRU)); o += W_LRU
    qc_ref[...] = (proj(o, W_SB) * (scale * math.log2(math.e))).astype(BF16); o += W_SB
    kc_ref[...] = proj(o, W_SB).astype(BF16); o += W_SB
    vc_ref[...] = proj(o, W_SB).astype(BF16); o += W_SB
    gc_ref[...] = silu(proj(o, W_SB))


_ROW = lambda i: (i, 0)
_CONST = lambda i: (0, 0)


def _inproj_specs(n, seq, tm):
    blocks_per_seq = seq // tm
    tab = lambda i: (i % blocks_per_seq, 0)
    widths = [(W_DIL, F32), (W_DIL, F32), (W_DIL, F32), (W_DIL, F32), (W_LRU, F32),
              (W_LRU, F32), (W_SB, BF16), (W_SB, BF16), (W_SB, BF16), (W_SB, F32)]
    in_specs = [pl.BlockSpec((1, D_MODEL), _CONST),
                pl.BlockSpec((D_MODEL, D_IN), _CONST),
                pl.BlockSpec((tm, LANES), tab),
                pl.BlockSpec((tm, LANES), tab),
                pl.BlockSpec((tm, LANES), tab)]
    return (in_specs, [pl.BlockSpec((tm, w), _ROW) for w, _ in widths],
            [jax.ShapeDtypeStruct((n, w), dt) for w, dt in widths])


def _in_projection(h, gain, w_bf16, tables, seq, tm):
    n = h.shape[0]
    in_specs, out_specs, out_shapes = _inproj_specs(n, seq, tm)
    return pl.pallas_call(
        _inproj_kernel,
        grid=(n // tm,),
        in_specs=[pl.BlockSpec((tm, D_MODEL), _ROW)] + in_specs,
        out_specs=out_specs,
        out_shape=out_shapes,
        compiler_params=_cparams(("parallel",)),
        name="in_projection",
    )(h, gain, w_bf16, *tables)


def _head_masks():
    lane = lax.broadcasted_iota(jnp.int32, (QB, LANES), 1)
    return lane < HEAD_DIM


SUPER = DILATED_PATTERNS[-1][0]
GROUP = 2


def _dilated_kernel(q_ref, k_ref, v_ref, g_ref, o_ref, k1, v1, k4, v4, k16, v16, stage, m_sc, acc_sc, out_sc):
    span = pl.program_id(2)
    seq = k_ref.shape[0]
    first_head = _head_masks()

    @pl.when(span == 0)
    def _():
        quarter = seq // 4

        def split(j, tiles_per_residue):
            bits = tiles_per_residue.bit_length() - 1
            assert tiles_per_residue == 1 << bits
            return lax.shift_right_logical(j, bits), lax.bitwise_and(j, tiles_per_residue - 1)

        for src_ref, c1, c4, c16 in ((k_ref, k1, k4, k16), (v_ref, v1, v4, v16)):
            def by_four(j, carry):
                dst = pl.ds(pl.multiple_of(j * QB, QB), QB)
                r, tile = split(j, quarter // QB)
                u0 = tile * QB
                regrouped = src_ref[pl.ds(4 * u0 + r, QB, stride=4), :]
                c1[dst, :] = src_ref[dst, :].astype(BF16)
                stage[dst, :] = regrouped
                c4[dst, :] = regrouped.astype(BF16)
                return carry
            lax.fori_loop(0, seq // QB, by_four, 0, unroll=4)

            def by_sixteen(j, carry):
                dst = pl.ds(pl.multiple_of(j * QB, QB), QB)
                r16, tile = split(j, seq // 16 // QB)
                high, low = split(r16, 4)
                src = pl.ds(low * quarter + high + 4 * QB * tile, QB, stride=4)
                c16[dst, :] = stage[src, :].astype(BF16)
                return carry
            lax.fori_loop(0, seq // QB, by_sixteen, 0, unroll=4)

    row = lax.broadcasted_iota(jnp.int32, (QB, 2 * QB), 0)
    col = lax.broadcasted_iota(jnp.int32, (QB, 2 * QB), 1) % QB

    def window_biases(query_pos):
        return jnp.where(col <= query_pos, 0.0, NEG_BIG), jnp.where(col >= query_pos, 0.0, NEG_BIG)

    quarter = QB // 4
    biases_in_order = window_biases(row)
    biases_by_four = window_biases(4 * (row % quarter) + row // quarter)
    head_sel = (jnp.where(first_head, 1.0, 0.0).astype(BF16), jnp.where(first_head, 0.0, 1.0).astype(BF16))

    groups = range(GROUP)

    def gather(ref, chunks, plane=None):
        parts = [ref[c, :] if plane is None else ref[plane, c, :] for c in chunks]
        return parts[0] if len(parts) == 1 else jnp.concatenate(parts, axis=0)

    def scatter(ref, chunks, value, plane=None):
        size = value.shape[0] // len(chunks)
        for n, c in enumerate(chunks):
            if plane is None:
                ref[c, :] = value[n * size:(n + 1) * size]
            else:
                ref[plane, c, :] = value[n * size:(n + 1) * size]

    def score_stage(kc, q_rows, cur_start, prev_valid, biases):
        qt = [gather(q_ref, q_rows[g]).astype(BF16) for g in groups]
        starts, masks = [], []
        for g in groups:
            cur = pl.multiple_of(cur_start[g], QB)
            prev = pl.multiple_of(jnp.maximum(cur_start[g] - QB, 0), QB)
            starts.append((cur, prev))
            known = isinstance(prev_valid[g], bool)
            assert not known or prev_valid[g]
            masks.append((biases[0], biases[1] if known else biases[1] + jnp.where(prev_valid[g], 0.0, NEG_BIG)))
        k_bd = [[jnp.concatenate([kblk * head_sel[0], kblk * head_sel[1]], axis=0)
                 for kblk in (kc[pl.ds(s, QB), :] for s in starts[g])] for g in groups]
        scores = [[lax.dot_general(qt[g], k_bd[g][t], (((1,), (1,)), ((), ())),
                                   preferred_element_type=F32) + masks[g][t] for t in range(2)]
                  for g in groups]
        return starts, scores

    def softmax_stage(vc, q_rows, state_rows, starts, scores, fresh, final):
        if not fresh:
            old = [(gather(m_sc, state_rows[g], 0), gather(m_sc, state_rows[g], 1),
                    jnp.concatenate([gather(acc_sc, state_rows[g], 0), gather(acc_sc, state_rows[g], 1)], axis=1))
                   for g in groups]
        top = [jnp.maximum(scores[g][0], scores[g][1]) for g in groups]
        m_new = []
        for g in groups:
            pair = []
            for h in range(2):
                m_h = jnp.max(top[g][:, h * QB:(h + 1) * QB], axis=-1, keepdims=True)
                if not fresh:
                    m_h = jnp.maximum(old[g][h], m_h)
                pair.append(jnp.broadcast_to(m_h, (QB, QB)))
            m_new.append(pair)
        p = [[jnp.exp2(scores[g][t] - jnp.concatenate(m_new[g], axis=1)).astype(BF16) for t in range(2)]
             for g in groups]
        v_ext = [[jnp.concatenate([jnp.concatenate([vblk * head_sel[0], head_sel[0]], axis=1),
                                   jnp.concatenate([vblk * head_sel[1], head_sel[1]], axis=1)], axis=0)
                  for vblk in (vc[pl.ds(s, QB), :] for s in starts[g])] for g in groups]
        pv = [jnp.dot(p[g][0], v_ext[g][0], preferred_element_type=F32)
              + jnp.dot(p[g][1], v_ext[g][1], preferred_element_type=F32) for g in groups]
        if not fresh:
            alpha = [jnp.exp2(jnp.where(first_head, old[g][0] - m_new[g][0], old[g][1] - m_new[g][1]))
                     for g in groups]
            pv = [pv[g] + jnp.concatenate([alpha[g], alpha[g]], axis=1) * old[g][2] for g in groups]
        for g in groups:
            if final:
                scatter(out_sc, q_rows[g], pv[g][:, :LANES] / pv[g][:, LANES:] * gather(g_ref, q_rows[g]))
            else:
                scatter(m_sc, state_rows[g], m_new[g][0], 0)
                scatter(m_sc, state_rows[g], m_new[g][1], 1)
                scatter(acc_sc, state_rows[g], pv[g][:, :LANES], 0)
                scatter(acc_sc, state_rows[g], pv[g][:, LANES:], 1)

    blocks = SUPER // QB
    span_quarter = SUPER // 4

    def pattern16(r):
        return ([pl.ds(r, QB, stride=16)], [pl.ds((r % 4) * span_quarter + r // 4, QB, stride=4)],
                r * (seq // 16) + span * QB, span > 0, biases_in_order)

    def pattern4(idx):
        r, ub = idx // 4, idx % 4
        return ([pl.ds(ub * (4 * QB) + r, QB, stride=4)], [pl.ds(r * span_quarter + ub * QB, QB)],
                r * (seq // 4) + (span * 4 + ub) * QB, True if ub > 0 else span > 0, biases_in_order)

    def pattern1(i):
        return ([pl.ds(i * QB + c, quarter, stride=4) for c in range(4)],
                [pl.ds(c * span_quarter + i * quarter, quarter) for c in range(4)],
                (span * blocks + i) * QB, True if i > 0 else span > 0, biases_by_four)

    items = [(pattern, kc, vc, fresh, final, first)
             for pattern, kc, vc, fresh, final in ((pattern16, k16, v16, True, False),
                                                   (pattern4, k4, v4, False, False),
                                                   (pattern1, k1, v1, False, True))
             for first in range(0, blocks, GROUP)]

    def scores_of(item):
        pattern, kc, _, _, _, first = item
        q_rows, state_rows, cur_start, prev_valid, biases = zip(*[pattern(first + g) for g in groups])
        return (q_rows, state_rows) + score_stage(kc, q_rows, cur_start, prev_valid, biases[0])

    ahead = scores_of(items[0])
    for k, item in enumerate(items):
        q_rows, state_rows, starts, scores = ahead
        if k + 1 < len(items):
            ahead = scores_of(items[k + 1])
        softmax_stage(item[2], q_rows, state_rows, starts, scores, item[3], item[4])
    o_ref[...] = out_sc[...].astype(BF16)


def _dilated_attention(qa, ka, va, ga, batch, seq):
    n = qa.shape[0]
    pairs = W_DIL // LANES
    spans = seq // SUPER
    qmap = lambda b, p, i: (b * spans + i, p)
    kvmap = lambda b, p, i: (b, p)
    copy = pltpu.VMEM((seq, LANES), BF16)
    return pl.pallas_call(
        _dilated_kernel,
        grid=(batch, pairs, spans),
        in_specs=[pl.BlockSpec((SUPER, LANES), qmap),
                  pl.BlockSpec((seq, LANES), kvmap),
                  pl.BlockSpec((seq, LANES), kvmap),
                  pl.BlockSpec((SUPER, LANES), qmap)],
        out_specs=pl.BlockSpec((SUPER, LANES), qmap),
        out_shape=jax.ShapeDtypeStruct((n, W_DIL), BF16),
        scratch_shapes=([copy] * 6 + [pltpu.VMEM((seq, LANES), F32)] + [pltpu.VMEM((2, SUPER, LANES), F32)] * 2
                        + [pltpu.VMEM((SUPER, LANES), F32)]),
        compiler_params=_cparams(("parallel", "parallel", "arbitrary")),
        name="dilated_attention",
    )(qa, ka, va, ga)


def _suffix_count_matrix(width):
    j = np.arange(width)[:, None]
    s = np.arange(width)[None, :]
    u = (j > s).astype(np.float32)
    z = np.zeros_like(u)
    return np.block([[u, z], [z, u]])


def _log_keep_and_beta(z2):
    log_keep = jnp.log(1.0 / (1.0 + jnp.exp2(-jnp.abs(z2)))) * math.log2(math.e) - jnp.maximum(z2, 0.0)
    return log_keep, z2 + log_keep


SB_GROUP = 8
SB_HALF = QB // 2


def _stick_kernel(q_ref, k_ref, v_ref, g_ref, u_ref, u_half_ref, o_ref):
    step = pl.program_id(2)
    first_head = _head_masks()
    head_sel = (jnp.where(first_head, 1.0, 0.0).astype(BF16), jnp.where(first_head, 0.0, 1.0).astype(BF16))
    suffix_count = {QB: u_ref[...], SB_HALF: u_half_ref[...]}
    row = lax.broadcasted_iota(jnp.int32, (QB, 2 * QB), 0)
    col = lax.broadcasted_iota(jnp.int32, (QB, 2 * QB), 1) % QB
    strictly_causal = col < row

    groups = range(SB_GROUP)

    def key_rows(g, offset, width):
        first = (step * SB_GROUP + g) * QB - offset
        exists = True if isinstance(offset, int) and g * QB >= offset else first >= 0
        return pl.multiple_of(jnp.maximum(first, 0), width), exists

    lane_half = lax.broadcasted_iota(jnp.int32, (SB_HALF, LANES), 1) < HEAD_DIM
    head_sel_half = (jnp.where(lane_half, 1.0, 0.0).astype(BF16), jnp.where(lane_half, 0.0, 1.0).astype(BF16))

    def block_diagonal(ref, g, offset, width):
        sel = head_sel if width == QB else head_sel_half
        blk = ref[pl.ds(key_rows(g, offset, width)[0], width), :]
        return jnp.concatenate([blk * sel[0], blk * sel[1]], axis=0)


    def scores(offset, width):
        return [lax.dot_general(q_ref[g * QB:(g + 1) * QB, :], block_diagonal(k_ref, g, offset, width),
                                (((1,), (1,)), ((), ())), preferred_element_type=F32)
                for g in groups]

    def logs(z, own_block):
        parts = []
        for g in groups:
            log_keep, log_beta = _log_keep_and_beta(z[g])
            if own_block:
                log_keep = jnp.where(strictly_causal, log_keep, 0.0)
            hi = log_keep.astype(BF16)
            parts.append((log_keep, log_beta, hi, (log_keep - hi.astype(F32)).astype(BF16)))
        return parts

    def fold(offset, width, parts, passed, acc, own_block):
        u = suffix_count[width]
        between, new_acc, new_passed, tops = [], [], [], []
        for g in groups:
            passed_g, exists = passed[g], key_rows(g, offset, width)[1]
            if exists is not True:
                passed_g = passed_g + jnp.where(exists, 0.0, NEG_BIG)
            between.append(passed_g + (jnp.dot(parts[g][2], u, preferred_element_type=F32)
                                       + jnp.dot(parts[g][3], u, preferred_element_type=F32)))
        for g in groups:
            w = jnp.exp2(parts[g][1] + between[g])
            if own_block:
                w = jnp.where(strictly_causal, w, 0.0)
            new_acc.append(acc[g] + jnp.dot(w.astype(BF16), block_diagonal(v_ref, g, offset, width),
                                            preferred_element_type=F32))
        for g in groups:
            total = between[g] + parts[g][0]
            total = (total[:, 0:1], total[:, width:width + 1])
            tops.append(jnp.maximum(total[0], total[1]))
            total = [jnp.broadcast_to(t, (QB, LANES)) for t in total]
            new_passed.append(jnp.concatenate(total, axis=1) if width == LANES
                              else jnp.where(first_head, total[0], total[1]))
        live = jnp.max(functools.reduce(jnp.maximum, tops)) > SB_DEAD_LOG2
        return tuple(new_passed), tuple(new_acc), live

    z_own, z_prev = scores(0, QB), scores(QB, QB)
    parts_own, parts_prev = logs(z_own, True), logs(z_prev, False)
    passed, acc, _ = fold(0, QB, parts_own, [jnp.zeros((QB, 2 * QB), F32)] * SB_GROUP,
                          [jnp.zeros((QB, LANES), F32)] * SB_GROUP, True)
    passed, acc, live = fold(QB, QB, parts_prev, passed, acc, False)

    last_block_start = (step * SB_GROUP + SB_GROUP - 1) * QB
    passed = tuple(jnp.where(first_head, p[:, :QB], p[:, QB:]) for p in passed)

    def cond(c):
        halves, live = c[:2]
        return jnp.logical_and(QB + SB_HALF * (halves + 1) <= last_block_start, live)

    def body(c):
        halves, _, passed, acc = c
        offset = QB + SB_HALF * (halves + 1)
        passed, acc, live = fold(offset, SB_HALF, logs(scores(offset, SB_HALF), False), passed, acc, False)
        return halves + 1, live, passed, acc

    acc = lax.while_loop(cond, body, (0, live, passed, acc))[-1]
    for g in groups:
        rows = slice(g * QB, (g + 1) * QB)
        o_ref[rows, :] = (acc[g] * g_ref[rows, :]).astype(BF16)


def _stick_breaking_attention(qc, kc, vc, gc, batch, seq):
    n = qc.shape[0]
    u, u_half = (jnp.asarray(_suffix_count_matrix(w), dtype=BF16) for w in (QB, SB_HALF))
    whole = lambda b, p, i: (0, 0)
    pairs = W_SB // LANES
    rows = SB_GROUP * QB
    steps = seq // rows
    qmap = lambda b, p, i: (b * steps + i, p)
    kvmap = lambda b, p, i: (b, p)
    return pl.pallas_call(
        _stick_kernel,
        grid=(batch, pairs, steps),
        in_specs=[pl.BlockSpec((rows, LANES), qmap),
                  pl.BlockSpec((seq, LANES), kvmap),
                  pl.BlockSpec((seq, LANES), kvmap),
                  pl.BlockSpec((rows, LANES), qmap),
                  pl.BlockSpec(u.shape, whole),
                  pl.BlockSpec(u_half.shape, whole)],
        out_specs=pl.BlockSpec((rows, LANES), qmap),
        out_shape=jax.ShapeDtypeStruct((n, W_SB), BF16),
        compiler_params=_cparams(("parallel", "parallel", "arbitrary")),
        name="stick_breaking_attention",
    )(qc, kc, vc, gc, u, u_half)


SUBLANES = 8
CONV_WIDTH = 4


def _rglru_kernel(x_ref, g_ref, cw_ref, cb_ref, wa_ref, ba_ref, wx_ref, bxg_ref, lam_ref, o_ref,
                  xbuf, a_sc, u_sc, h_sc, hcar):
    tc = x_ref.shape[0]

    @pl.when(pl.program_id(1) == 0)
    def _():
        xbuf[0:SUBLANES, :] = jnp.zeros((SUBLANES, W_LRU), F32)
        hcar[...] = jnp.zeros_like(hcar)

    x = x_ref[...]
    xbuf[SUBLANES:SUBLANES + tc, :] = x
    xc = x * cw_ref[CONV_WIDTH - 1:CONV_WIDTH, :] + cb_ref[...]
    for back in range(1, CONV_WIDTH):
        xc = xc + xbuf[SUBLANES - back:SUBLANES - back + tc, :] * cw_ref[CONV_WIDTH - 1 - back:CONV_WIDTH - back, :]
    xbuf[0:SUBLANES, :] = xbuf[tc:tc + SUBLANES, :]

    xcb = xc.astype(BF16)
    r = jax.nn.sigmoid(jnp.dot(xcb, wa_ref[...], preferred_element_type=F32) + ba_ref[...])
    i = jax.nn.sigmoid(jnp.dot(xcb, wx_ref[...], preferred_element_type=F32) + bxg_ref[...])
    lam = lam_ref[...]
    log_sig_lam = -(jnp.maximum(-lam, 0.0) + jnp.log1p(jnp.exp(-jnp.abs(lam))))
    log_a = LRU_C * r * log_sig_lam
    a = jnp.exp(log_a)
    a_sc[...] = a
    one_minus_a2 = jnp.tanh(-log_a) * (a * a + 1.0)
    root = jnp.where(one_minus_a2 > 0.0, one_minus_a2 * lax.rsqrt(one_minus_a2), 0.0)
    u_sc[...] = root * (i * xc)

    rows = lax.broadcasted_iota(jnp.int32, (SUBLANES, W_LRU), 0)

    def step(j, h_prev):
        start = pl.multiple_of(j * SUBLANES, SUBLANES)
        a = a_sc[pl.ds(start, SUBLANES), :]
        b = u_sc[pl.ds(start, SUBLANES), :]
        for shift in (1, 2, 4):
            keep = rows >= shift
            a_prev = jnp.where(keep, pltpu.roll(a, shift, 0), 1.0)
            b_prev = jnp.where(keep, pltpu.roll(b, shift, 0), 0.0)
            b = a * b_prev + b
            a = a * a_prev
        h = a * h_prev + b
        h_sc[pl.ds(start, SUBLANES), :] = h
        return jnp.broadcast_to(h[SUBLANES - 1:SUBLANES, :], (SUBLANES, W_LRU))

    hcar[...] = lax.fori_loop(0, tc // SUBLANES, step, hcar[...], unroll=8)
    o_ref[...] = (h_sc[...] * g_ref[...]).astype(BF16)


def _rg_lru(bx, bg, cw, cb, wa, ba, wx, bxg, lam, batch, seq, tc):
    n = bx.shape[0]
    nchunks = seq // tc
    row = lambda b, c: (b * nchunks + c, 0)
    const = lambda b, c: (0, 0)
    return pl.pallas_call(
        _rglru_kernel,
        grid=(batch, nchunks),
        in_specs=[pl.BlockSpec((tc, W_LRU), row),
                  pl.BlockSpec((tc, W_LRU), row),
                  pl.BlockSpec((CONV_WIDTH, W_LRU), const),
                  pl.BlockSpec((1, W_LRU), const),
                  pl.BlockSpec((W_LRU, W_LRU), const),
                  pl.BlockSpec((1, W_LRU), const),
                  pl.BlockSpec((W_LRU, W_LRU), const),
                  pl.BlockSpec((1, W_LRU), const),
                  pl.BlockSpec((1, W_LRU), const)],
        out_specs=pl.BlockSpec((tc, W_LRU), row),
        out_shape=jax.ShapeDtypeStruct((n, W_LRU), BF16),
        scratch_shapes=[pltpu.VMEM((tc + SUBLANES, W_LRU), F32),
                        pltpu.VMEM((tc, W_LRU), F32),
                        pltpu.VMEM((tc, W_LRU), F32),
                        pltpu.VMEM((tc, W_LRU), F32),
                        pltpu.VMEM((SUBLANES, W_LRU), F32)],
        compiler_params=_cparams(("arbitrary", "arbitrary")),
        name="rg_lru",
    )(bx, bg, cw, cb, wa, ba, wx, bxg, lam)


def _residual_out(x_ref, ya_ref, yb_ref, yc_ref, w_ref):
    acc = x_ref[...]
    acc = acc + jnp.dot(ya_ref[...], w_ref[0:W_DIL, :], preferred_element_type=F32)
    acc = acc + jnp.dot(yb_ref[...], w_ref[W_DIL:W_DIL + W_LRU, :], preferred_element_type=F32)
    return acc + jnp.dot(yc_ref[...], w_ref[W_DIL + W_LRU:, :], preferred_element_type=F32)


def _final_kernel(x_ref, ya_ref, yb_ref, yc_ref, w_ref, fg_ref, o_ref):
    acc = _residual_out(x_ref, ya_ref, yb_ref, yc_ref, w_ref)
    ms = jnp.mean(acc * acc, axis=-1, keepdims=True)
    o_ref[...] = (acc * lax.rsqrt(ms + EPS)) * fg_ref[...]


def _boundary_kernel(x_ref, ya_ref, yb_ref, yc_ref, w_ref, *refs):
    h_ref = refs[5]
    h = _residual_out(x_ref, ya_ref, yb_ref, yc_ref, w_ref)
    h_ref[...] = h
    _project_in(h, *refs[:5], *refs[6:])


def _outproj_specs(tm):
    return [pl.BlockSpec((tm, D_MODEL), _ROW),
            pl.BlockSpec((tm, W_DIL), _ROW),
            pl.BlockSpec((tm, W_LRU), _ROW),
            pl.BlockSpec((tm, W_SB), _ROW),
            pl.BlockSpec((D_MODEL, D_MODEL), _CONST)]


def _final_projection(h, ya, yb, yc, w_bf16, final_gain, tm):
    n = h.shape[0]
    return pl.pallas_call(
        _final_kernel,
        grid=(n // tm,),
        in_specs=_outproj_specs(tm) + [pl.BlockSpec((1, D_MODEL), _CONST)],
        out_specs=pl.BlockSpec((tm, D_MODEL), _ROW),
        out_shape=jax.ShapeDtypeStruct((n, D_MODEL), F32),
        compiler_params=_cparams(("parallel",)),
        name="out_projection",
    )(h, ya, yb, yc, w_bf16, final_gain)


def _boundary_projection(h, ya, yb, yc, w_out_bf16, gain, w_in_bf16, tables, seq, tm):
    n = h.shape[0]
    in_specs, out_specs, out_shapes = _inproj_specs(n, seq, tm)
    return pl.pallas_call(
        _boundary_kernel,
        grid=(n // tm,),
        in_specs=_outproj_specs(tm) + in_specs,
        out_specs=[pl.BlockSpec((tm, D_MODEL), _ROW)] + out_specs,
        out_shape=[jax.ShapeDtypeStruct((n, D_MODEL), F32)] + out_shapes,
        compiler_params=_cparams(("parallel",)),
        name="out_in_projection",
    )(h, ya, yb, yc, w_out_bf16, gain, w_in_bf16, *tables)


def _rope_tables(seq):
    pos = np.arange(seq, dtype=np.float64)
    inv_freq = ROPE_THETA ** (-np.arange(0, ROPE_DIM, 2, dtype=np.float64) / ROPE_DIM)
    ang = pos[:, None] * inv_freq[None, :]
    cos, sin = np.cos(ang), np.sin(ang)
    half = ROPE_DIM // 2
    lane = np.arange(LANES) % HEAD_DIM
    freq = lane % half
    cos_l, sin_l = cos[:, freq], sin[:, freq]
    cos_t = np.where(lane < ROPE_DIM, cos_l, 1.0)
    s1_t = np.where(lane < half, -sin_l, 0.0)
    s2_t = np.where((lane >= half) & (lane < ROPE_DIM), sin_l, 0.0)
    return tuple(jnp.asarray(t, dtype=F32) for t in (cos_t, s1_t, s2_t))


def _block_diagonal(w):
    nb, d, _ = w.shape
    out = jnp.zeros((nb * d, nb * d), w.dtype)
    for i in range(nb):
        out = lax.dynamic_update_slice(out, w[i], (i * d, i * d))
    return out


def kernel(x, norm_gain, w_in, conv_w, conv_b, gate_a_w, gate_a_b, gate_x_w, gate_x_b, lru_lambda, w_out, final_gain):
    batch, seq, d = x.shape
    depth = w_in.shape[0]
    n = batch * seq
    tables = _rope_tables(seq)
    w_in_bf16, w_out_bf16 = w_in.astype(BF16), w_out.astype(BF16)
    tm = 512
    h = x.reshape(n, d)
    projected = _in_projection(h, norm_gain[0].reshape(1, d), w_in_bf16[0], tables, seq, tm)
    for l in range(depth):
        qa, ka, va, ga, bx, bg, qc, kc, vc, gc = projected
        ya = _dilated_attention(qa, ka, va, ga, batch, seq)
        yb = _rg_lru(bx, bg, conv_w[l], conv_b[l].reshape(1, W_LRU),
                     _block_diagonal(gate_a_w[l]).astype(BF16), gate_a_b[l].reshape(1, W_LRU),
                     _block_diagonal(gate_x_w[l]).astype(BF16), gate_x_b[l].reshape(1, W_LRU),
                     lru_lambda[l].reshape(1, W_LRU), batch, seq, tc=512)
        yc = _stick_breaking_attention(qc, kc, vc, gc, batch, seq)
        if l + 1 < depth:
            h, *projected = _boundary_projection(h, ya, yb, yc, w_out_bf16[l], norm_gain[l + 1].reshape(1, d),
                                                 w_in_bf16[l + 1], tables, seq, tm)
        else:
            h = _final_projection(h, ya, yb, yc, w_out_bf16[l], final_gain.reshape(1, d), tm)
    return h.reshape(batch, seq, d)
```

```python
import functools
import math

import jax
import jax.numpy as jnp
import numpy as np
from jax import lax
from jax.experimental import pallas as pl
from jax.experimental.pallas import tpu as pltpu

F32 = jnp.float32
BF16 = jnp.bfloat16

D_MODEL = 1024
HEAD_DIM = 64
W_DIL = 512
W_LRU = 256
W_SB = 256
D_IN = 4 * W_DIL + 2 * W_LRU + 4 * W_SB
ROPE_DIM = 16
ROPE_THETA = 500000.0
DILATED_PATTERNS = ((128, 1), (512, 4), (2048, 16))
LRU_C = 8.0
EPS = 1e-6

LANES = 128
QB = 128
NEG_BIG = -1e30
SB_DEAD_LOG2 = -150.5

VMEM_LIMIT = 56 * 1024 * 1024


def _cparams(sem):
    return pltpu.CompilerParams(dimension_semantics=sem, vmem_limit_bytes=VMEM_LIMIT)


def _inproj_kernel(x_ref, *refs):
    _project_in(x_ref[...], *refs)


def _project_in(x, gain_ref, w_ref, cos_ref, s1_ref, s2_ref,
                qa_ref, ka_ref, va_ref, ga_ref, bx_ref, bg_ref,
                qc_ref, kc_ref, vc_ref, gc_ref):
    ms = jnp.mean(x * x, axis=-1, keepdims=True)
    xn = ((x * lax.rsqrt(ms + EPS)) * gain_ref[...]).astype(BF16)

    def proj(lo, width):
        return jnp.dot(xn, w_ref[:, lo:lo + width], preferred_element_type=F32)

    cos = cos_ref[...]
    s1 = s1_ref[...]
    s2 = s2_ref[...]

    def rope(t):
        parts = []
        for c in range(t.shape[1] // LANES):
            xc = t[:, c * LANES:(c + 1) * LANES]
            parts.append(xc * cos + pltpu.roll(xc, LANES - ROPE_DIM // 2, 1) * s1
                         + pltpu.roll(xc, ROPE_DIM // 2, 1) * s2)
        return jnp.concatenate(parts, axis=1)

    def silu(t):
        return t * jax.nn.sigmoid(t)

    scale = 1.0 / math.sqrt(HEAD_DIM)
    o = 0
    qa_ref[...] = rope(proj(o, W_DIL)) * (scale * math.log2(math.e)); o += W_DIL
    ka_ref[...] = rope(proj(o, W_DIL)); o += W_DIL
    va_ref[...] = proj(o, W_DIL); o += W_DIL
    ga_ref[...] = silu(proj(o, W_DIL)); o += W_DIL
    bx_ref[...] = proj(o, W_LRU); o += W_LRU
    bg_ref[...] = silu(proj(o, W_LRU)); o += W_LRU
    qc_ref[...] = (proj(o, W_SB) * (scale * math.log2(math.e))).astype(BF16); o += W_SB
    kc_ref[...] = proj(o, W_SB).astype(BF16); o += W_SB
    vc_ref[...] = proj(o, W_SB).astype(BF16); o += W_SB
    gc_ref[...] = silu(proj(o, W_SB))


_ROW = lambda i: (i, 0)
_CONST = lambda i: (0, 0)


def _inproj_specs(n, seq, tm):
    blocks_per_seq = seq // tm
    tab = lambda i: (i % blocks_per_seq, 0)
    widths = [(W_DIL, F32), (W_DIL, F32), (W_DIL, F32), (W_DIL, F32), (W_LRU, F32),
              (W_LRU, F32), (W_SB, BF16), (W_SB, BF16), (W_SB, BF16), (W_SB, F32)]
    in_specs = [pl.BlockSpec((1, D_MODEL), _CONST),
                pl.BlockSpec((D_MODEL, D_IN), _CONST),
                pl.BlockSpec((tm, LANES), tab),
                pl.BlockSpec((tm, LANES), tab),
                pl.BlockSpec((tm, LANES), tab)]
    return (in_specs, [pl.BlockSpec((tm, w), _ROW) for w, _ in widths],
            [jax.ShapeDtypeStruct((n, w), dt) for w, dt in widths])


def _in_projection(h, gain, w_bf16, tables, seq, tm):
    n = h.shape[0]
    in_specs, out_specs, out_shapes = _inproj_specs(n, seq, tm)
    return pl.pallas_call(
        _inproj_kernel,
        grid=(n // tm,),
        in_specs=[pl.BlockSpec((tm, D_MODEL), _ROW)] + in_specs,
        out_specs=out_specs,
        out_shape=out_shapes,
        compiler_params=_cparams(("parallel",)),
        name="in_projection",
    )(h, gain, w_bf16, *tables)


def _head_masks():
    lane = lax.broadcasted_iota(jnp.int32, (QB, LANES), 1)
    return lane < HEAD_DIM


SUPER = DILATED_PATTERNS[-1][0]
GROUP = 2


def _dilated_kernel(q_ref, k_ref, v_ref, g_ref, o_ref, k1, v1, k4, v4, k16, v16, stage, m_sc, acc_sc, out_sc, q_sc):
    span = pl.program_id(2)
    seq = k_ref.shape[0]
    first_head = _head_masks()

    @pl.when(span == 0)
    def _():
        quarter = seq // 4

        def split(j, tiles_per_residue):
            bits = tiles_per_residue.bit_length() - 1
            assert tiles_per_residue == 1 << bits
            return lax.shift_right_logical(j, bits), lax.bitwise_and(j, tiles_per_residue - 1)

        for src_ref, c1, c4, c16 in ((k_ref, k1, k4, k16), (v_ref, v1, v4, v16)):
            def by_four(j, carry):
                dst = pl.ds(pl.multiple_of(j * QB, QB), QB)
                r, tile = split(j, quarter // QB)
                u0 = tile * QB
                regrouped = src_ref[pl.ds(4 * u0 + r, QB, stride=4), :]
                c1[dst, :] = src_ref[dst, :].astype(BF16)
                stage[dst, :] = regrouped
                c4[dst, :] = regrouped.astype(BF16)
                return carry
            lax.fori_loop(0, seq // QB, by_four, 0, unroll=4)

            def by_sixteen(j, carry):
                dst = pl.ds(pl.multiple_of(j * QB, QB), QB)
                r16, tile = split(j, seq // 16 // QB)
                high, low = split(r16, 4)
                src = pl.ds(low * quarter + high + 4 * QB * tile, QB, stride=4)
                c16[dst, :] = stage[src, :].astype(BF16)
                return carry
            lax.fori_loop(0, seq // QB, by_sixteen, 0, unroll=4)

    row = lax.broadcasted_iota(jnp.int32, (QB, 2 * QB), 0)
    col = lax.broadcasted_iota(jnp.int32, (QB, 2 * QB), 1) % QB

    def window_biases(query_pos):
        return jnp.where(col <= query_pos, 0.0, NEG_BIG), jnp.where(col >= query_pos, 0.0, NEG_BIG)

    quarter = QB // 4
    biases_in_order = window_biases(row)
    biases_by_four = window_biases(4 * (row % quarter) + row // quarter)
    head_sel = (jnp.where(first_head, 1.0, 0.0).astype(BF16), jnp.where(first_head, 0.0, 1.0).astype(BF16))

    groups = range(GROUP)

    def gather(ref, chunks, plane=None):
        parts = [ref[c, :] if plane is None else ref[plane, c, :] for c in chunks]
        return parts[0] if len(parts) == 1 else jnp.concatenate(parts, axis=0)

    def scatter(ref, chunks, value, plane=None):
        size = value.shape[0] // len(chunks)
        for n, c in enumerate(chunks):
            if plane is None:
                ref[c, :] = value[n * size:(n + 1) * size]
            else:
                ref[plane, c, :] = value[n * size:(n + 1) * size]

    for j in range(SUPER // QB):
        q_sc[j * QB:(j + 1) * QB, :] = q_ref[pl.ds((j % 4) * (4 * QB) + j // 4, QB, stride=4), :]

    def score_stage(kc, state_rows, cur_start, prev_valid, biases):
        qt = [gather(q_sc, state_rows[g]).astype(BF16) for g in groups]
        starts, masks = [], []
        for g in groups:
            cur = pl.multiple_of(cur_start[g], QB)
            prev = pl.multiple_of(jnp.maximum(cur_start[g] - QB, 0), QB)
            starts.append((cur, prev))
            known = isinstance(prev_valid[g], bool)
            assert not known or prev_valid[g]
            masks.append((biases[0], biases[1] if known else biases[1] + jnp.where(prev_valid[g], 0.0, NEG_BIG)))
        k_bd = [[jnp.concatenate([kblk * head_sel[0], kblk * head_sel[1]], axis=0)
                 for kblk in (kc[pl.ds(s, QB), :] for s in starts[g])] for g in groups]
        scores = [[lax.dot_general(qt[g], k_bd[g][t], (((1,), (1,)), ((), ())),
                                   preferred_element_type=F32) + masks[g][t] for t in range(2)]
                  for g in groups]
        return starts, scores

    def softmax_stage(vc, q_rows, state_rows, starts, scores, fresh, final):
        if not fresh:
            old = [(gather(m_sc, state_rows[g], 0), gather(m_sc, state_rows[g], 1),
                    jnp.concatenate([gather(acc_sc, state_rows[g], 0), gather(acc_sc, state_rows[g], 1)], axis=1))
                   for g in groups]
        top = [jnp.maximum(scores[g][0], scores[g][1]) for g in groups]
        m_new = []
        for g in groups:
            pair = []
            for h in range(2):
                m_h = jnp.max(top[g][:, h * QB:(h + 1) * QB], axis=-1, keepdims=True)
                if not fresh:
                    m_h = jnp.maximum(old[g][h], m_h)
                pair.append(jnp.broadcast_to(m_h, (QB, QB)))
            m_new.append(pair)
        p = [[jnp.exp2(scores[g][t] - jnp.concatenate(m_new[g], axis=1)).astype(BF16) for t in range(2)]
             for g in groups]
        v_ext = [[jnp.concatenate([jnp.concatenate([vblk * head_sel[0], head_sel[0]], axis=1),
                                   jnp.concatenate([vblk * head_sel[1], head_sel[1]], axis=1)], axis=0)
                  for vblk in (vc[pl.ds(s, QB), :] for s in starts[g])] for g in groups]
        pv = [jnp.dot(p[g][0], v_ext[g][0], preferred_element_type=F32)
              + jnp.dot(p[g][1], v_ext[g][1], preferred_element_type=F32) for g in groups]
        if not fresh:
            alpha = [jnp.exp2(jnp.where(first_head, old[g][0] - m_new[g][0], old[g][1] - m_new[g][1]))
                     for g in groups]
            pv = [pv[g] + jnp.concatenate([alpha[g], alpha[g]], axis=1) * old[g][2] for g in groups]
        for g in groups:
            if final:
                scatter(out_sc, q_rows[g], pv[g][:, :LANES] / pv[g][:, LANES:] * gather(g_ref, q_rows[g]))
            else:
                scatter(m_sc, state_rows[g], m_new[g][0], 0)
                scatter(m_sc, state_rows[g], m_new[g][1], 1)
                scatter(acc_sc, state_rows[g], pv[g][:, :LANES], 0)
                scatter(acc_sc, state_rows[g], pv[g][:, LANES:], 1)

    blocks = SUPER // QB
    span_quarter = SUPER // 4

    def pattern16(r):
        return (None, [pl.ds((r % 4) * span_quarter + r // 4, QB, stride=4)],
                r * (seq // 16) + span * QB, span > 0, biases_in_order)

    def pattern4(idx):
        r, ub = idx // 4, idx % 4
        return (None, [pl.ds(r * span_quarter + ub * QB, QB)],
                r * (seq // 4) + (span * 4 + ub) * QB, True if ub > 0 else span > 0, biases_in_order)

    def pattern1(i):
        return ([pl.ds(i * QB + c, quarter, stride=4) for c in range(4)],
                [pl.ds(c * span_quarter + i * quarter, quarter) for c in range(4)],
                (span * blocks + i) * QB, True if i > 0 else span > 0, biases_by_four)

    items = [(pattern, kc, vc, fresh, final, first)
             for pattern, kc, vc, fresh, final in ((pattern16, k16, v16, True, False),
                                                   (pattern4, k4, v4, False, False),
                                                   (pattern1, k1, v1, False, True))
             for first in range(0, blocks, GROUP)]

    def scores_of(item):
        pattern, kc, _, _, _, first = item
        q_rows, state_rows, cur_start, prev_valid, biases = zip(*[pattern(first + g) for g in groups])
        return (q_rows, state_rows) + score_stage(kc, state_rows, cur_start, prev_valid, biases[0])

    ahead = scores_of(items[0])
    for k, item in enumerate(items):
        q_rows, state_rows, starts, scores = ahead
        if k + 1 < len(items):
            ahead = scores_of(items[k + 1])
        softmax_stage(item[2], q_rows, state_rows, starts, scores, item[3], item[4])
    o_ref[...] = out_sc[...].astype(BF16)


def _dilated_attention(qa, ka, va, ga, batch, seq):
    n = qa.shape[0]
    pairs = W_DIL // LANES
    spans = seq // SUPER
    qmap = lambda b, p, i: (b * spans + i, p)
    kvmap = lambda b, p, i: (b, p)
    copy = pltpu.VMEM((seq, LANES), BF16)
    return pl.pallas_call(
        _dilated_kernel,
        grid=(batch, pairs, spans),
        in_specs=[pl.BlockSpec((SUPER, LANES), qmap),
                  pl.BlockSpec((seq, LANES), kvmap),
                  pl.BlockSpec((seq, LANES), kvmap),
                  pl.BlockSpec((SUPER, LANES), qmap)],
        out_specs=pl.BlockSpec((SUPER, LANES), qmap),
        out_shape=jax.ShapeDtypeStruct((n, W_DIL), BF16),
        scratch_shapes=([copy] * 6 + [pltpu.VMEM((seq, LANES), F32)] + [pltpu.VMEM((2, SUPER, LANES), F32)] * 2
                        + [pltpu.VMEM((SUPER, LANES), F32)] * 2),
        compiler_params=_cparams(("parallel", "parallel", "arbitrary")),
        name="dilated_attention",
    )(qa, ka, va, ga)


def _suffix_count_matrix(width):
    j = np.arange(width)[:, None]
    s = np.arange(width)[None, :]
    u = (j > s).astype(np.float32)
    z = np.zeros_like(u)
    return np.block([[u, z], [z, u]])


def _log_keep_and_beta(z2):
    log_keep = jnp.log(1.0 / (1.0 + jnp.exp2(-jnp.abs(z2)))) * math.log2(math.e) - jnp.maximum(z2, 0.0)
    return log_keep, z2 + log_keep


SB_GROUP = 8
SB_HALF = QB // 2


def _stick_kernel(q_ref, k_ref, v_ref, g_ref, u_ref, u_half_ref, o_ref):
    step = pl.program_id(2)
    first_head = _head_masks()
    head_sel = (jnp.where(first_head, 1.0, 0.0).astype(BF16), jnp.where(first_head, 0.0, 1.0).astype(BF16))
    suffix_count = {QB: u_ref[...], SB_HALF: u_half_ref[...]}
    row = lax.broadcasted_iota(jnp.int32, (QB, 2 * QB), 0)
    col = lax.broadcasted_iota(jnp.int32, (QB, 2 * QB), 1) % QB
    strictly_causal = col < row

    groups = range(SB_GROUP)

    def key_rows(g, offset, width):
        first = (step * SB_GROUP + g) * QB - offset
        exists = True if isinstance(offset, int) and g * QB >= offset else first >= 0
        return pl.multiple_of(jnp.maximum(first, 0), width), exists

    lane_half = lax.broadcasted_iota(jnp.int32, (SB_HALF, LANES), 1) < HEAD_DIM
    head_sel_half = (jnp.where(lane_half, 1.0, 0.0).astype(BF16), jnp.where(lane_half, 0.0, 1.0).astype(BF16))

    def block_diagonal(ref, g, offset, width):
        sel = head_sel if width == QB else head_sel_half
        blk = ref[pl.ds(key_rows(g, offset, width)[0], width), :]
        return jnp.concatenate([blk * sel[0], blk * sel[1]], axis=0)


    def scores(offset, width):
        return [lax.dot_general(q_ref[g * QB:(g + 1) * QB, :], block_diagonal(k_ref, g, offset, width),
                                (((1,), (1,)), ((), ())), preferred_element_type=F32)
                for g in groups]

    def logs(z, own_block):
        parts = []
        for g in groups:
            log_keep, log_beta = _log_keep_and_beta(z[g])
            if own_block:
                log_keep = jnp.where(strictly_causal, log_keep, 0.0)
            hi = log_keep.astype(BF16)
            parts.append((log_keep, log_beta, hi, (log_keep - hi.astype(F32)).astype(BF16)))
        return parts

    def fold(offset, width, parts, passed, acc, own_block):
        u = suffix_count[width]
        between, new_acc, new_passed, tops = [], [], [], []
        for g in groups:
            passed_g, exists = passed[g], key_rows(g, offset, width)[1]
            if exists is not True:
                passed_g = passed_g + jnp.where(exists, 0.0, NEG_BIG)
            between.append(passed_g + (jnp.dot(parts[g][2], u, preferred_element_type=F32)
                                       + jnp.dot(parts[g][3], u, preferred_element_type=F32)))
        for g in groups:
            w = jnp.exp2(parts[g][1] + between[g])
            if own_block:
                w = jnp.where(strictly_causal, w, 0.0)
            new_acc.append(acc[g] + jnp.dot(w.astype(BF16), block_diagonal(v_ref, g, offset, width),
                                            preferred_element_type=F32))
        for g in groups:
            total = between[g] + parts[g][0]
            total = (total[:, 0:1], total[:, width:width + 1])
            tops.append(jnp.maximum(total[0], total[1]))
            total = [jnp.broadcast_to(t, (QB, LANES)) for t in total]
            new_passed.append(jnp.concatenate(total, axis=1) if width == LANES
                              else jnp.where(first_head, total[0], total[1]))
        live = jnp.max(functools.reduce(jnp.maximum, tops)) > SB_DEAD_LOG2
        return tuple(new_passed), tuple(new_acc), live

    z_own, z_prev = scores(0, QB), scores(QB, QB)
    parts_own, parts_prev = logs(z_own, True), logs(z_prev, False)
    passed, acc, _ = fold(0, QB, parts_own, [jnp.zeros((QB, 2 * QB), F32)] * SB_GROUP,
                          [jnp.zeros((QB, LANES), F32)] * SB_GROUP, True)
    passed, acc, live = fold(QB, QB, parts_prev, passed, acc, False)

    last_block_start = (step * SB_GROUP + SB_GROUP - 1) * QB
    passed = tuple(jnp.where(first_head, p[:, :QB], p[:, QB:]) for p in passed)

    def cond(c):
        halves, live = c[:2]
        return jnp.logical_and(QB + SB_HALF * (halves + 1) <= last_block_start, live)

    def body(c):
        halves, _, passed, acc = c
        offset = QB + SB_HALF * (halves + 1)
        passed, acc, live = fold(offset, SB_HALF, logs(scores(offset, SB_HALF), False), passed, acc, False)
        return halves + 1, live, passed, acc

    acc = lax.while_loop(cond, body, (0, live, passed, acc))[-1]
    for g in groups:
        rows = slice(g * QB, (g + 1) * QB)
        o_ref[rows, :] = (acc[g] * g_ref[rows, :]).astype(BF16)


def _stick_breaking_attention(qc, kc, vc, gc, batch, seq):
    n = qc.shape[0]
    u, u_half = (jnp.asarray(_suffix_count_matrix(w), dtype=BF16) for w in (QB, SB_HALF))
    whole = lambda b, p, i: (0, 0)
    pairs = W_SB // LANES
    rows = SB_GROUP * QB
    steps = seq // rows
    qmap = lambda b, p, i: (b * steps + i, p)
    kvmap = lambda b, p, i: (b, p)
    return pl.pallas_call(
        _stick_kernel,
        grid=(batch, pairs, steps),
        in_specs=[pl.BlockSpec((rows, LANES), qmap),
                  pl.BlockSpec((seq, LANES), kvmap),
                  pl.BlockSpec((seq, LANES), kvmap),
                  pl.BlockSpec((rows, LANES), qmap),
                  pl.BlockSpec(u.shape, whole),
                  pl.BlockSpec(u_half.shape, whole)],
        out_specs=pl.BlockSpec((rows, LANES), qmap),
        out_shape=jax.ShapeDtypeStruct((n, W_SB), BF16),
        compiler_params=_cparams(("parallel", "parallel", "arbitrary")),
        name="stick_breaking_attention",
    )(qc, kc, vc, gc, u, u_half)


SUBLANES = 8
CONV_WIDTH = 4


def _rglru_kernel(x_ref, g_ref, cw_ref, cb_ref, wa_ref, ba_ref, wx_ref, bxg_ref, lam_ref, o_ref,
                  xbuf, a_sc, u_sc, h_sc, hcar):
    tc = x_ref.shape[0]

    @pl.when(pl.program_id(1) == 0)
    def _():
        xbuf[0:SUBLANES, :] = jnp.zeros((SUBLANES, W_LRU), F32)
        hcar[...] = jnp.zeros_like(hcar)

    x = x_ref[...]
    xbuf[SUBLANES:SUBLANES + tc, :] = x
    xc = x * cw_ref[CONV_WIDTH - 1:CONV_WIDTH, :] + cb_ref[...]
    for back in range(1, CONV_WIDTH):
        xc = xc + xbuf[SUBLANES - back:SUBLANES - back + tc, :] * cw_ref[CONV_WIDTH - 1 - back:CONV_WIDTH - back, :]
    xbuf[0:SUBLANES, :] = xbuf[tc:tc + SUBLANES, :]

    xcb = xc.astype(BF16)
    r = jax.nn.sigmoid(jnp.dot(xcb, wa_ref[...], preferred_element_type=F32) + ba_ref[...])
    i = jax.nn.sigmoid(jnp.dot(xcb, wx_ref[...], preferred_element_type=F32) + bxg_ref[...])
    lam = lam_ref[...]
    log_sig_lam = -(jnp.maximum(-lam, 0.0) + jnp.log1p(jnp.exp(-jnp.abs(lam))))
    log_a = LRU_C * r * log_sig_lam
    a = jnp.exp(log_a)
    a_sc[...] = a
    one_minus_a2 = jnp.tanh(-log_a) * (a * a + 1.0)
    root = jnp.where(one_minus_a2 > 0.0, one_minus_a2 * lax.rsqrt(one_minus_a2), 0.0)
    u_sc[...] = root * (i * xc)

    rows = lax.broadcasted_iota(jnp.int32, (SUBLANES, W_LRU), 0)

    def step(j, h_prev):
        start = pl.multiple_of(j * SUBLANES, SUBLANES)
        a = a_sc[pl.ds(start, SUBLANES), :]
        b = u_sc[pl.ds(start, SUBLANES), :]
        for shift in (1, 2, 4):
            keep = rows >= shift
            a_prev = jnp.where(keep, pltpu.roll(a, shift, 0), 1.0)
            b_prev = jnp.where(keep, pltpu.roll(b, shift, 0), 0.0)
            b = a * b_prev + b
            a = a * a_prev
        h = a * h_prev + b
        h_sc[pl.ds(start, SUBLANES), :] = h
        return jnp.broadcast_to(h[SUBLANES - 1:SUBLANES, :], (SUBLANES, W_LRU))

    hcar[...] = lax.fori_loop(0, tc // SUBLANES, step, hcar[...], unroll=8)
    o_ref[...] = (h_sc[...] * g_ref[...]).astype(BF16)


def _rg_lru(bx, bg, cw, cb, wa, ba, wx, bxg, lam, batch, seq, tc):
    n = bx.shape[0]
    nchunks = seq // tc
    row = lambda b, c: (b * nchunks + c, 0)
    const = lambda b, c: (0, 0)
    return pl.pallas_call(
        _rglru_kernel,
        grid=(batch, nchunks),
        in_specs=[pl.BlockSpec((tc, W_LRU), row),
                  pl.BlockSpec((tc, W_LRU), row),
                  pl.BlockSpec((CONV_WIDTH, W_LRU), const),
                  pl.BlockSpec((1, W_LRU), const),
                  pl.BlockSpec((W_LRU, W_LRU), const),
                  pl.BlockSpec((1, W_LRU), const),
                  pl.BlockSpec((W_LRU, W_LRU), const),
                  pl.BlockSpec((1, W_LRU), const),
                  pl.BlockSpec((1, W_LRU), const)],
        out_specs=pl.BlockSpec((tc, W_LRU), row),
        out_shape=jax.ShapeDtypeStruct((n, W_LRU), BF16),
        scratch_shapes=[pltpu.VMEM((tc + SUBLANES, W_LRU), F32),
                        pltpu.VMEM((tc, W_LRU), F32),
                        pltpu.VMEM((tc, W_LRU), F32),
                        pltpu.VMEM((tc, W_LRU), F32),
                        pltpu.VMEM((SUBLANES, W_LRU), F32)],
        compiler_params=_cparams(("arbitrary", "arbitrary")),
        name="rg_lru",
    )(bx, bg, cw, cb, wa, ba, wx, bxg, lam)


def _residual_out(x_ref, ya_ref, yb_ref, yc_ref, w_ref):
    acc = x_ref[...]
    acc = acc + jnp.dot(ya_ref[...], w_ref[0:W_DIL, :], preferred_element_type=F32)
    acc = acc + jnp.dot(yb_ref[...], w_ref[W_DIL:W_DIL + W_LRU, :], preferred_element_type=F32)
    return acc + jnp.dot(yc_ref[...], w_ref[W_DIL + W_LRU:, :], preferred_element_type=F32)


def _final_kernel(x_ref, ya_ref, yb_ref, yc_ref, w_ref, fg_ref, o_ref):
    acc = _residual_out(x_ref, ya_ref, yb_ref, yc_ref, w_ref)
    ms = jnp.mean(acc * acc, axis=-1, keepdims=True)
    o_ref[...] = (acc * lax.rsqrt(ms + EPS)) * fg_ref[...]


def _boundary_kernel(x_ref, ya_ref, yb_ref, yc_ref, w_ref, *refs):
    h_ref = refs[5]
    h = _residual_out(x_ref, ya_ref, yb_ref, yc_ref, w_ref)
    h_ref[...] = h
    _project_in(h, *refs[:5], *refs[6:])


def _outproj_specs(tm):
    return [pl.BlockSpec((tm, D_MODEL), _ROW),
            pl.BlockSpec((tm, W_DIL), _ROW),
            pl.BlockSpec((tm, W_LRU), _ROW),
            pl.BlockSpec((tm, W_SB), _ROW),
            pl.BlockSpec((D_MODEL, D_MODEL), _CONST)]


def _final_projection(h, ya, yb, yc, w_bf16, final_gain, tm):
    n = h.shape[0]
    return pl.pallas_call(
        _final_kernel,
        grid=(n // tm,),
        in_specs=_outproj_specs(tm) + [pl.BlockSpec((1, D_MODEL), _CONST)],
        out_specs=pl.BlockSpec((tm, D_MODEL), _ROW),
        out_shape=jax.ShapeDtypeStruct((n, D_MODEL), F32),
        compiler_params=_cparams(("parallel",)),
        name="out_projection",
    )(h, ya, yb, yc, w_bf16, final_gain)


def _boundary_projection(h, ya, yb, yc, w_out_bf16, gain, w_in_bf16, tables, seq, tm):
    n = h.shape[0]
    in_specs, out_specs, out_shapes = _inproj_specs(n, seq, tm)
    return pl.pallas_call(
        _boundary_kernel,
        grid=(n // tm,),
        in_specs=_outproj_specs(tm) + in_specs,
        out_specs=[pl.BlockSpec((tm, D_MODEL), _ROW)] + out_specs,
        out_shape=[jax.ShapeDtypeStruct((n, D_MODEL), F32)] + out_shapes,
        compiler_params=_cparams(("parallel",)),
        name="out_in_projection",
    )(h, ya, yb, yc, w_out_bf16, gain, w_in_bf16, *tables)


def _rope_tables(seq):
    pos = np.arange(seq, dtype=np.float64)
    inv_freq = ROPE_THETA ** (-np.arange(0, ROPE_DIM, 2, dtype=np.float64) / ROPE_DIM)
    ang = pos[:, None] * inv_freq[None, :]
    cos, sin = np.cos(ang), np.sin(ang)
    half = ROPE_DIM // 2
    lane = np.arange(LANES) % HEAD_DIM
    freq = lane % half
    cos_l, sin_l = cos[:, freq], sin[:, freq]
    cos_t = np.where(lane < ROPE_DIM, cos_l, 1.0)
    s1_t = np.where(lane < half, -sin_l, 0.0)
    s2_t = np.where((lane >= half) & (lane < ROPE_DIM), sin_l, 0.0)
    return tuple(jnp.asarray(t, dtype=F32) for t in (cos_t, s1_t, s2_t))


def _block_diagonal(w):
    nb, d, _ = w.shape
    out = jnp.zeros((nb * d, nb * d), w.dtype)
    for i in range(nb):
        out = lax.dynamic_update_slice(out, w[i], (i * d, i * d))
    return out


def kernel(x, norm_gain, w_in, conv_w, conv_b, gate_a_w, gate_a_b, gate_x_w, gate_x_b, lru_lambda, w_out, final_gain):
    batch, seq, d = x.shape
    depth = w_in.shape[0]
    n = batch * seq
    tables = _rope_tables(seq)
    w_in_bf16, w_out_bf16 = w_in.astype(BF16), w_out.astype(BF16)
    tm = 512
    h = x.reshape(n, d)
    projected = _in_projection(h, norm_gain[0].reshape(1, d), w_in_bf16[0], tables, seq, tm)
    for l in range(depth):
        qa, ka, va, ga, bx, bg, qc, kc, vc, gc = projected
        ya = _dilated_attention(qa, ka, va, ga, batch, seq)
        yb = _rg_lru(bx, bg, conv_w[l], conv_b[l].reshape(1, W_LRU),
                     _block_diagonal(gate_a_w[l]).astype(BF16), gate_a_b[l].reshape(1, W_LRU),
                     _block_diagonal(gate_x_w[l]).astype(BF16), gate_x_b[l].reshape(1, W_LRU),
                     lru_lambda[l].reshape(1, W_LRU), batch, seq, tc=2048)
        yc = _stick_breaking_attention(qc, kc, vc, gc, batch, seq)
        if l + 1 < depth:
            h, *projected = _boundary_projection(h, ya, yb, yc, w_out_bf16[l], norm_gain[l + 1].reshape(1, d),
                                                 w_in_bf16[l + 1], tables, seq, tm)
        else:
            h = _final_projection(h, ya, yb, yc, w_out_bf16[l], final_gain.reshape(1, d), tm)
    return h.reshape(batch, seq, d)
```

```python
import functools
import math

import jax
import jax.numpy as jnp
import numpy as np
from jax import lax
from jax.experimental import pallas as pl
from jax.experimental.pallas import tpu as pltpu

F32 = jnp.float32
BF16 = jnp.bfloat16

D_MODEL = 1024
HEAD_DIM = 64
W_DIL = 512
W_LRU = 256
W_SB = 256
D_IN = 4 * W_DIL + 2 * W_LRU + 4 * W_SB
ROPE_DIM = 16
ROPE_THETA = 500000.0
DILATED_PATTERNS = ((128, 1), (512, 4), (2048, 16))
LRU_C = 8.0
EPS = 1e-6

LANES = 128
QB = 128
NEG_BIG = -1e30
SB_DEAD_LOG2 = -150.5

VMEM_LIMIT = 56 * 1024 * 1024


def _cparams(sem):
    return pltpu.CompilerParams(dimension_semantics=sem, vmem_limit_bytes=VMEM_LIMIT)


def _inproj_kernel(x_ref, *refs):
    _project_in(x_ref[...], *refs)


def _project_in(x, gain_ref, w_ref, cos_ref, s1_ref, s2_ref,
                qa_ref, ka_ref, va_ref, ga_ref, bx_ref, bg_ref,
                qc_ref, kc_ref, vc_ref, gc_ref):
    ms = jnp.mean(x * x, axis=-1, keepdims=True)
    xn = ((x * lax.rsqrt(ms + EPS)) * gain_ref[...]).astype(BF16)

    def proj(lo, width):
        return jnp.dot(xn, w_ref[:, lo:lo + width], preferred_element_type=F32)

    cos = cos_ref[...]
    s1 = s1_ref[...]
    s2 = s2_ref[...]

    def rope(t):
        parts = []
        for c in range(t.shape[1] // LANES):
            xc = t[:, c * LANES:(c + 1) * LANES]
            parts.append(xc * cos + pltpu.roll(xc, LANES - ROPE_DIM // 2, 1) * s1
                         + pltpu.roll(xc, ROPE_DIM // 2, 1) * s2)
        return jnp.concatenate(parts, axis=1)

    def silu(t):
        return t * jax.nn.sigmoid(t)

    scale = 1.0 / math.sqrt(HEAD_DIM)
    o = 0
    qa_ref[...] = rope(proj(o, W_DIL)) * (scale * math.log2(math.e)); o += W_DIL
    ka_ref[...] = rope(proj(o, W_DIL)); o += W_DIL
    va_ref[...] = proj(o, W_DIL); o += W_DIL
    ga_ref[...] = silu(proj(o, W_DIL)); o += W_DIL
    bx_ref[...] = proj(o, W_LRU); o += W_LRU
    bg_ref[...] = silu(proj(o, W_LRU)); o += W_LRU
    qc_ref[...] = (proj(o, W_SB) * (scale * math.log2(math.e))).astype(BF16); o += W_SB
    kc_ref[...] = proj(o, W_SB).astype(BF16); o += W_SB
    vc_ref[...] = proj(o, W_SB).astype(BF16); o += W_SB
    gc_ref[...] = silu(proj(o, W_SB))


_ROW = lambda i: (i, 0)
_CONST = lambda i: (0, 0)


def _inproj_specs(n, seq, tm):
    blocks_per_seq = seq // tm
    tab = lambda i: (i % blocks_per_seq, 0)
    widths = [(W_DIL, F32), (W_DIL, F32), (W_DIL, F32), (W_DIL, F32), (W_LRU, F32),
              (W_LRU, F32), (W_SB, BF16), (W_SB, BF16), (W_SB, BF16), (W_SB, F32)]
    in_specs = [pl.BlockSpec((1, D_MODEL), _CONST),
                pl.BlockSpec((D_MODEL, D_IN), _CONST),
                pl.BlockSpec((tm, LANES), tab),
                pl.BlockSpec((tm, LANES), tab),
                pl.BlockSpec((tm, LANES), tab)]
    return (in_specs, [pl.BlockSpec((tm, w), _ROW) for w, _ in widths],
            [jax.ShapeDtypeStruct((n, w), dt) for w, dt in widths])


def _in_projection(h, gain, w_bf16, tables, seq, tm):
    n = h.shape[0]
    in_specs, out_specs, out_shapes = _inproj_specs(n, seq, tm)
    return pl.pallas_call(
        _inproj_kernel,
        grid=(n // tm,),
        in_specs=[pl.BlockSpec((tm, D_MODEL), _ROW)] + in_specs,
        out_specs=out_specs,
        out_shape=out_shapes,
        compiler_params=_cparams(("parallel",)),
        name="in_projection",
    )(h, gain, w_bf16, *tables)


def _head_masks():
    lane = lax.broadcasted_iota(jnp.int32, (QB, LANES), 1)
    return lane < HEAD_DIM


SUPER = DILATED_PATTERNS[-1][0]
GROUP = 1
LOOKAHEAD = 1


def _dilated_kernel(q_ref, k_ref, v_ref, g_ref, o_ref, k1, v1, k4, v4, k16, v16, stage, m_sc, acc_sc, out_sc, q_sc):
    span = pl.program_id(2)
    seq = k_ref.shape[0]
    first_head = _head_masks()

    @pl.when(span == 0)
    def _():
        quarter = seq // 4

        def split(j, tiles_per_residue):
            bits = tiles_per_residue.bit_length() - 1
            assert tiles_per_residue == 1 << bits
            return lax.shift_right_logical(j, bits), lax.bitwise_and(j, tiles_per_residue - 1)

        for src_ref, c1, c4, c16 in ((k_ref, k1, k4, k16), (v_ref, v1, v4, v16)):
            def by_four(j, carry):
                dst = pl.ds(pl.multiple_of(j * QB, QB), QB)
                r, tile = split(j, quarter // QB)
                u0 = tile * QB
                regrouped = src_ref[pl.ds(4 * u0 + r, QB, stride=4), :]
                c1[dst, :] = src_ref[dst, :].astype(BF16)
                stage[dst, :] = regrouped
                c4[dst, :] = regrouped.astype(BF16)
                return carry
            lax.fori_loop(0, seq // QB, by_four, 0, unroll=4)

            def by_sixteen(j, carry):
                dst = pl.ds(pl.multiple_of(j * QB, QB), QB)
                r16, tile = split(j, seq // 16 // QB)
                high, low = split(r16, 4)
                src = pl.ds(low * quarter + high + 4 * QB * tile, QB, stride=4)
                c16[dst, :] = stage[src, :].astype(BF16)
                return carry
            lax.fori_loop(0, seq // QB, by_sixteen, 0, unroll=4)

    row = lax.broadcasted_iota(jnp.int32, (QB, 2 * QB), 0)
    col = lax.broadcasted_iota(jnp.int32, (QB, 2 * QB), 1) % QB

    def window_biases(query_pos):
        return jnp.where(col <= query_pos, 0.0, NEG_BIG), jnp.where(col >= query_pos, 0.0, NEG_BIG)

    quarter = QB // 4
    biases_in_order = window_biases(row)
    biases_by_four = window_biases(4 * (row % quarter) + row // quarter)
    head_sel = (jnp.where(first_head, 1.0, 0.0).astype(BF16), jnp.where(first_head, 0.0, 1.0).astype(BF16))

    groups = range(GROUP)

    def gather(ref, chunks, plane=None):
        parts = [ref[c, :] if plane is None else ref[plane, c, :] for c in chunks]
        return parts[0] if len(parts) == 1 else jnp.concatenate(parts, axis=0)

    def scatter(ref, chunks, value, plane=None):
        size = value.shape[0] // len(chunks)
        for n, c in enumerate(chunks):
            if plane is None:
                ref[c, :] = value[n * size:(n + 1) * size]
            else:
                ref[plane, c, :] = value[n * size:(n + 1) * size]

    for j in range(SUPER // QB):
        q_sc[j * QB:(j + 1) * QB, :] = q_ref[pl.ds((j % 4) * (4 * QB) + j // 4, QB, stride=4), :]

    def score_stage(kc, state_rows, cur_start, prev_valid, biases):
        qt = [gather(q_sc, state_rows[g]).astype(BF16) for g in groups]
        starts, masks = [], []
        for g in groups:
            cur = pl.multiple_of(cur_start[g], QB)
            prev = pl.multiple_of(jnp.maximum(cur_start[g] - QB, 0), QB)
            starts.append((cur, prev))
            known = isinstance(prev_valid[g], bool)
            assert not known or prev_valid[g]
            masks.append((biases[0], biases[1] if known else biases[1] + jnp.where(prev_valid[g], 0.0, NEG_BIG)))
        k_bd = [[jnp.concatenate([kblk * head_sel[0], kblk * head_sel[1]], axis=0)
                 for kblk in (kc[pl.ds(s, QB), :] for s in starts[g])] for g in groups]
        scores = [[lax.dot_general(qt[g], k_bd[g][t], (((1,), (1,)), ((), ())),
                                   preferred_element_type=F32) + masks[g][t] for t in range(2)]
                  for g in groups]
        return starts, scores

    def softmax_stage(vc, q_rows, state_rows, starts, scores, fresh, final):
        if not fresh:
            old = [(gather(m_sc, state_rows[g], 0), gather(m_sc, state_rows[g], 1),
                    jnp.concatenate([gather(acc_sc, state_rows[g], 0), gather(acc_sc, state_rows[g], 1)], axis=1))
                   for g in groups]
        top = [jnp.maximum(scores[g][0], scores[g][1]) for g in groups]
        m_new = []
        for g in groups:
            pair = []
            for h in range(2):
                m_h = jnp.max(top[g][:, h * QB:(h + 1) * QB], axis=-1, keepdims=True)
                if not fresh:
                    m_h = jnp.maximum(old[g][h], m_h)
                pair.append(jnp.broadcast_to(m_h, (QB, QB)))
            m_new.append(pair)
        p = [[jnp.exp2(scores[g][t] - jnp.concatenate(m_new[g], axis=1)).astype(BF16) for t in range(2)]
             for g in groups]
        v_ext = [[jnp.concatenate([jnp.concatenate([vblk * head_sel[0], head_sel[0]], axis=1),
                                   jnp.concatenate([vblk * head_sel[1], head_sel[1]], axis=1)], axis=0)
                  for vblk in (vc[pl.ds(s, QB), :] for s in starts[g])] for g in groups]
        pv = [jnp.dot(p[g][0], v_ext[g][0], preferred_element_type=F32)
              + jnp.dot(p[g][1], v_ext[g][1], preferred_element_type=F32) for g in groups]
        if not fresh:
            alpha = [jnp.exp2(jnp.where(first_head, old[g][0] - m_new[g][0], old[g][1] - m_new[g][1]))
                     for g in groups]
            pv = [pv[g] + jnp.concatenate([alpha[g], alpha[g]], axis=1) * old[g][2] for g in groups]
        for g in groups:
            if final:
                scatter(out_sc, q_rows[g], pv[g][:, :LANES] / pv[g][:, LANES:] * gather(g_ref, q_rows[g]))
            else:
                scatter(m_sc, state_rows[g], m_new[g][0], 0)
                scatter(m_sc, state_rows[g], m_new[g][1], 1)
                scatter(acc_sc, state_rows[g], pv[g][:, :LANES], 0)
                scatter(acc_sc, state_rows[g], pv[g][:, LANES:], 1)

    blocks = SUPER // QB
    span_quarter = SUPER // 4

    def pattern16(r):
        return (None, [pl.ds((r % 4) * span_quarter + r // 4, QB, stride=4)],
                r * (seq // 16) + span * QB, span > 0, biases_in_order)

    def pattern4(idx):
        r, ub = idx // 4, idx % 4
        return (None, [pl.ds(r * span_quarter + ub * QB, QB)],
                r * (seq // 4) + (span * 4 + ub) * QB, True if ub > 0 else span > 0, biases_in_order)

    def pattern1(i):
        return ([pl.ds(i * QB + c, quarter, stride=4) for c in range(4)],
                [pl.ds(c * span_quarter + i * quarter, quarter) for c in range(4)],
                (span * blocks + i) * QB, True if i > 0 else span > 0, biases_by_four)

    items = [(pattern, kc, vc, fresh, final, first)
             for pattern, kc, vc, fresh, final in ((pattern16, k16, v16, True, False),
                                                   (pattern4, k4, v4, False, False),
                                                   (pattern1, k1, v1, False, True))
             for first in range(0, blocks, GROUP)]

    def scores_of(item):
        pattern, kc, _, _, _, first = item
        q_rows, state_rows, cur_start, prev_valid, biases = zip(*[pattern(first + g) for g in groups])
        return (q_rows, state_rows) + score_stage(kc, state_rows, cur_start, prev_valid, biases[0])

    ahead = [scores_of(item) for item in items[:LOOKAHEAD]]
    for k, item in enumerate(items):
        q_rows, state_rows, starts, scores = ahead.pop(0)
        if k + LOOKAHEAD < len(items):
            ahead.append(scores_of(items[k + LOOKAHEAD]))
        softmax_stage(item[2], q_rows, state_rows, starts, scores, item[3], item[4])
    o_ref[...] = out_sc[...].astype(BF16)


def _dilated_attention(qa, ka, va, ga, batch, seq):
    n = qa.shape[0]
    pairs = W_DIL // LANES
    spans = seq // SUPER
    qmap = lambda b, p, i: (b * spans + i, p)
    kvmap = lambda b, p, i: (b, p)
    copy = pltpu.VMEM((seq, LANES), BF16)
    return pl.pallas_call(
        _dilated_kernel,
        grid=(batch, pairs, spans),
        in_specs=[pl.BlockSpec((SUPER, LANES), qmap),
                  pl.BlockSpec((seq, LANES), kvmap),
                  pl.BlockSpec((seq, LANES), kvmap),
                  pl.BlockSpec((SUPER, LANES), qmap)],
        out_specs=pl.BlockSpec((SUPER, LANES), qmap),
        out_shape=jax.ShapeDtypeStruct((n, W_DIL), BF16),
        scratch_shapes=([copy] * 6 + [pltpu.VMEM((seq, LANES), F32)] + [pltpu.VMEM((2, SUPER, LANES), F32)] * 2
                        + [pltpu.VMEM((SUPER, LANES), F32)] * 2),
        compiler_params=_cparams(("parallel", "parallel", "arbitrary")),
        name="dilated_attention",
    )(qa, ka, va, ga)


def _suffix_count_matrix(width):
    j = np.arange(width)[:, None]
    s = np.arange(width)[None, :]
    u = (j > s).astype(np.float32)
    z = np.zeros_like(u)
    return np.block([[u, z], [z, u]])


def _log_keep_and_beta(z2):
    log_keep = jnp.log(1.0 / (1.0 + jnp.exp2(-jnp.abs(z2)))) * math.log2(math.e) - jnp.maximum(z2, 0.0)
    return log_keep, z2 + log_keep


SB_GROUP = 8
SB_HALF = QB // 2


def _stick_kernel(q_ref, k_ref, v_ref, g_ref, u_ref, u_half_ref, o_ref):
    step = pl.program_id(2)
    first_head = _head_masks()
    head_sel = (jnp.where(first_head, 1.0, 0.0).astype(BF16), jnp.where(first_head, 0.0, 1.0).astype(BF16))
    suffix_count = {QB: u_ref[...], SB_HALF: u_half_ref[...]}
    row = lax.broadcasted_iota(jnp.int32, (QB, 2 * QB), 0)
    col = lax.broadcasted_iota(jnp.int32, (QB, 2 * QB), 1) % QB
    strictly_causal = col < row

    groups = range(SB_GROUP)

    def key_rows(g, offset, width):
        first = (step * SB_GROUP + g) * QB - offset
        exists = True if isinstance(offset, int) and g * QB >= offset else first >= 0
        return pl.multiple_of(jnp.maximum(first, 0), width), exists

    lane_half = lax.broadcasted_iota(jnp.int32, (SB_HALF, LANES), 1) < HEAD_DIM
    head_sel_half = (jnp.where(lane_half, 1.0, 0.0).astype(BF16), jnp.where(lane_half, 0.0, 1.0).astype(BF16))

    def block_diagonal(ref, g, offset, width):
        sel = head_sel if width == QB else head_sel_half
        blk = ref[pl.ds(key_rows(g, offset, width)[0], width), :]
        return jnp.concatenate([blk * sel[0], blk * sel[1]], axis=0)


    def scores(offset, width):
        return [lax.dot_general(q_ref[g * QB:(g + 1) * QB, :], block_diagonal(k_ref, g, offset, width),
                                (((1,), (1,)), ((), ())), preferred_element_type=F32)
                for g in groups]

    def logs(z, own_block):
        parts = []
        for g in groups:
            log_keep, log_beta = _log_keep_and_beta(z[g])
            if own_block:
                log_keep = jnp.where(strictly_causal, log_keep, 0.0)
            hi = log_keep.astype(BF16)
            parts.append((log_keep, log_beta, hi, (log_keep - hi.astype(F32)).astype(BF16)))
        return parts

    def fold(offset, width, parts, passed, acc, own_block):
        u = suffix_count[width]
        between, new_acc, new_passed, tops = [], [], [], []
        for g in groups:
            passed_g, exists = passed[g], key_rows(g, offset, width)[1]
            if exists is not True:
                passed_g = passed_g + jnp.where(exists, 0.0, NEG_BIG)
            between.append(passed_g + (jnp.dot(parts[g][2], u, preferred_element_type=F32)
                                       + jnp.dot(parts[g][3], u, preferred_element_type=F32)))
        for g in groups:
            w = jnp.exp2(parts[g][1] + between[g])
            if own_block:
                w = jnp.where(strictly_causal, w, 0.0)
            new_acc.append(acc[g] + jnp.dot(w.astype(BF16), block_diagonal(v_ref, g, offset, width),
                                            preferred_element_type=F32))
        for g in groups:
            total = between[g] + parts[g][0]
            total = (total[:, 0:1], total[:, width:width + 1])
            tops.append(jnp.maximum(total[0], total[1]))
            total = [jnp.broadcast_to(t, (QB, LANES)) for t in total]
            new_passed.append(jnp.concatenate(total, axis=1) if width == LANES
                              else jnp.where(first_head, total[0], total[1]))
        live = jnp.max(functools.reduce(jnp.maximum, tops)) > SB_DEAD_LOG2
        return tuple(new_passed), tuple(new_acc), live

    z_own, z_prev = scores(0, QB), scores(QB, QB)
    parts_own, parts_prev = logs(z_own, True), logs(z_prev, False)
    passed, acc, _ = fold(0, QB, parts_own, [jnp.zeros((QB, 2 * QB), F32)] * SB_GROUP,
                          [jnp.zeros((QB, LANES), F32)] * SB_GROUP, True)
    passed, acc, live = fold(QB, QB, parts_prev, passed, acc, False)

    last_block_start = (step * SB_GROUP + SB_GROUP - 1) * QB
    passed = tuple(jnp.where(first_head, p[:, :QB], p[:, QB:]) for p in passed)

    def cond(c):
        halves, live = c[:2]
        return jnp.logical_and(QB + SB_HALF * (halves + 1) <= last_block_start, live)

    def body(c):
        halves, _, passed, acc = c
        offset = QB + SB_HALF * (halves + 1)
        passed, acc, live = fold(offset, SB_HALF, logs(scores(offset, SB_HALF), False), passed, acc, False)
        return halves + 1, live, passed, acc

    acc = lax.while_loop(cond, body, (0, live, passed, acc))[-1]
    for g in groups:
        rows = slice(g * QB, (g + 1) * QB)
        o_ref[rows, :] = (acc[g] * g_ref[rows, :]).astype(BF16)


def _stick_breaking_attention(qc, kc, vc, gc, batch, seq):
    n = qc.shape[0]
    u, u_half = (jnp.asarray(_suffix_count_matrix(w), dtype=BF16) for w in (QB, SB_HALF))
    whole = lambda b, p, i: (0, 0)
    pairs = W_SB // LANES
    rows = SB_GROUP * QB
    steps = seq // rows
    qmap = lambda b, p, i: (b * steps + i, p)
    kvmap = lambda b, p, i: (b, p)
    return pl.pallas_call(
        _stick_kernel,
        grid=(batch, pairs, steps),
        in_specs=[pl.BlockSpec((rows, LANES), qmap),
                  pl.BlockSpec((seq, LANES), kvmap),
                  pl.BlockSpec((seq, LANES), kvmap),
                  pl.BlockSpec((rows, LANES), qmap),
                  pl.BlockSpec(u.shape, whole),
                  pl.BlockSpec(u_half.shape, whole)],
        out_specs=pl.BlockSpec((rows, LANES), qmap),
        out_shape=jax.ShapeDtypeStruct((n, W_SB), BF16),
        compiler_params=_cparams(("parallel", "parallel", "arbitrary")),
        name="stick_breaking_attention",
    )(qc, kc, vc, gc, u, u_half)


SUBLANES = 8
CONV_WIDTH = 4


def _rglru_kernel(x_ref, g_ref, cw_ref, cb_ref, wa_ref, ba_ref, wx_ref, bxg_ref, lam_ref, o_ref,
                  xbuf, a_sc, u_sc, h_sc, hcar):
    tc = x_ref.shape[0]

    @pl.when(pl.program_id(1) == 0)
    def _():
        xbuf[0:SUBLANES, :] = jnp.zeros((SUBLANES, W_LRU), F32)
        hcar[...] = jnp.zeros_like(hcar)

    x = x_ref[...]
    xbuf[SUBLANES:SUBLANES + tc, :] = x
    xc = x * cw_ref[CONV_WIDTH - 1:CONV_WIDTH, :] + cb_ref[...]
    for back in range(1, CONV_WIDTH):
        xc = xc + xbuf[SUBLANES - back:SUBLANES - back + tc, :] * cw_ref[CONV_WIDTH - 1 - back:CONV_WIDTH - back, :]
    xbuf[0:SUBLANES, :] = xbuf[tc:tc + SUBLANES, :]

    xcb = xc.astype(BF16)
    r = jax.nn.sigmoid(jnp.dot(xcb, wa_ref[...], preferred_element_type=F32) + ba_ref[...])
    i = jax.nn.sigmoid(jnp.dot(xcb, wx_ref[...], preferred_element_type=F32) + bxg_ref[...])
    lam = lam_ref[...]
    log_sig_lam = -(jnp.maximum(-lam, 0.0) + jnp.log1p(jnp.exp(-jnp.abs(lam))))
    log_a = LRU_C * r * log_sig_lam
    a = jnp.exp(log_a)
    a_sc[...] = a
    one_minus_a2 = jnp.tanh(-log_a) * (a * a + 1.0)
    root = jnp.where(one_minus_a2 > 0.0, one_minus_a2 * lax.rsqrt(one_minus_a2), 0.0)
    u_sc[...] = root * (i * xc)

    rows = lax.broadcasted_iota(jnp.int32, (SUBLANES, W_LRU), 0)

    def step(j, h_prev):
        start = pl.multiple_of(j * SUBLANES, SUBLANES)
        a = a_sc[pl.ds(start, SUBLANES), :]
        b = u_sc[pl.ds(start, SUBLANES), :]
        for shift in (1, 2, 4):
            keep = rows >= shift
            a_prev = jnp.where(keep, pltpu.roll(a, shift, 0), 1.0)
            b_prev = jnp.where(keep, pltpu.roll(b, shift, 0), 0.0)
            b = a * b_prev + b
            a = a * a_prev
        h = a * h_prev + b
        h_sc[pl.ds(start, SUBLANES), :] = h
        return jnp.broadcast_to(h[SUBLANES - 1:SUBLANES, :], (SUBLANES, W_LRU))

    hcar[...] = lax.fori_loop(0, tc // SUBLANES, step, hcar[...], unroll=8)
    o_ref[...] = (h_sc[...] * g_ref[...]).astype(BF16)


def _rg_lru(bx, bg, cw, cb, wa, ba, wx, bxg, lam, batch, seq, tc):
    n = bx.shape[0]
    nchunks = seq // tc
    row = lambda b, c: (b * nchunks + c, 0)
    const = lambda b, c: (0, 0)
    return pl.pallas_call(
        _rglru_kernel,
        grid=(batch, nchunks),
        in_specs=[pl.BlockSpec((tc, W_LRU), row),
                  pl.BlockSpec((tc, W_LRU), row),
                  pl.BlockSpec((CONV_WIDTH, W_LRU), const),
                  pl.BlockSpec((1, W_LRU), const),
                  pl.BlockSpec((W_LRU, W_LRU), const),
                  pl.BlockSpec((1, W_LRU), const),
                  pl.BlockSpec((W_LRU, W_LRU), const),
                  pl.BlockSpec((1, W_LRU), const),
                  pl.BlockSpec((1, W_LRU), const)],
        out_specs=pl.BlockSpec((tc, W_LRU), row),
        out_shape=jax.ShapeDtypeStruct((n, W_LRU), BF16),
        scratch_shapes=[pltpu.VMEM((tc + SUBLANES, W_LRU), F32),
                        pltpu.VMEM((tc, W_LRU), F32),
                        pltpu.VMEM((tc, W_LRU), F32),
                        pltpu.VMEM((tc, W_LRU), F32),
                        pltpu.VMEM((SUBLANES, W_LRU), F32)],
        compiler_params=_cparams(("arbitrary", "arbitrary")),
        name="rg_lru",
    )(bx, bg, cw, cb, wa, ba, wx, bxg, lam)


def _residual_out(x_ref, ya_ref, yb_ref, yc_ref, w_ref):
    acc = x_ref[...]
    acc = acc + jnp.dot(ya_ref[...], w_ref[0:W_DIL, :], preferred_element_type=F32)
    acc = acc + jnp.dot(yb_ref[...], w_ref[W_DIL:W_DIL + W_LRU, :], preferred_element_type=F32)
    return acc + jnp.dot(yc_ref[...], w_ref[W_DIL + W_LRU:, :], preferred_element_type=F32)


def _final_kernel(x_ref, ya_ref, yb_ref, yc_ref, w_ref, fg_ref, o_ref):
    acc = _residual_out(x_ref, ya_ref, yb_ref, yc_ref, w_ref)
    ms = jnp.mean(acc * acc, axis=-1, keepdims=True)
    o_ref[...] = (acc * lax.rsqrt(ms + EPS)) * fg_ref[...]


def _boundary_kernel(x_ref, ya_ref, yb_ref, yc_ref, w_ref, *refs):
    h_ref = refs[5]
    h = _residual_out(x_ref, ya_ref, yb_ref, yc_ref, w_ref)
    h_ref[...] = h
    _project_in(h, *refs[:5], *refs[6:])


def _outproj_specs(tm):
    return [pl.BlockSpec((tm, D_MODEL), _ROW),
            pl.BlockSpec((tm, W_DIL), _ROW),
            pl.BlockSpec((tm, W_LRU), _ROW),
            pl.BlockSpec((tm, W_SB), _ROW),
            pl.BlockSpec((D_MODEL, D_MODEL), _CONST)]


def _final_projection(h, ya, yb, yc, w_bf16, final_gain, tm):
    n = h.shape[0]
    return pl.pallas_call(
        _final_kernel,
        grid=(n // tm,),
        in_specs=_outproj_specs(tm) + [pl.BlockSpec((1, D_MODEL), _CONST)],
        out_specs=pl.BlockSpec((tm, D_MODEL), _ROW),
        out_shape=jax.ShapeDtypeStruct((n, D_MODEL), F32),
        compiler_params=_cparams(("parallel",)),
        name="out_projection",
    )(h, ya, yb, yc, w_bf16, final_gain)


def _boundary_projection(h, ya, yb, yc, w_out_bf16, gain, w_in_bf16, tables, seq, tm):
    n = h.shape[0]
    in_specs, out_specs, out_shapes = _inproj_specs(n, seq, tm)
    return pl.pallas_call(
        _boundary_kernel,
        grid=(n // tm,),
        in_specs=_outproj_specs(tm) + in_specs,
        out_specs=[pl.BlockSpec((tm, D_MODEL), _ROW)] + out_specs,
        out_shape=[jax.ShapeDtypeStruct((n, D_MODEL), F32)] + out_shapes,
        compiler_params=_cparams(("parallel",)),
        name="out_in_projection",
    )(h, ya, yb, yc, w_out_bf16, gain, w_in_bf16, *tables)


def _rope_tables(seq):
    pos = np.arange(seq, dtype=np.float64)
    inv_freq = ROPE_THETA ** (-np.arange(0, ROPE_DIM, 2, dtype=np.float64) / ROPE_DIM)
    ang = pos[:, None] * inv_freq[None, :]
    cos, sin = np.cos(ang), np.sin(ang)
    half = ROPE_DIM // 2
    lane = np.arange(LANES) % HEAD_DIM
    freq = lane % half
    cos_l, sin_l = cos[:, freq], sin[:, freq]
    cos_t = np.where(lane < ROPE_DIM, cos_l, 1.0)
    s1_t = np.where(lane < half, -sin_l, 0.0)
    s2_t = np.where((lane >= half) & (lane < ROPE_DIM), sin_l, 0.0)
    return tuple(jnp.asarray(t, dtype=F32) for t in (cos_t, s1_t, s2_t))


def _block_diagonal(w):
    nb, d, _ = w.shape
    out = jnp.zeros((nb * d, nb * d), w.dtype)
    for i in range(nb):
        out = lax.dynamic_update_slice(out, w[i], (i * d, i * d))
    return out


def kernel(x, norm_gain, w_in, conv_w, conv_b, gate_a_w, gate_a_b, gate_x_w, gate_x_b, lru_lambda, w_out, final_gain):
    batch, seq, d = x.shape
    depth = w_in.shape[0]
    n = batch * seq
    tables = _rope_tables(seq)
    w_in_bf16, w_out_bf16 = w_in.astype(BF16), w_out.astype(BF16)
    tm = 512
    h = x.reshape(n, d)
    projected = _in_projection(h, norm_gain[0].reshape(1, d), w_in_bf16[0], tables, seq, tm)
    for l in range(depth):
        qa, ka, va, ga, bx, bg, qc, kc, vc, gc = projected
        ya = _dilated_attention(qa, ka, va, ga, batch, seq)
        yb = _rg_lru(bx, bg, conv_w[l], conv_b[l].reshape(1, W_LRU),
                     _block_diagonal(gate_a_w[l]).astype(BF16), gate_a_b[l].reshape(1, W_LRU),
                     _block_diagonal(gate_x_w[l]).astype(BF16), gate_x_b[l].reshape(1, W_LRU),
                     lru_lambda[l].reshape(1, W_LRU), batch, seq, tc=2048)
        yc = _stick_breaking_attention(qc, kc, vc, gc, batch, seq)
        if l + 1 < depth:
            h, *projected = _boundary_projection(h, ya, yb, yc, w_out_bf16[l], norm_gain[l + 1].reshape(1, d),
                                                 w_in_bf16[l + 1], tables, seq, tm)
        else:
            h = _final_projection(h, ya, yb, yc, w_out_bf16[l], final_gain.reshape(1, d), 2 * tm)
    return h.reshape(batch, seq, d)
```

```python
import functools
import math

import jax
import jax.numpy as jnp
import numpy as np
from jax import lax
from jax.experimental import pallas as pl
from jax.experimental.pallas import tpu as pltpu

F32 = jnp.float32
BF16 = jnp.bfloat16

D_MODEL = 1024
HEAD_DIM = 64
W_DIL = 512
W_LRU = 256
W_SB = 256
D_IN = 4 * W_DIL + 2 * W_LRU + 4 * W_SB
ROPE_DIM = 16
ROPE_THETA = 500000.0
DILATED_PATTERNS = ((128, 1), (512, 4), (2048, 16))
LRU_C = 8.0
EPS = 1e-6

LANES = 128
QB = 128
NEG_BIG = -1e30
SB_DEAD_LOG2 = -150.5

VMEM_LIMIT = 56 * 1024 * 1024


def _cparams(sem):
    return pltpu.CompilerParams(dimension_semantics=sem, vmem_limit_bytes=VMEM_LIMIT)


def _inproj_kernel(x_ref, *refs):
    _project_in(x_ref[...], *refs)


def _project_in(x, gain_ref, w_ref, cos_ref, s1_ref, s2_ref,
                qa_ref, ka_ref, va_ref, ga_ref, bx_ref, bg_ref,
                qc_ref, kc_ref, vc_ref, gc_ref):
    ms = jnp.mean(x * x, axis=-1, keepdims=True)
    xn = ((x * lax.rsqrt(ms + EPS)) * gain_ref[...]).astype(BF16)

    def proj(lo, width):
        return jnp.dot(xn, w_ref[:, lo:lo + width], preferred_element_type=F32)

    cos = cos_ref[...]
    s1 = s1_ref[...]
    s2 = s2_ref[...]

    def rope(t):
        parts = []
        for c in range(t.shape[1] // LANES):
            xc = t[:, c * LANES:(c + 1) * LANES]
            parts.append(xc * cos + pltpu.roll(xc, LANES - ROPE_DIM // 2, 1) * s1
                         + pltpu.roll(xc, ROPE_DIM // 2, 1) * s2)
        return jnp.concatenate(parts, axis=1)

    def silu(t):
        return t * jax.nn.sigmoid(t)

    scale = 1.0 / math.sqrt(HEAD_DIM)
    o = 0
    qa_ref[...] = rope(proj(o, W_DIL)) * (scale * math.log2(math.e)); o += W_DIL
    ka_ref[...] = rope(proj(o, W_DIL)); o += W_DIL
    va_ref[...] = proj(o, W_DIL); o += W_DIL
    ga_ref[...] = silu(proj(o, W_DIL)); o += W_DIL
    bx_ref[...] = proj(o, W_LRU); o += W_LRU
    bg_ref[...] = silu(proj(o, W_LRU)); o += W_LRU
    qc_ref[...] = (proj(o, W_SB) * (scale * math.log2(math.e))).astype(BF16); o += W_SB
    kc_ref[...] = proj(o, W_SB).astype(BF16); o += W_SB
    vc_ref[...] = proj(o, W_SB).astype(BF16); o += W_SB
    gc_ref[...] = silu(proj(o, W_SB))


_ROW = lambda i: (i, 0)
_CONST = lambda i: (0, 0)


def _inproj_specs(n, seq, tm, layer):
    blocks_per_seq = seq // tm
    tab = lambda i: (i % blocks_per_seq, 0)
    widths = [(W_DIL, F32), (W_DIL, F32), (W_DIL, F32), (W_DIL, F32), (W_LRU, F32),
              (W_LRU, F32), (W_SB, BF16), (W_SB, BF16), (W_SB, BF16), (W_SB, F32)]
    in_specs = [pl.BlockSpec((1, D_MODEL), _CONST),
                pl.BlockSpec((None, D_MODEL, D_IN), lambda i: (layer, 0, 0)),
                pl.BlockSpec((tm, LANES), tab),
                pl.BlockSpec((tm, LANES), tab),
                pl.BlockSpec((tm, LANES), tab)]
    return (in_specs, [pl.BlockSpec((tm, w), _ROW) for w, _ in widths],
            [jax.ShapeDtypeStruct((n, w), dt) for w, dt in widths])


def _in_projection(h, gain, w_bf16, layer, tables, seq, tm):
    n = h.shape[0]
    in_specs, out_specs, out_shapes = _inproj_specs(n, seq, tm, layer)
    return pl.pallas_call(
        _inproj_kernel,
        grid=(n // tm,),
        in_specs=[pl.BlockSpec((tm, D_MODEL), _ROW)] + in_specs,
        out_specs=out_specs,
        out_shape=out_shapes,
        compiler_params=_cparams(("parallel",)),
        name="in_projection",
    )(h, gain, w_bf16, *tables)


def _head_masks():
    lane = lax.broadcasted_iota(jnp.int32, (QB, LANES), 1)
    return lane < HEAD_DIM


SUPER = DILATED_PATTERNS[-1][0]
GROUP = 1
LOOKAHEAD = 1


def _dilated_kernel(q_ref, k_ref, v_ref, g_ref, o_ref, k1, v1, k4, v4, k16, v16, stage, m_sc, acc_sc, out_sc, q_sc):
    span = pl.program_id(2)
    seq = k_ref.shape[0]
    first_head = _head_masks()

    @pl.when(span == 0)
    def _():
        quarter = seq // 4

        def split(j, tiles_per_residue):
            bits = tiles_per_residue.bit_length() - 1
            assert tiles_per_residue == 1 << bits
            return lax.shift_right_logical(j, bits), lax.bitwise_and(j, tiles_per_residue - 1)

        for src_ref, c1, c4, c16 in ((k_ref, k1, k4, k16), (v_ref, v1, v4, v16)):
            def by_four(j, carry):
                dst = pl.ds(pl.multiple_of(j * QB, QB), QB)
                r, tile = split(j, quarter // QB)
                u0 = tile * QB
                regrouped = src_ref[pl.ds(4 * u0 + r, QB, stride=4), :]
                c1[dst, :] = src_ref[dst, :].astype(BF16)
                stage[dst, :] = regrouped
                c4[dst, :] = regrouped.astype(BF16)
                return carry
            lax.fori_loop(0, seq // QB, by_four, 0, unroll=4)

            def by_sixteen(j, carry):
                dst = pl.ds(pl.multiple_of(j * QB, QB), QB)
                r16, tile = split(j, seq // 16 // QB)
                high, low = split(r16, 4)
                src = pl.ds(low * quarter + high + 4 * QB * tile, QB, stride=4)
                c16[dst, :] = stage[src, :].astype(BF16)
                return carry
            lax.fori_loop(0, seq // QB, by_sixteen, 0, unroll=4)

    row = lax.broadcasted_iota(jnp.int32, (QB, 2 * QB), 0)
    col = lax.broadcasted_iota(jnp.int32, (QB, 2 * QB), 1) % QB

    def window_biases(query_pos):
        return jnp.where(col <= query_pos, 0.0, NEG_BIG), jnp.where(col >= query_pos, 0.0, NEG_BIG)

    quarter = QB // 4
    biases_in_order = window_biases(row)
    biases_by_four = window_biases(4 * (row % quarter) + row // quarter)
    head_sel = (jnp.where(first_head, 1.0, 0.0).astype(BF16), jnp.where(first_head, 0.0, 1.0).astype(BF16))

    groups = range(GROUP)

    def gather(ref, chunks, plane=None):
        parts = [ref[c, :] if plane is None else ref[plane, c, :] for c in chunks]
        return parts[0] if len(parts) == 1 else jnp.concatenate(parts, axis=0)

    def scatter(ref, chunks, value, plane=None):
        size = value.shape[0] // len(chunks)
        for n, c in enumerate(chunks):
            if plane is None:
                ref[c, :] = value[n * size:(n + 1) * size]
            else:
                ref[plane, c, :] = value[n * size:(n + 1) * size]

    for j in range(SUPER // QB):
        q_sc[j * QB:(j + 1) * QB, :] = q_ref[pl.ds((j % 4) * (4 * QB) + j // 4, QB, stride=4), :]

    def score_stage(kc, state_rows, cur_start, prev_valid, biases):
        qt = [gather(q_sc, state_rows[g]).astype(BF16) for g in groups]
        starts, masks = [], []
        for g in groups:
            cur = pl.multiple_of(cur_start[g], QB)
            prev = pl.multiple_of(jnp.maximum(cur_start[g] - QB, 0), QB)
            starts.append((cur, prev))
            known = isinstance(prev_valid[g], bool)
            assert not known or prev_valid[g]
            masks.append((biases[0], biases[1] if known else biases[1] + jnp.where(prev_valid[g], 0.0, NEG_BIG)))
        k_bd = [[jnp.concatenate([kblk * head_sel[0], kblk * head_sel[1]], axis=0)
                 for kblk in (kc[pl.ds(s, QB), :] for s in starts[g])] for g in groups]
        scores = [[lax.dot_general(qt[g], k_bd[g][t], (((1,), (1,)), ((), ())),
                                   preferred_element_type=F32) + masks[g][t] for t in range(2)]
                  for g in groups]
        return starts, scores

    def softmax_stage(vc, q_rows, state_rows, starts, scores, fresh, final):
        if not fresh:
            old = [(gather(m_sc, state_rows[g], 0), gather(m_sc, state_rows[g], 1),
                    jnp.concatenate([gather(acc_sc, state_rows[g], 0), gather(acc_sc, state_rows[g], 1)], axis=1))
                   for g in groups]
        top = [jnp.maximum(scores[g][0], scores[g][1]) for g in groups]
        m_new = []
        for g in groups:
            pair = []
            for h in range(2):
                m_h = jnp.max(top[g][:, h * QB:(h + 1) * QB], axis=-1, keepdims=True)
                if not fresh:
                    m_h = jnp.maximum(old[g][h], m_h)
                pair.append(jnp.broadcast_to(m_h, (QB, QB)))
            m_new.append(pair)
        p = [[jnp.exp2(scores[g][t] - jnp.concatenate(m_new[g], axis=1)).astype(BF16) for t in range(2)]
             for g in groups]
        v_ext = [[jnp.concatenate([jnp.concatenate([vblk * head_sel[0], head_sel[0]], axis=1),
                                   jnp.concatenate([vblk * head_sel[1], head_sel[1]], axis=1)], axis=0)
                  for vblk in (vc[pl.ds(s, QB), :] for s in starts[g])] for g in groups]
        pv = [jnp.dot(p[g][0], v_ext[g][0], preferred_element_type=F32)
              + jnp.dot(p[g][1], v_ext[g][1], preferred_element_type=F32) for g in groups]
        if not fresh:
            alpha = [jnp.exp2(jnp.where(first_head, old[g][0] - m_new[g][0], old[g][1] - m_new[g][1]))
                     for g in groups]
            pv = [pv[g] + jnp.concatenate([alpha[g], alpha[g]], axis=1) * old[g][2] for g in groups]
        for g in groups:
            if final:
                scatter(out_sc, q_rows[g], pv[g][:, :LANES] / pv[g][:, LANES:] * gather(g_ref, q_rows[g]))
            else:
                scatter(m_sc, state_rows[g], m_new[g][0], 0)
                scatter(m_sc, state_rows[g], m_new[g][1], 1)
                scatter(acc_sc, state_rows[g], pv[g][:, :LANES], 0)
                scatter(acc_sc, state_rows[g], pv[g][:, LANES:], 1)

    blocks = SUPER // QB
    span_quarter = SUPER // 4

    def pattern16(r):
        return (None, [pl.ds((r % 4) * span_quarter + r // 4, QB, stride=4)],
                r * (seq // 16) + span * QB, span > 0, biases_in_order)

    def pattern4(idx):
        r, ub = idx // 4, idx % 4
        return (None, [pl.ds(r * span_quarter + ub * QB, QB)],
                r * (seq // 4) + (span * 4 + ub) * QB, True if ub > 0 else span > 0, biases_in_order)

    def pattern1(i):
        return ([pl.ds(i * QB + c, quarter, stride=4) for c in range(4)],
                [pl.ds(c * span_quarter + i * quarter, quarter) for c in range(4)],
                (span * blocks + i) * QB, True if i > 0 else span > 0, biases_by_four)

    items = [(pattern, kc, vc, fresh, final, first)
             for pattern, kc, vc, fresh, final in ((pattern16, k16, v16, True, False),
                                                   (pattern4, k4, v4, False, False),
                                                   (pattern1, k1, v1, False, True))
             for first in range(0, blocks, GROUP)]

    def scores_of(item):
        pattern, kc, _, _, _, first = item
        q_rows, state_rows, cur_start, prev_valid, biases = zip(*[pattern(first + g) for g in groups])
        return (q_rows, state_rows) + score_stage(kc, state_rows, cur_start, prev_valid, biases[0])

    ahead = [scores_of(item) for item in items[:LOOKAHEAD]]
    for k, item in enumerate(items):
        q_rows, state_rows, starts, scores = ahead.pop(0)
        if k + LOOKAHEAD < len(items):
            ahead.append(scores_of(items[k + LOOKAHEAD]))
        softmax_stage(item[2], q_rows, state_rows, starts, scores, item[3], item[4])
    o_ref[...] = out_sc[...].astype(BF16)


def _dilated_attention(qa, ka, va, ga, batch, seq):
    n = qa.shape[0]
    pairs = W_DIL // LANES
    spans = seq // SUPER
    qmap = lambda b, p, i: (b * spans + i, p)
    kvmap = lambda b, p, i: (b, p)
    copy = pltpu.VMEM((seq, LANES), BF16)
    return pl.pallas_call(
        _dilated_kernel,
        grid=(batch, pairs, spans),
        in_specs=[pl.BlockSpec((SUPER, LANES), qmap),
                  pl.BlockSpec((seq, LANES), kvmap),
                  pl.BlockSpec((seq, LANES), kvmap),
                  pl.BlockSpec((SUPER, LANES), qmap)],
        out_specs=pl.BlockSpec((SUPER, LANES), qmap),
        out_shape=jax.ShapeDtypeStruct((n, W_DIL), BF16),
        scratch_shapes=([copy] * 6 + [pltpu.VMEM((seq, LANES), F32)] + [pltpu.VMEM((2, SUPER, LANES), F32)] * 2
                        + [pltpu.VMEM((SUPER, LANES), F32)] * 2),
        compiler_params=_cparams(("parallel", "parallel", "arbitrary")),
        name="dilated_attention",
    )(qa, ka, va, ga)


def _suffix_count_matrix(width):
    j = np.arange(width)[:, None]
    s = np.arange(width)[None, :]
    u = (j > s).astype(np.float32)
    z = np.zeros_like(u)
    return np.block([[u, z], [z, u]])


def _log_keep_and_beta(z2):
    log_keep = jnp.log(1.0 / (1.0 + jnp.exp2(-jnp.abs(z2)))) * math.log2(math.e) - jnp.maximum(z2, 0.0)
    return log_keep, z2 + log_keep


SB_GROUP = 8
SB_HALF = QB // 2


def _stick_kernel(q_ref, k_ref, v_ref, g_ref, u_ref, u_half_ref, o_ref):
    step = pl.program_id(2)
    first_head = _head_masks()
    head_sel = (jnp.where(first_head, 1.0, 0.0).astype(BF16), jnp.where(first_head, 0.0, 1.0).astype(BF16))
    suffix_count = {QB: u_ref[...], SB_HALF: u_half_ref[...]}
    row = lax.broadcasted_iota(jnp.int32, (QB, 2 * QB), 0)
    col = lax.broadcasted_iota(jnp.int32, (QB, 2 * QB), 1) % QB
    strictly_causal = col < row

    groups = range(SB_GROUP)

    def key_rows(g, offset, width):
        first = (step * SB_GROUP + g) * QB - offset
        exists = True if isinstance(offset, int) and g * QB >= offset else first >= 0
        return pl.multiple_of(jnp.maximum(first, 0), width), exists

    lane_half = lax.broadcasted_iota(jnp.int32, (SB_HALF, LANES), 1) < HEAD_DIM
    head_sel_half = (jnp.where(lane_half, 1.0, 0.0).astype(BF16), jnp.where(lane_half, 0.0, 1.0).astype(BF16))

    def block_diagonal(ref, g, offset, width):
        sel = head_sel if width == QB else head_sel_half
        blk = ref[pl.ds(key_rows(g, offset, width)[0], width), :]
        return jnp.concatenate([blk * sel[0], blk * sel[1]], axis=0)


    def scores(offset, width):
        return [lax.dot_general(q_ref[g * QB:(g + 1) * QB, :], block_diagonal(k_ref, g, offset, width),
                                (((1,), (1,)), ((), ())), preferred_element_type=F32)
                for g in groups]

    def logs(z, own_block):
        parts = []
        for g in groups:
            log_keep, log_beta = _log_keep_and_beta(z[g])
            if own_block:
                log_keep = jnp.where(strictly_causal, log_keep, 0.0)
            hi = log_keep.astype(BF16)
            parts.append((log_keep, log_beta, hi, (log_keep - hi.astype(F32)).astype(BF16)))
        return parts

    def fold(offset, width, parts, passed, acc, own_block):
        u = suffix_count[width]
        between, new_acc, new_passed, tops = [], [], [], []
        for g in groups:
            passed_g, exists = passed[g], key_rows(g, offset, width)[1]
            if exists is not True:
                passed_g = passed_g + jnp.where(exists, 0.0, NEG_BIG)
            between.append(passed_g + (jnp.dot(parts[g][2], u, preferred_element_type=F32)
                                       + jnp.dot(parts[g][3], u, preferred_element_type=F32)))
        for g in groups:
            w = jnp.exp2(parts[g][1] + between[g])
            if own_block:
                w = jnp.where(strictly_causal, w, 0.0)
            new_acc.append(acc[g] + jnp.dot(w.astype(BF16), block_diagonal(v_ref, g, offset, width),
                                            preferred_element_type=F32))
        for g in groups:
            total = between[g] + parts[g][0]
            total = (total[:, 0:1], total[:, width:width + 1])
            tops.append(jnp.maximum(total[0], total[1]))
            total = [jnp.broadcast_to(t, (QB, LANES)) for t in total]
            new_passed.append(jnp.concatenate(total, axis=1) if width == LANES
                              else jnp.where(first_head, total[0], total[1]))
        live = jnp.max(functools.reduce(jnp.maximum, tops)) > SB_DEAD_LOG2
        return tuple(new_passed), tuple(new_acc), live

    z_own, z_prev = scores(0, QB), scores(QB, QB)
    parts_own, parts_prev = logs(z_own, True), logs(z_prev, False)
    passed, acc, _ = fold(0, QB, parts_own, [jnp.zeros((QB, 2 * QB), F32)] * SB_GROUP,
                          [jnp.zeros((QB, LANES), F32)] * SB_GROUP, True)
    passed, acc, live = fold(QB, QB, parts_prev, passed, acc, False)

    last_block_start = (step * SB_GROUP + SB_GROUP - 1) * QB
    passed = tuple(jnp.where(first_head, p[:, :QB], p[:, QB:]) for p in passed)

    def cond(c):
        halves, live = c[:2]
        return jnp.logical_and(QB + SB_HALF * (halves + 1) <= last_block_start, live)

    def body(c):
        halves, _, passed, acc = c
        offset = QB + SB_HALF * (halves + 1)
        passed, acc, live = fold(offset, SB_HALF, logs(scores(offset, SB_HALF), False), passed, acc, False)
        return halves + 1, live, passed, acc

    acc = lax.while_loop(cond, body, (0, live, passed, acc))[-1]
    for g in groups:
        rows = slice(g * QB, (g + 1) * QB)
        o_ref[rows, :] = (acc[g] * g_ref[rows, :]).astype(BF16)


def _stick_breaking_attention(qc, kc, vc, gc, batch, seq):
    n = qc.shape[0]
    u, u_half = (jnp.asarray(_suffix_count_matrix(w), dtype=BF16) for w in (QB, SB_HALF))
    whole = lambda b, p, i: (0, 0)
    pairs = W_SB // LANES
    rows = SB_GROUP * QB
    steps = seq // rows
    qmap = lambda b, p, i: (b * steps + i, p)
    kvmap = lambda b, p, i: (b, p)
    return pl.pallas_call(
        _stick_kernel,
        grid=(batch, pairs, steps),
        in_specs=[pl.BlockSpec((rows, LANES), qmap),
                  pl.BlockSpec((seq, LANES), kvmap),
                  pl.BlockSpec((seq, LANES), kvmap),
                  pl.BlockSpec((rows, LANES), qmap),
                  pl.BlockSpec(u.shape, whole),
                  pl.BlockSpec(u_half.shape, whole)],
        out_specs=pl.BlockSpec((rows, LANES), qmap),
        out_shape=jax.ShapeDtypeStruct((n, W_SB), BF16),
        compiler_params=_cparams(("parallel", "parallel", "arbitrary")),
        name="stick_breaking_attention",
    )(qc, kc, vc, gc, u, u_half)


SUBLANES = 8
CONV_WIDTH = 4


def _rglru_kernel(x_ref, g_ref, cw_ref, cb_ref, wa_ref, ba_ref, wx_ref, bxg_ref, lam_ref, o_ref,
                  xbuf, a_sc, u_sc, h_sc, hcar):
    tc = x_ref.shape[0]

    @pl.when(pl.program_id(1) == 0)
    def _():
        xbuf[0:SUBLANES, :] = jnp.zeros((SUBLANES, W_LRU), F32)
        hcar[...] = jnp.zeros_like(hcar)

    x = x_ref[...]
    xbuf[SUBLANES:SUBLANES + tc, :] = x
    xc = x * cw_ref[CONV_WIDTH - 1:CONV_WIDTH, :] + cb_ref[...]
    for back in range(1, CONV_WIDTH):
        xc = xc + xbuf[SUBLANES - back:SUBLANES - back + tc, :] * cw_ref[CONV_WIDTH - 1 - back:CONV_WIDTH - back, :]
    xbuf[0:SUBLANES, :] = xbuf[tc:tc + SUBLANES, :]

    xcb = xc.astype(BF16)
    r = jax.nn.sigmoid(jnp.dot(xcb, wa_ref[...], preferred_element_type=F32) + ba_ref[...])
    i = jax.nn.sigmoid(jnp.dot(xcb, wx_ref[...], preferred_element_type=F32) + bxg_ref[...])
    lam = lam_ref[...]
    log_sig_lam = -(jnp.maximum(-lam, 0.0) + jnp.log1p(jnp.exp(-jnp.abs(lam))))
    log_a = LRU_C * r * log_sig_lam
    a = jnp.exp(log_a)
    a_sc[...] = a
    one_minus_a2 = jnp.tanh(-log_a) * (a * a + 1.0)
    root = jnp.where(one_minus_a2 > 0.0, one_minus_a2 * lax.rsqrt(one_minus_a2), 0.0)
    u_sc[...] = root * (i * xc)

    rows = lax.broadcasted_iota(jnp.int32, (SUBLANES, W_LRU), 0)

    def step(j, h_prev):
        start = pl.multiple_of(j * SUBLANES, SUBLANES)
        a = a_sc[pl.ds(start, SUBLANES), :]
        b = u_sc[pl.ds(start, SUBLANES), :]
        for shift in (1, 2, 4):
            keep = rows >= shift
            a_prev = jnp.where(keep, pltpu.roll(a, shift, 0), 1.0)
            b_prev = jnp.where(keep, pltpu.roll(b, shift, 0), 0.0)
            b = a * b_prev + b
            a = a * a_prev
        h = a * h_prev + b
        h_sc[pl.ds(start, SUBLANES), :] = h
        return jnp.broadcast_to(h[SUBLANES - 1:SUBLANES, :], (SUBLANES, W_LRU))

    hcar[...] = lax.fori_loop(0, tc // SUBLANES, step, hcar[...], unroll=8)
    o_ref[...] = (h_sc[...] * g_ref[...]).astype(BF16)


def _rg_lru(bx, bg, cw, cb, wa, ba, wx, bxg, lam, batch, seq, tc):
    n = bx.shape[0]
    nchunks = seq // tc
    row = lambda b, c: (b * nchunks + c, 0)
    const = lambda b, c: (0, 0)
    return pl.pallas_call(
        _rglru_kernel,
        grid=(batch, nchunks),
        in_specs=[pl.BlockSpec((tc, W_LRU), row),
                  pl.BlockSpec((tc, W_LRU), row),
                  pl.BlockSpec((CONV_WIDTH, W_LRU), const),
                  pl.BlockSpec((1, W_LRU), const),
                  pl.BlockSpec((W_LRU, W_LRU), const),
                  pl.BlockSpec((1, W_LRU), const),
                  pl.BlockSpec((W_LRU, W_LRU), const),
                  pl.BlockSpec((1, W_LRU), const),
                  pl.BlockSpec((1, W_LRU), const)],
        out_specs=pl.BlockSpec((tc, W_LRU), row),
        out_shape=jax.ShapeDtypeStruct((n, W_LRU), BF16),
        scratch_shapes=[pltpu.VMEM((tc + SUBLANES, W_LRU), F32),
                        pltpu.VMEM((tc, W_LRU), F32),
                        pltpu.VMEM((tc, W_LRU), F32),
                        pltpu.VMEM((tc, W_LRU), F32),
                        pltpu.VMEM((SUBLANES, W_LRU), F32)],
        compiler_params=_cparams(("arbitrary", "arbitrary")),
        name="rg_lru",
    )(bx, bg, cw, cb, wa, ba, wx, bxg, lam)


def _residual_out(x_ref, ya_ref, yb_ref, yc_ref, w_ref):
    acc = x_ref[...]
    acc = acc + jnp.dot(ya_ref[...], w_ref[0:W_DIL, :], preferred_element_type=F32)
    acc = acc + jnp.dot(yb_ref[...], w_ref[W_DIL:W_DIL + W_LRU, :], preferred_element_type=F32)
    return acc + jnp.dot(yc_ref[...], w_ref[W_DIL + W_LRU:, :], preferred_element_type=F32)


def _final_kernel(x_ref, ya_ref, yb_ref, yc_ref, w_ref, fg_ref, o_ref):
    acc = _residual_out(x_ref, ya_ref, yb_ref, yc_ref, w_ref)
    ms = jnp.mean(acc * acc, axis=-1, keepdims=True)
    o_ref[...] = (acc * lax.rsqrt(ms + EPS)) * fg_ref[...]


def _boundary_kernel(x_ref, ya_ref, yb_ref, yc_ref, w_ref, *refs):
    h_ref = refs[5]
    h = _residual_out(x_ref, ya_ref, yb_ref, yc_ref, w_ref)
    h_ref[...] = h
    _project_in(h, *refs[:5], *refs[6:])


def _outproj_specs(tm, layer):
    return [pl.BlockSpec((tm, D_MODEL), _ROW),
            pl.BlockSpec((tm, W_DIL), _ROW),
            pl.BlockSpec((tm, W_LRU), _ROW),
            pl.BlockSpec((tm, W_SB), _ROW),
            pl.BlockSpec((None, D_MODEL, D_MODEL), lambda i: (layer, 0, 0))]


def _final_projection(h, ya, yb, yc, w_bf16, layer, final_gain, tm):
    n = h.shape[0]
    return pl.pallas_call(
        _final_kernel,
        grid=(n // tm,),
        in_specs=_outproj_specs(tm, layer) + [pl.BlockSpec((1, D_MODEL), _CONST)],
        out_specs=pl.BlockSpec((tm, D_MODEL), _ROW),
        out_shape=jax.ShapeDtypeStruct((n, D_MODEL), F32),
        compiler_params=_cparams(("parallel",)),
        name="out_projection",
    )(h, ya, yb, yc, w_bf16, final_gain)


def _boundary_projection(h, ya, yb, yc, w_out_bf16, gain, w_in_bf16, layer, tables, seq, tm):
    n = h.shape[0]
    in_specs, out_specs, out_shapes = _inproj_specs(n, seq, tm, layer + 1)
    return pl.pallas_call(
        _boundary_kernel,
        grid=(n // tm,),
        in_specs=_outproj_specs(tm, layer) + in_specs,
        out_specs=[pl.BlockSpec((tm, D_MODEL), _ROW)] + out_specs,
        out_shape=[jax.ShapeDtypeStruct((n, D_MODEL), F32)] + out_shapes,
        compiler_params=_cparams(("parallel",)),
        name="out_in_projection",
    )(h, ya, yb, yc, w_out_bf16, gain, w_in_bf16, *tables)


def _rope_tables(seq):
    pos = np.arange(seq, dtype=np.float64)
    inv_freq = ROPE_THETA ** (-np.arange(0, ROPE_DIM, 2, dtype=np.float64) / ROPE_DIM)
    ang = pos[:, None] * inv_freq[None, :]
    cos, sin = np.cos(ang), np.sin(ang)
    half = ROPE_DIM // 2
    lane = np.arange(LANES) % HEAD_DIM
    freq = lane % half
    cos_l, sin_l = cos[:, freq], sin[:, freq]
    cos_t = np.where(lane < ROPE_DIM, cos_l, 1.0)
    s1_t = np.where(lane < half, -sin_l, 0.0)
    s2_t = np.where((lane >= half) & (lane < ROPE_DIM), sin_l, 0.0)
    return tuple(jnp.asarray(t, dtype=F32) for t in (cos_t, s1_t, s2_t))


def _block_diagonal(w):
    nb, d, _ = w.shape
    eye = jnp.eye(nb, dtype=w.dtype)
    return (w[:, :, None, :] * eye[:, None, :, None]).reshape(nb * d, nb * d)


def kernel(x, norm_gain, w_in, conv_w, conv_b, gate_a_w, gate_a_b, gate_x_w, gate_x_b, lru_lambda, w_out, final_gain):
    batch, seq, d = x.shape
    depth = w_in.shape[0]
    n = batch * seq
    tables = _rope_tables(seq)
    w_in_bf16, w_out_bf16 = w_in.astype(BF16), w_out.astype(BF16)
    tm = 512
    h = x.reshape(n, d)
    projected = _in_projection(h, norm_gain[0].reshape(1, d), w_in_bf16, 0, tables, seq, tm)
    for l in range(depth):
        qa, ka, va, ga, bx, bg, qc, kc, vc, gc = projected
        ya = _dilated_attention(qa, ka, va, ga, batch, seq)
        yb = _rg_lru(bx, bg, conv_w[l], conv_b[l].reshape(1, W_LRU),
                     _block_diagonal(gate_a_w[l]).astype(BF16), gate_a_b[l].reshape(1, W_LRU),
                     _block_diagonal(gate_x_w[l]).astype(BF16), gate_x_b[l].reshape(1, W_LRU),
                     lru_lambda[l].reshape(1, W_LRU), batch, seq, tc=2048)
        yc = _stick_breaking_attention(qc, kc, vc, gc, batch, seq)
        if l + 1 < depth:
            h, *projected = _boundary_projection(h, ya, yb, yc, w_out_bf16, norm_gain[l + 1].reshape(1, d),
                                                 w_in_bf16, l, tables, seq, tm)
        else:
            h = _final_projection(h, ya, yb, yc, w_out_bf16, l, final_gain.reshape(1, d), 2 * tm)
    return h.reshape(batch, seq, d)
```

```python
import functools
import math

import jax
import jax.numpy as jnp
import numpy as np
from jax import lax
from jax.experimental import pallas as pl
from jax.experimental.pallas import tpu as pltpu

F32 = jnp.float32
BF16 = jnp.bfloat16

D_MODEL = 1024
HEAD_DIM = 64
W_DIL = 512
W_LRU = 256
W_SB = 256
D_IN = 4 * W_DIL + 2 * W_LRU + 4 * W_SB
ROPE_DIM = 16
ROPE_THETA = 500000.0
DILATED_PATTERNS = ((128, 1), (512, 4), (2048, 16))
LRU_C = 8.0
EPS = 1e-6

LANES = 128
QB = 128
NEG_BIG = -1e30
SB_DEAD_LOG2 = -150.5

VMEM_LIMIT = 56 * 1024 * 1024


def _cparams(sem):
    return pltpu.CompilerParams(dimension_semantics=sem, vmem_limit_bytes=VMEM_LIMIT)


def _inproj_kernel(x_ref, *refs):
    _project_in(x_ref[...], *refs)


def _project_in(x, gain_ref, w_ref, cos_ref, s1_ref, s2_ref,
                qa_ref, ka_ref, va_ref, ga_ref, bx_ref, bg_ref,
                qc_ref, kc_ref, vc_ref, gc_ref):
    ms = jnp.mean(x * x, axis=-1, keepdims=True)
    xn = ((x * lax.rsqrt(ms + EPS)) * gain_ref[...]).astype(BF16)

    def proj(lo, width):
        return jnp.dot(xn, w_ref[:, lo:lo + width], preferred_element_type=F32)

    cos = cos_ref[...]
    s1 = s1_ref[...]
    s2 = s2_ref[...]

    def rope(t):
        parts = []
        for c in range(t.shape[1] // LANES):
            xc = t[:, c * LANES:(c + 1) * LANES]
            parts.append(xc * cos + pltpu.roll(xc, LANES - ROPE_DIM // 2, 1) * s1
                         + pltpu.roll(xc, ROPE_DIM // 2, 1) * s2)
        return jnp.concatenate(parts, axis=1)

    def silu(t):
        return t * jax.nn.sigmoid(t)

    scale = 1.0 / math.sqrt(HEAD_DIM)
    o = 0
    qa_ref[...] = rope(proj(o, W_DIL)) * (scale * math.log2(math.e)); o += W_DIL
    ka_ref[...] = rope(proj(o, W_DIL)); o += W_DIL
    va_ref[...] = proj(o, W_DIL); o += W_DIL
    ga_ref[...] = silu(proj(o, W_DIL)); o += W_DIL
    bx_ref[...] = proj(o, W_LRU); o += W_LRU
    bg_ref[...] = silu(proj(o, W_LRU)); o += W_LRU
    qc_ref[...] = (proj(o, W_SB) * (scale * math.log2(math.e))).astype(BF16); o += W_SB
    kc_ref[...] = proj(o, W_SB).astype(BF16); o += W_SB
    vc_ref[...] = proj(o, W_SB).astype(BF16); o += W_SB
    gc_ref[...] = silu(proj(o, W_SB))


_ROW = lambda i: (i, 0)
_CONST = lambda i: (0, 0)


def _inproj_specs(n, seq, tm, layer):
    blocks_per_seq = seq // tm
    tab = lambda i: (i % blocks_per_seq, 0)
    widths = [(W_DIL, F32), (W_DIL, F32), (W_DIL, F32), (W_DIL, F32), (W_LRU, F32),
              (W_LRU, F32), (W_SB, BF16), (W_SB, BF16), (W_SB, BF16), (W_SB, F32)]
    in_specs = [pl.BlockSpec((1, D_MODEL), _CONST),
                pl.BlockSpec((None, D_MODEL, D_IN), lambda i: (layer, 0, 0)),
                pl.BlockSpec((tm, LANES), tab),
                pl.BlockSpec((tm, LANES), tab),
                pl.BlockSpec((tm, LANES), tab)]
    return (in_specs, [pl.BlockSpec((tm, w), _ROW) for w, _ in widths],
            [jax.ShapeDtypeStruct((n, w), dt) for w, dt in widths])


def _in_projection(h, gain, w_bf16, layer, tables, seq, tm):
    n = h.shape[0]
    in_specs, out_specs, out_shapes = _inproj_specs(n, seq, tm, layer)
    return pl.pallas_call(
        _inproj_kernel,
        grid=(n // tm,),
        in_specs=[pl.BlockSpec((tm, D_MODEL), _ROW)] + in_specs,
        out_specs=out_specs,
        out_shape=out_shapes,
        compiler_params=_cparams(("parallel",)),
        name="in_projection",
    )(h, gain, w_bf16, *tables)


def _head_masks():
    lane = lax.broadcasted_iota(jnp.int32, (QB, LANES), 1)
    return lane < HEAD_DIM


assert DILATED_PATTERNS == tuple((QB * d, d) for d in (1, 4, 16))
SUPER = DILATED_PATTERNS[-1][0]
GROUP = 1
LOOKAHEAD = 1


def _dilated_kernel(q_ref, k_ref, v_ref, g_ref, o_ref, k1, v1, k4, v4, k16, v16, stage, m_sc, acc_sc, out_sc, q_sc):
    span = pl.program_id(2)
    seq = k_ref.shape[0]
    first_head = _head_masks()

    @pl.when(span == 0)
    def _():
        quarter = seq // 4

        def split(j, tiles_per_residue):
            bits = tiles_per_residue.bit_length() - 1
            assert tiles_per_residue == 1 << bits
            return lax.shift_right_logical(j, bits), lax.bitwise_and(j, tiles_per_residue - 1)

        for src_ref, c1, c4, c16 in ((k_ref, k1, k4, k16), (v_ref, v1, v4, v16)):
            def by_four(j, carry):
                dst = pl.ds(pl.multiple_of(j * QB, QB), QB)
                r, tile = split(j, quarter // QB)
                u0 = tile * QB
                regrouped = src_ref[pl.ds(4 * u0 + r, QB, stride=4), :]
                c1[dst, :] = src_ref[dst, :].astype(BF16)
                stage[dst, :] = regrouped
                c4[dst, :] = regrouped.astype(BF16)
                return carry
            lax.fori_loop(0, seq // QB, by_four, 0, unroll=4)

            def by_sixteen(j, carry):
                dst = pl.ds(pl.multiple_of(j * QB, QB), QB)
                r16, tile = split(j, seq // 16 // QB)
                high, low = split(r16, 4)
                src = pl.ds(low * quarter + high + 4 * QB * tile, QB, stride=4)
                c16[dst, :] = stage[src, :].astype(BF16)
                return carry
            lax.fori_loop(0, seq // QB, by_sixteen, 0, unroll=4)

    row = lax.broadcasted_iota(jnp.int32, (QB, 2 * QB), 0)
    col = lax.broadcasted_iota(jnp.int32, (QB, 2 * QB), 1) % QB

    def window_biases(query_pos):
        return jnp.where(col <= query_pos, 0.0, NEG_BIG), jnp.where(col >= query_pos, 0.0, NEG_BIG)

    quarter = QB // 4
    biases_in_order = window_biases(row)
    biases_by_four = window_biases(4 * (row % quarter) + row // quarter)
    head_sel = (jnp.where(first_head, 1.0, 0.0).astype(BF16), jnp.where(first_head, 0.0, 1.0).astype(BF16))

    groups = range(GROUP)

    def gather(ref, chunks, plane=None):
        parts = [ref[c, :] if plane is None else ref[plane, c, :] for c in chunks]
        return parts[0] if len(parts) == 1 else jnp.concatenate(parts, axis=0)

    def scatter(ref, chunks, value, plane=None):
        size = value.shape[0] // len(chunks)
        for n, c in enumerate(chunks):
            if plane is None:
                ref[c, :] = value[n * size:(n + 1) * size]
            else:
                ref[plane, c, :] = value[n * size:(n + 1) * size]

    for j in range(SUPER // QB):
        q_sc[j * QB:(j + 1) * QB, :] = q_ref[pl.ds((j % 4) * (4 * QB) + j // 4, QB, stride=4), :]

    def score_stage(kc, state_rows, cur_start, prev_valid, biases):
        qt = [gather(q_sc, state_rows[g]).astype(BF16) for g in groups]
        starts, masks = [], []
        for g in groups:
            cur = pl.multiple_of(cur_start[g], QB)
            prev = pl.multiple_of(jnp.maximum(cur_start[g] - QB, 0), QB)
            starts.append((cur, prev))
            known = isinstance(prev_valid[g], bool)
            assert not known or prev_valid[g]
            masks.append((biases[0], biases[1] if known else biases[1] + jnp.where(prev_valid[g], 0.0, NEG_BIG)))
        k_bd = [[jnp.concatenate([kblk * head_sel[0], kblk * head_sel[1]], axis=0)
                 for kblk in (kc[pl.ds(s, QB), :] for s in starts[g])] for g in groups]
        scores = [[lax.dot_general(qt[g], k_bd[g][t], (((1,), (1,)), ((), ())),
                                   preferred_element_type=F32) + masks[g][t] for t in range(2)]
                  for g in groups]
        return starts, scores

    def softmax_stage(vc, q_rows, state_rows, starts, scores, fresh, final):
        if not fresh:
            old = [(gather(m_sc, state_rows[g], 0), gather(m_sc, state_rows[g], 1),
                    jnp.concatenate([gather(acc_sc, state_rows[g], 0), gather(acc_sc, state_rows[g], 1)], axis=1))
                   for g in groups]
        top = [jnp.maximum(scores[g][0], scores[g][1]) for g in groups]
        m_new = []
        for g in groups:
            pair = []
            for h in range(2):
                m_h = jnp.max(top[g][:, h * QB:(h + 1) * QB], axis=-1, keepdims=True)
                if not fresh:
                    m_h = jnp.maximum(old[g][h], m_h)
                pair.append(jnp.broadcast_to(m_h, (QB, QB)))
            m_new.append(pair)
        p = [[jnp.exp2(scores[g][t] - jnp.concatenate(m_new[g], axis=1)).astype(BF16) for t in range(2)]
             for g in groups]
        v_ext = [[jnp.concatenate([jnp.concatenate([vblk * head_sel[0], head_sel[0]], axis=1),
                                   jnp.concatenate([vblk * head_sel[1], head_sel[1]], axis=1)], axis=0)
                  for vblk in (vc[pl.ds(s, QB), :] for s in starts[g])] for g in groups]
        pv = [jnp.dot(p[g][0], v_ext[g][0], preferred_element_type=F32)
              + jnp.dot(p[g][1], v_ext[g][1], preferred_element_type=F32) for g in groups]
        if not fresh:
            alpha = [jnp.exp2(jnp.where(first_head, old[g][0] - m_new[g][0], old[g][1] - m_new[g][1]))
                     for g in groups]
            pv = [pv[g] + jnp.concatenate([alpha[g], alpha[g]], axis=1) * old[g][2] for g in groups]
        for g in groups:
            if final:
                scatter(out_sc, q_rows[g], pv[g][:, :LANES] / pv[g][:, LANES:] * gather(g_ref, q_rows[g]))
            else:
                scatter(m_sc, state_rows[g], m_new[g][0], 0)
                scatter(m_sc, state_rows[g], m_new[g][1], 1)
                scatter(acc_sc, state_rows[g], pv[g][:, :LANES], 0)
                scatter(acc_sc, state_rows[g], pv[g][:, LANES:], 1)

    blocks = SUPER // QB
    span_quarter = SUPER // 4

    def pattern16(r):
        return (None, [pl.ds((r % 4) * span_quarter + r // 4, QB, stride=4)],
                r * (seq // 16) + span * QB, span > 0, biases_in_order)

    def pattern4(idx):
        r, ub = idx // 4, idx % 4
        return (None, [pl.ds(r * span_quarter + ub * QB, QB)],
                r * (seq // 4) + (span * 4 + ub) * QB, True if ub > 0 else span > 0, biases_in_order)

    def pattern1(i):
        return ([pl.ds(i * QB + c, quarter, stride=4) for c in range(4)],
                [pl.ds(c * span_quarter + i * quarter, quarter) for c in range(4)],
                (span * blocks + i) * QB, True if i > 0 else span > 0, biases_by_four)

    items = [(pattern, kc, vc, fresh, final, first)
             for pattern, kc, vc, fresh, final in ((pattern16, k16, v16, True, False),
                                                   (pattern4, k4, v4, False, False),
                                                   (pattern1, k1, v1, False, True))
             for first in range(0, blocks, GROUP)]

    def scores_of(item):
        pattern, kc, _, _, _, first = item
        q_rows, state_rows, cur_start, prev_valid, biases = zip(*[pattern(first + g) for g in groups])
        return (q_rows, state_rows) + score_stage(kc, state_rows, cur_start, prev_valid, biases[0])

    ahead = [scores_of(item) for item in items[:LOOKAHEAD]]
    for k, item in enumerate(items):
        q_rows, state_rows, starts, scores = ahead.pop(0)
        if k + LOOKAHEAD < len(items):
            ahead.append(scores_of(items[k + LOOKAHEAD]))
        softmax_stage(item[2], q_rows, state_rows, starts, scores, item[3], item[4])
    o_ref[...] = out_sc[...].astype(BF16)


def _dilated_attention(qa, ka, va, ga, batch, seq):
    n = qa.shape[0]
    pairs = W_DIL // LANES
    spans = seq // SUPER
    qmap = lambda b, p, i: (b * spans + i, p)
    kvmap = lambda b, p, i: (b, p)
    copy = pltpu.VMEM((seq, LANES), BF16)
    return pl.pallas_call(
        _dilated_kernel,
        grid=(batch, pairs, spans),
        in_specs=[pl.BlockSpec((SUPER, LANES), qmap),
                  pl.BlockSpec((seq, LANES), kvmap),
                  pl.BlockSpec((seq, LANES), kvmap),
                  pl.BlockSpec((SUPER, LANES), qmap)],
        out_specs=pl.BlockSpec((SUPER, LANES), qmap),
        out_shape=jax.ShapeDtypeStruct((n, W_DIL), BF16),
        scratch_shapes=([copy] * 6 + [pltpu.VMEM((seq, LANES), F32)] + [pltpu.VMEM((2, SUPER, LANES), F32)] * 2
                        + [pltpu.VMEM((SUPER, LANES), F32)] * 2),
        compiler_params=_cparams(("parallel", "parallel", "arbitrary")),
        name="dilated_attention",
    )(qa, ka, va, ga)


def _suffix_count_matrix(width):
    j = np.arange(width)[:, None]
    s = np.arange(width)[None, :]
    u = (j > s).astype(np.float32)
    z = np.zeros_like(u)
    return np.block([[u, z], [z, u]])


def _log_keep_and_beta(z2):
    log_keep = jnp.log(1.0 / (1.0 + jnp.exp2(-jnp.abs(z2)))) * math.log2(math.e) - jnp.maximum(z2, 0.0)
    return log_keep, z2 + log_keep


SB_GROUP = 8
SB_HALF = QB // 2


def _stick_kernel(q_ref, k_ref, v_ref, g_ref, u_ref, u_half_ref, o_ref):
    step = pl.program_id(2)
    first_head = _head_masks()
    head_sel = (jnp.where(first_head, 1.0, 0.0).astype(BF16), jnp.where(first_head, 0.0, 1.0).astype(BF16))
    suffix_count = {QB: u_ref[...], SB_HALF: u_half_ref[...]}
    row = lax.broadcasted_iota(jnp.int32, (QB, 2 * QB), 0)
    col = lax.broadcasted_iota(jnp.int32, (QB, 2 * QB), 1) % QB
    strictly_causal = col < row

    groups = range(SB_GROUP)

    def key_rows(g, offset, width):
        first = (step * SB_GROUP + g) * QB - offset
        exists = True if isinstance(offset, int) and g * QB >= offset else first >= 0
        return pl.multiple_of(jnp.maximum(first, 0), width), exists

    lane_half = lax.broadcasted_iota(jnp.int32, (SB_HALF, LANES), 1) < HEAD_DIM
    head_sel_half = (jnp.where(lane_half, 1.0, 0.0).astype(BF16), jnp.where(lane_half, 0.0, 1.0).astype(BF16))

    def block_diagonal(ref, g, offset, width):
        sel = head_sel if width == QB else head_sel_half
        blk = ref[pl.ds(key_rows(g, offset, width)[0], width), :]
        return jnp.concatenate([blk * sel[0], blk * sel[1]], axis=0)


    def scores(offset, width):
        return [lax.dot_general(q_ref[g * QB:(g + 1) * QB, :], block_diagonal(k_ref, g, offset, width),
                                (((1,), (1,)), ((), ())), preferred_element_type=F32)
                for g in groups]

    def logs(z, own_block):
        parts = []
        for g in groups:
            log_keep, log_beta = _log_keep_and_beta(z[g])
            if own_block:
                log_keep = jnp.where(strictly_causal, log_keep, 0.0)
            hi = log_keep.astype(BF16)
            parts.append((log_keep, log_beta, hi, (log_keep - hi.astype(F32)).astype(BF16)))
        return parts

    def fold(offset, width, parts, passed, acc, own_block):
        u = suffix_count[width]
        between, new_acc, new_passed, tops = [], [], [], []
        for g in groups:
            passed_g, exists = passed[g], key_rows(g, offset, width)[1]
            if exists is not True:
                passed_g = passed_g + jnp.where(exists, 0.0, NEG_BIG)
            between.append(passed_g + (jnp.dot(parts[g][2], u, preferred_element_type=F32)
                                       + jnp.dot(parts[g][3], u, preferred_element_type=F32)))
        for g in groups:
            w = jnp.exp2(parts[g][1] + between[g])
            if own_block:
                w = jnp.where(strictly_causal, w, 0.0)
            new_acc.append(acc[g] + jnp.dot(w.astype(BF16), block_diagonal(v_ref, g, offset, width),
                                            preferred_element_type=F32))
        for g in groups:
            total = between[g] + parts[g][0]
            total = (total[:, 0:1], total[:, width:width + 1])
            tops.append(jnp.maximum(total[0], total[1]))
            total = [jnp.broadcast_to(t, (QB, LANES)) for t in total]
            new_passed.append(jnp.concatenate(total, axis=1) if width == LANES
                              else jnp.where(first_head, total[0], total[1]))
        live = jnp.max(functools.reduce(jnp.maximum, tops)) > SB_DEAD_LOG2
        return tuple(new_passed), tuple(new_acc), live

    z_own, z_prev = scores(0, QB), scores(QB, QB)
    parts_own, parts_prev = logs(z_own, True), logs(z_prev, False)
    passed, acc, _ = fold(0, QB, parts_own, [jnp.zeros((QB, 2 * QB), F32)] * SB_GROUP,
                          [jnp.zeros((QB, LANES), F32)] * SB_GROUP, True)
    passed, acc, live = fold(QB, QB, parts_prev, passed, acc, False)

    last_block_start = (step * SB_GROUP + SB_GROUP - 1) * QB
    passed = tuple(jnp.where(first_head, p[:, :QB], p[:, QB:]) for p in passed)

    def cond(c):
        halves, live = c[:2]
        return jnp.logical_and(QB + SB_HALF * (halves + 1) <= last_block_start, live)

    def body(c):
        halves, _, passed, acc = c
        offset = QB + SB_HALF * (halves + 1)
        passed, acc, live = fold(offset, SB_HALF, logs(scores(offset, SB_HALF), False), passed, acc, False)
        return halves + 1, live, passed, acc

    acc = lax.while_loop(cond, body, (0, live, passed, acc))[-1]
    for g in groups:
        rows = slice(g * QB, (g + 1) * QB)
        o_ref[rows, :] = (acc[g] * g_ref[rows, :]).astype(BF16)


def _stick_breaking_attention(qc, kc, vc, gc, batch, seq):
    n = qc.shape[0]
    u, u_half = (jnp.asarray(_suffix_count_matrix(w), dtype=BF16) for w in (QB, SB_HALF))
    whole = lambda b, p, i: (0, 0)
    pairs = W_SB // LANES
    rows = SB_GROUP * QB
    steps = seq // rows
    qmap = lambda b, p, i: (b * steps + i, p)
    kvmap = lambda b, p, i: (b, p)
    return pl.pallas_call(
        _stick_kernel,
        grid=(batch, pairs, steps),
        in_specs=[pl.BlockSpec((rows, LANES), qmap),
                  pl.BlockSpec((seq, LANES), kvmap),
                  pl.BlockSpec((seq, LANES), kvmap),
                  pl.BlockSpec((rows, LANES), qmap),
                  pl.BlockSpec(u.shape, whole),
                  pl.BlockSpec(u_half.shape, whole)],
        out_specs=pl.BlockSpec((rows, LANES), qmap),
        out_shape=jax.ShapeDtypeStruct((n, W_SB), BF16),
        compiler_params=_cparams(("parallel", "parallel", "arbitrary")),
        name="stick_breaking_attention",
    )(qc, kc, vc, gc, u, u_half)


SUBLANES = 8
CONV_WIDTH = 4


def _rglru_kernel(x_ref, g_ref, cw_ref, cb_ref, wa_ref, ba_ref, wx_ref, bxg_ref, lam_ref, o_ref,
                  xbuf, a_sc, u_sc, h_sc, hcar):
    tc = x_ref.shape[0]

    @pl.when(pl.program_id(1) == 0)
    def _():
        xbuf[0:SUBLANES, :] = jnp.zeros((SUBLANES, W_LRU), F32)
        hcar[...] = jnp.zeros_like(hcar)

    x = x_ref[...]
    xbuf[SUBLANES:SUBLANES + tc, :] = x
    xc = x * cw_ref[CONV_WIDTH - 1:CONV_WIDTH, :] + cb_ref[...]
    for back in range(1, CONV_WIDTH):
        xc = xc + xbuf[SUBLANES - back:SUBLANES - back + tc, :] * cw_ref[CONV_WIDTH - 1 - back:CONV_WIDTH - back, :]
    xbuf[0:SUBLANES, :] = xbuf[tc:tc + SUBLANES, :]

    xcb = xc.astype(BF16)
    r = jax.nn.sigmoid(jnp.dot(xcb, wa_ref[...], preferred_element_type=F32) + ba_ref[...])
    i = jax.nn.sigmoid(jnp.dot(xcb, wx_ref[...], preferred_element_type=F32) + bxg_ref[...])
    lam = lam_ref[...]
    log_sig_lam = -(jnp.maximum(-lam, 0.0) + jnp.log1p(jnp.exp(-jnp.abs(lam))))
    log_a = LRU_C * r * log_sig_lam
    a = jnp.exp(log_a)
    a_sc[...] = a
    one_minus_a2 = jnp.tanh(-log_a) * (a * a + 1.0)
    root = jnp.where(one_minus_a2 > 0.0, one_minus_a2 * lax.rsqrt(one_minus_a2), 0.0)
    u_sc[...] = root * (i * xc)

    rows = lax.broadcasted_iota(jnp.int32, (SUBLANES, W_LRU), 0)

    def step(j, h_prev):
        start = pl.multiple_of(j * SUBLANES, SUBLANES)
        a = a_sc[pl.ds(start, SUBLANES), :]
        b = u_sc[pl.ds(start, SUBLANES), :]
        for shift in (1, 2, 4):
            keep = rows >= shift
            a_prev = jnp.where(keep, pltpu.roll(a, shift, 0), 1.0)
            b_prev = jnp.where(keep, pltpu.roll(b, shift, 0), 0.0)
            b = a * b_prev + b
            a = a * a_prev
        h = a * h_prev + b
        h_sc[pl.ds(start, SUBLANES), :] = h
        return jnp.broadcast_to(h[SUBLANES - 1:SUBLANES, :], (SUBLANES, W_LRU))

    hcar[...] = lax.fori_loop(0, tc // SUBLANES, step, hcar[...], unroll=16)
    o_ref[...] = (h_sc[...] * g_ref[...]).astype(BF16)


def _rg_lru(bx, bg, cw, cb, wa, ba, wx, bxg, lam, batch, seq, tc):
    n = bx.shape[0]
    nchunks = seq // tc
    row = lambda b, c: (b * nchunks + c, 0)
    const = lambda b, c: (0, 0)
    return pl.pallas_call(
        _rglru_kernel,
        grid=(batch, nchunks),
        in_specs=[pl.BlockSpec((tc, W_LRU), row),
                  pl.BlockSpec((tc, W_LRU), row),
                  pl.BlockSpec((CONV_WIDTH, W_LRU), const),
                  pl.BlockSpec((1, W_LRU), const),
                  pl.BlockSpec((W_LRU, W_LRU), const),
                  pl.BlockSpec((1, W_LRU), const),
                  pl.BlockSpec((W_LRU, W_LRU), const),
                  pl.BlockSpec((1, W_LRU), const),
                  pl.BlockSpec((1, W_LRU), const)],
        out_specs=pl.BlockSpec((tc, W_LRU), row),
        out_shape=jax.ShapeDtypeStruct((n, W_LRU), BF16),
        scratch_shapes=[pltpu.VMEM((tc + SUBLANES, W_LRU), F32),
                        pltpu.VMEM((tc, W_LRU), F32),
                        pltpu.VMEM((tc, W_LRU), F32),
                        pltpu.VMEM((tc, W_LRU), F32),
                        pltpu.VMEM((SUBLANES, W_LRU), F32)],
        compiler_params=_cparams(("arbitrary", "arbitrary")),
        name="rg_lru",
    )(bx, bg, cw, cb, wa, ba, wx, bxg, lam)


def _residual_out(x_ref, ya_ref, yb_ref, yc_ref, w_ref):
    acc = x_ref[...]
    acc = acc + jnp.dot(ya_ref[...], w_ref[0:W_DIL, :], preferred_element_type=F32)
    acc = acc + jnp.dot(yb_ref[...], w_ref[W_DIL:W_DIL + W_LRU, :], preferred_element_type=F32)
    return acc + jnp.dot(yc_ref[...], w_ref[W_DIL + W_LRU:, :], preferred_element_type=F32)


def _final_kernel(x_ref, ya_ref, yb_ref, yc_ref, w_ref, fg_ref, o_ref):
    acc = _residual_out(x_ref, ya_ref, yb_ref, yc_ref, w_ref)
    ms = jnp.mean(acc * acc, axis=-1, keepdims=True)
    o_ref[...] = (acc * lax.rsqrt(ms + EPS)) * fg_ref[...]


def _boundary_kernel(x_ref, ya_ref, yb_ref, yc_ref, w_ref, *refs):
    h_ref = refs[5]
    h = _residual_out(x_ref, ya_ref, yb_ref, yc_ref, w_ref)
    h_ref[...] = h
    _project_in(h, *refs[:5], *refs[6:])


def _outproj_specs(tm, layer):
    return [pl.BlockSpec((tm, D_MODEL), _ROW),
            pl.BlockSpec((tm, W_DIL), _ROW),
            pl.BlockSpec((tm, W_LRU), _ROW),
            pl.BlockSpec((tm, W_SB), _ROW),
            pl.BlockSpec((None, D_MODEL, D_MODEL), lambda i: (layer, 0, 0))]


def _final_projection(h, ya, yb, yc, w_bf16, layer, final_gain, tm):
    n = h.shape[0]
    return pl.pallas_call(
        _final_kernel,
        grid=(n // tm,),
        in_specs=_outproj_specs(tm, layer) + [pl.BlockSpec((1, D_MODEL), _CONST)],
        out_specs=pl.BlockSpec((tm, D_MODEL), _ROW),
        out_shape=jax.ShapeDtypeStruct((n, D_MODEL), F32),
        compiler_params=_cparams(("parallel",)),
        name="out_projection",
    )(h, ya, yb, yc, w_bf16, final_gain)


def _boundary_projection(h, ya, yb, yc, w_out_bf16, gain, w_in_bf16, layer, tables, seq, tm):
    n = h.shape[0]
    in_specs, out_specs, out_shapes = _inproj_specs(n, seq, tm, layer + 1)
    return pl.pallas_call(
        _boundary_kernel,
        grid=(n // tm,),
        in_specs=_outproj_specs(tm, layer) + in_specs,
        out_specs=[pl.BlockSpec((tm, D_MODEL), _ROW)] + out_specs,
        out_shape=[jax.ShapeDtypeStruct((n, D_MODEL), F32)] + out_shapes,
        compiler_params=_cparams(("parallel",)),
        name="out_in_projection",
    )(h, ya, yb, yc, w_out_bf16, gain, w_in_bf16, *tables)


def _rope_tables(seq):
    pos = np.arange(seq, dtype=np.float64)
    inv_freq = ROPE_THETA ** (-np.arange(0, ROPE_DIM, 2, dtype=np.float64) / ROPE_DIM)
    ang = pos[:, None] * inv_freq[None, :]
    cos, sin = np.cos(ang), np.sin(ang)
    half = ROPE_DIM // 2
    lane = np.arange(LANES) % HEAD_DIM
    freq = lane % half
    cos_l, sin_l = cos[:, freq], sin[:, freq]
    cos_t = np.where(lane < ROPE_DIM, cos_l, 1.0)
    s1_t = np.where(lane < half, -sin_l, 0.0)
    s2_t = np.where((lane >= half) & (lane < ROPE_DIM), sin_l, 0.0)
    return tuple(jnp.asarray(t, dtype=F32) for t in (cos_t, s1_t, s2_t))


def _block_diagonal(w):
    nb, d, _ = w.shape
    eye = jnp.eye(nb, dtype=w.dtype)
    return (w[:, :, None, :] * eye[:, None, :, None]).reshape(nb * d, nb * d)


def kernel(x, norm_gain, w_in, conv_w, conv_b, gate_a_w, gate_a_b, gate_x_w, gate_x_b, lru_lambda, w_out, final_gain):
    batch, seq, d = x.shape
    depth = w_in.shape[0]
    n = batch * seq
    tables = _rope_tables(seq)
    w_in_bf16, w_out_bf16 = w_in.astype(BF16), w_out.astype(BF16)
    tm = 512
    h = x.reshape(n, d)
    projected = _in_projection(h, norm_gain[0].reshape(1, d), w_in_bf16, 0, tables, seq, tm)
    for l in range(depth):
        qa, ka, va, ga, bx, bg, qc, kc, vc, gc = projected
        ya = _dilated_attention(qa, ka, va, ga, batch, seq)
        yb = _rg_lru(bx, bg, conv_w[l], conv_b[l].reshape(1, W_LRU),
                     _block_diagonal(gate_a_w[l]).astype(BF16), gate_a_b[l].reshape(1, W_LRU),
                     _block_diagonal(gate_x_w[l]).astype(BF16), gate_x_b[l].reshape(1, W_LRU),
                     lru_lambda[l].reshape(1, W_LRU), batch, seq, tc=2048)
        yc = _stick_breaking_attention(qc, kc, vc, gc, batch, seq)
        if l + 1 < depth:
            h, *projected = _boundary_projection(h, ya, yb, yc, w_out_bf16, norm_gain[l + 1].reshape(1, d),
                                                 w_in_bf16, l, tables, seq, tm)
        else:
            h = _final_projection(h, ya, yb, yc, w_out_bf16, l, final_gain.reshape(1, d), 4 * tm)
    return h.reshape(batch, seq, d)
```

```python
import functools
import math

import jax
import jax.numpy as jnp
import numpy as np
from jax import lax
from jax.experimental import pallas as pl
from jax.experimental.pallas import tpu as pltpu

F32 = jnp.float32
BF16 = jnp.bfloat16

D_MODEL = 1024
HEAD_DIM = 64
W_DIL = 512
W_LRU = 256
W_SB = 256
D_IN = 4 * W_DIL + 2 * W_LRU + 4 * W_SB
ROPE_DIM = 16
ROPE_THETA = 500000.0
DILATED_PATTERNS = ((128, 1), (512, 4), (2048, 16))
LRU_C = 8.0
EPS = 1e-6

LANES = 128
QB = 128
NEG_BIG = -1e30
SB_DEAD_LOG2 = -150.5

VMEM_LIMIT = 56 * 1024 * 1024


def _cparams(sem):
    return pltpu.CompilerParams(dimension_semantics=sem, vmem_limit_bytes=VMEM_LIMIT)


def _inproj_kernel(x_ref, *refs):
    _project_in(x_ref[...], *refs)


def _project_in(x, gain_ref, w_ref, cos_ref, s1_ref, s2_ref,
                qa_ref, ka_ref, va_ref, ga_ref, bx_ref, bg_ref,
                qc_ref, kc_ref, vc_ref, gc_ref):
    ms = jnp.mean(x * x, axis=-1, keepdims=True)
    xn = ((x * lax.rsqrt(ms + EPS)) * gain_ref[...]).astype(BF16)

    def proj(lo, width):
        return jnp.dot(xn, w_ref[:, lo:lo + width], preferred_element_type=F32)

    cos = cos_ref[...]
    s1 = s1_ref[...]
    s2 = s2_ref[...]

    def rope(t):
        parts = []
        for c in range(t.shape[1] // LANES):
            xc = t[:, c * LANES:(c + 1) * LANES]
            parts.append(xc * cos + pltpu.roll(xc, LANES - ROPE_DIM // 2, 1) * s1
                         + pltpu.roll(xc, ROPE_DIM // 2, 1) * s2)
        return jnp.concatenate(parts, axis=1)

    def silu(t):
        return t * jax.nn.sigmoid(t)

    scale = 1.0 / math.sqrt(HEAD_DIM)
    o = 0
    qa_ref[...] = rope(proj(o, W_DIL)) * (scale * math.log2(math.e)); o += W_DIL
    ka_ref[...] = rope(proj(o, W_DIL)); o += W_DIL
    va_ref[...] = proj(o, W_DIL); o += W_DIL
    ga_ref[...] = silu(proj(o, W_DIL)); o += W_DIL
    bx_ref[...] = proj(o, W_LRU); o += W_LRU
    bg_ref[...] = silu(proj(o, W_LRU)); o += W_LRU
    qc_ref[...] = (proj(o, W_SB) * (scale * math.log2(math.e))).astype(BF16); o += W_SB
    kc_ref[...] = proj(o, W_SB).astype(BF16); o += W_SB
    vc_ref[...] = proj(o, W_SB).astype(BF16); o += W_SB
    gc_ref[...] = silu(proj(o, W_SB))


_ROW = lambda i: (i, 0)
_CONST = lambda i: (0, 0)


def _inproj_specs(n, seq, tm, layer):
    blocks_per_seq = seq // tm
    tab = lambda i: (i % blocks_per_seq, 0)
    widths = [(W_DIL, F32), (W_DIL, F32), (W_DIL, F32), (W_DIL, F32), (W_LRU, F32),
              (W_LRU, F32), (W_SB, BF16), (W_SB, BF16), (W_SB, BF16), (W_SB, F32)]
    in_specs = [pl.BlockSpec((1, D_MODEL), _CONST),
                pl.BlockSpec((None, D_MODEL, D_IN), lambda i: (layer, 0, 0)),
                pl.BlockSpec((tm, LANES), tab),
                pl.BlockSpec((tm, LANES), tab),
                pl.BlockSpec((tm, LANES), tab)]
    return (in_specs, [pl.BlockSpec((tm, w), _ROW) for w, _ in widths],
            [jax.ShapeDtypeStruct((n, w), dt) for w, dt in widths])


def _in_projection(h, gain, w_bf16, layer, tables, seq, tm):
    n = h.shape[0]
    in_specs, out_specs, out_shapes = _inproj_specs(n, seq, tm, layer)
    return pl.pallas_call(
        _inproj_kernel,
        grid=(n // tm,),
        in_specs=[pl.BlockSpec((tm, D_MODEL), _ROW)] + in_specs,
        out_specs=out_specs,
        out_shape=out_shapes,
        compiler_params=_cparams(("parallel",)),
        name="in_projection",
    )(h, gain, w_bf16, *tables)


def _head_masks():
    lane = lax.broadcasted_iota(jnp.int32, (QB, LANES), 1)
    return lane < HEAD_DIM


assert DILATED_PATTERNS == tuple((QB * d, d) for d in (1, 4, 16))
SUPER = DILATED_PATTERNS[-1][0]
GROUP = 1
LOOKAHEAD = 1


def _dilated_kernel(q_ref, k_ref, v_ref, g_ref, o_ref, k1, v1, k4, v4, k16, v16, stage, m_sc, acc_sc, out_sc, q_sc):
    span = pl.program_id(2)
    seq = k_ref.shape[0]
    first_head = _head_masks()

    @pl.when(span == 0)
    def _():
        quarter = seq // 4

        def split(j, tiles_per_residue):
            bits = tiles_per_residue.bit_length() - 1
            assert tiles_per_residue == 1 << bits
            return lax.shift_right_logical(j, bits), lax.bitwise_and(j, tiles_per_residue - 1)

        for src_ref, c1, c4, c16 in ((k_ref, k1, k4, k16), (v_ref, v1, v4, v16)):
            def by_four(j, carry):
                dst = pl.ds(pl.multiple_of(j * QB, QB), QB)
                r, tile = split(j, quarter // QB)
                u0 = tile * QB
                regrouped = src_ref[pl.ds(4 * u0 + r, QB, stride=4), :]
                c1[dst, :] = src_ref[dst, :].astype(BF16)
                stage[dst, :] = regrouped
                c4[dst, :] = regrouped.astype(BF16)
                return carry
            lax.fori_loop(0, seq // QB, by_four, 0, unroll=4)

            def by_sixteen(j, carry):
                dst = pl.ds(pl.multiple_of(j * QB, QB), QB)
                r16, tile = split(j, seq // 16 // QB)
                high, low = split(r16, 4)
                src = pl.ds(low * quarter + high + 4 * QB * tile, QB, stride=4)
                c16[dst, :] = stage[src, :].astype(BF16)
                return carry
            lax.fori_loop(0, seq // QB, by_sixteen, 0, unroll=4)

    row = lax.broadcasted_iota(jnp.int32, (QB, 2 * QB), 0)
    col = lax.broadcasted_iota(jnp.int32, (QB, 2 * QB), 1) % QB

    def window_biases(query_pos):
        return jnp.where(col <= query_pos, 0.0, NEG_BIG), jnp.where(col >= query_pos, 0.0, NEG_BIG)

    quarter = QB // 4
    biases_in_order = window_biases(row)
    biases_by_four = window_biases(4 * (row % quarter) + row // quarter)
    head_sel = (jnp.where(first_head, 1.0, 0.0).astype(BF16), jnp.where(first_head, 0.0, 1.0).astype(BF16))

    groups = range(GROUP)

    def gather(ref, chunks, plane=None):
        parts = [ref[c, :] if plane is None else ref[plane, c, :] for c in chunks]
        return parts[0] if len(parts) == 1 else jnp.concatenate(parts, axis=0)

    def scatter(ref, chunks, value, plane=None):
        size = value.shape[0] // len(chunks)
        for n, c in enumerate(chunks):
            if plane is None:
                ref[c, :] = value[n * size:(n + 1) * size]
            else:
                ref[plane, c, :] = value[n * size:(n + 1) * size]

    for j in range(SUPER // QB):
        q_sc[j * QB:(j + 1) * QB, :] = q_ref[pl.ds((j % 4) * (4 * QB) + j // 4, QB, stride=4), :]

    def score_stage(kc, state_rows, cur_start, prev_valid, biases):
        qt = [gather(q_sc, state_rows[g]).astype(BF16) for g in groups]
        starts, masks = [], []
        for g in groups:
            cur = pl.multiple_of(cur_start[g], QB)
            prev = pl.multiple_of(jnp.maximum(cur_start[g] - QB, 0), QB)
            starts.append((cur, prev))
            known = isinstance(prev_valid[g], bool)
            assert not known or prev_valid[g]
            masks.append((biases[0], biases[1] if known else biases[1] + jnp.where(prev_valid[g], 0.0, NEG_BIG)))
        k_bd = [[jnp.concatenate([kblk * head_sel[0], kblk * head_sel[1]], axis=0)
                 for kblk in (kc[pl.ds(s, QB), :] for s in starts[g])] for g in groups]
        scores = [[lax.dot_general(qt[g], k_bd[g][t], (((1,), (1,)), ((), ())),
                                   preferred_element_type=F32) + masks[g][t] for t in range(2)]
                  for g in groups]
        return starts, scores

    def softmax_stage(vc, q_rows, state_rows, starts, scores, fresh, final):
        if not fresh:
            old = [(gather(m_sc, state_rows[g], 0), gather(m_sc, state_rows[g], 1),
                    jnp.concatenate([gather(acc_sc, state_rows[g], 0), gather(acc_sc, state_rows[g], 1)], axis=1))
                   for g in groups]
        top = [jnp.maximum(scores[g][0], scores[g][1]) for g in groups]
        m_new = []
        for g in groups:
            pair = []
            for h in range(2):
                m_h = jnp.max(top[g][:, h * QB:(h + 1) * QB], axis=-1, keepdims=True)
                if not fresh:
                    m_h = jnp.maximum(old[g][h], m_h)
                pair.append(jnp.broadcast_to(m_h, (QB, QB)))
            m_new.append(pair)
        p = [[jnp.exp2(scores[g][t] - jnp.concatenate(m_new[g], axis=1)).astype(BF16) for t in range(2)]
             for g in groups]
        v_ext = [[jnp.concatenate([jnp.concatenate([vblk * head_sel[0], head_sel[0]], axis=1),
                                   jnp.concatenate([vblk * head_sel[1], head_sel[1]], axis=1)], axis=0)
                  for vblk in (vc[pl.ds(s, QB), :] for s in starts[g])] for g in groups]
        pv = [jnp.dot(p[g][0], v_ext[g][0], preferred_element_type=F32)
              + jnp.dot(p[g][1], v_ext[g][1], preferred_element_type=F32) for g in groups]
        if not fresh:
            alpha = [jnp.exp2(jnp.where(first_head, old[g][0] - m_new[g][0], old[g][1] - m_new[g][1]))
                     for g in groups]
            pv = [pv[g] + jnp.concatenate([alpha[g], alpha[g]], axis=1) * old[g][2] for g in groups]
        for g in groups:
            if final:
                scatter(out_sc, q_rows[g], pv[g][:, :LANES] / pv[g][:, LANES:] * gather(g_ref, q_rows[g]))
            else:
                scatter(m_sc, state_rows[g], m_new[g][0], 0)
                scatter(m_sc, state_rows[g], m_new[g][1], 1)
                scatter(acc_sc, state_rows[g], pv[g][:, :LANES], 0)
                scatter(acc_sc, state_rows[g], pv[g][:, LANES:], 1)

    blocks = SUPER // QB
    span_quarter = SUPER // 4

    def pattern16(r):
        return (None, [pl.ds((r % 4) * span_quarter + r // 4, QB, stride=4)],
                r * (seq // 16) + span * QB, span > 0, biases_in_order)

    def pattern4(idx):
        r, ub = idx // 4, idx % 4
        return (None, [pl.ds(r * span_quarter + ub * QB, QB)],
                r * (seq // 4) + (span * 4 + ub) * QB, True if ub > 0 else span > 0, biases_in_order)

    def pattern1(i):
        return ([pl.ds(i * QB + c, quarter, stride=4) for c in range(4)],
                [pl.ds(c * span_quarter + i * quarter, quarter) for c in range(4)],
                (span * blocks + i) * QB, True if i > 0 else span > 0, biases_by_four)

    items = [(pattern, kc, vc, fresh, final, first)
             for pattern, kc, vc, fresh, final in ((pattern16, k16, v16, True, False),
                                                   (pattern4, k4, v4, False, False),
                                                   (pattern1, k1, v1, False, True))
             for first in range(0, blocks, GROUP)]

    def scores_of(item):
        pattern, kc, _, _, _, first = item
        q_rows, state_rows, cur_start, prev_valid, biases = zip(*[pattern(first + g) for g in groups])
        return (q_rows, state_rows) + score_stage(kc, state_rows, cur_start, prev_valid, biases[0])

    ahead = [scores_of(item) for item in items[:LOOKAHEAD]]
    for k, item in enumerate(items):
        q_rows, state_rows, starts, scores = ahead.pop(0)
        if k + LOOKAHEAD < len(items):
            ahead.append(scores_of(items[k + LOOKAHEAD]))
        softmax_stage(item[2], q_rows, state_rows, starts, scores, item[3], item[4])
    o_ref[...] = out_sc[...].astype(BF16)


def _dilated_attention(qa, ka, va, ga, batch, seq):
    n = qa.shape[0]
    pairs = W_DIL // LANES
    spans = seq // SUPER
    qmap = lambda b, p, i: (b * spans + i, p)
    kvmap = lambda b, p, i: (b, p)
    copy = pltpu.VMEM((seq, LANES), BF16)
    return pl.pallas_call(
        _dilated_kernel,
        grid=(batch, pairs, spans),
        in_specs=[pl.BlockSpec((SUPER, LANES), qmap),
                  pl.BlockSpec((seq, LANES), kvmap),
                  pl.BlockSpec((seq, LANES), kvmap),
                  pl.BlockSpec((SUPER, LANES), qmap)],
        out_specs=pl.BlockSpec((SUPER, LANES), qmap),
        out_shape=jax.ShapeDtypeStruct((n, W_DIL), BF16),
        scratch_shapes=([copy] * 6 + [pltpu.VMEM((seq, LANES), F32)] + [pltpu.VMEM((2, SUPER, LANES), F32)] * 2
                        + [pltpu.VMEM((SUPER, LANES), F32)] * 2),
        compiler_params=_cparams(("parallel", "parallel", "arbitrary")),
        name="dilated_attention",
    )(qa, ka, va, ga)


def _suffix_count_matrix(width):
    j = np.arange(width)[:, None]
    s = np.arange(width)[None, :]
    u = (j > s).astype(np.float32)
    z = np.zeros_like(u)
    return np.block([[u, z], [z, u]])


def _log_keep_and_beta(z2):
    log_keep = jnp.log(1.0 / (1.0 + jnp.exp2(-jnp.abs(z2)))) * math.log2(math.e) - jnp.maximum(z2, 0.0)
    return log_keep, z2 + log_keep


SB_GROUP = 16
SB_HALF = QB // 2


def _stick_kernel(q_ref, k_ref, v_ref, g_ref, u_ref, u_half_ref, o_ref):
    step = pl.program_id(2)
    first_head = _head_masks()
    head_sel = (jnp.where(first_head, 1.0, 0.0).astype(BF16), jnp.where(first_head, 0.0, 1.0).astype(BF16))
    suffix_count = {QB: u_ref[...], SB_HALF: u_half_ref[...]}
    row = lax.broadcasted_iota(jnp.int32, (QB, 2 * QB), 0)
    col = lax.broadcasted_iota(jnp.int32, (QB, 2 * QB), 1) % QB
    strictly_causal = col < row

    groups = range(SB_GROUP)

    def key_rows(g, offset, width):
        first = (step * SB_GROUP + g) * QB - offset
        exists = True if isinstance(offset, int) and g * QB >= offset else first >= 0
        return pl.multiple_of(jnp.maximum(first, 0), width), exists

    lane_half = lax.broadcasted_iota(jnp.int32, (SB_HALF, LANES), 1) < HEAD_DIM
    head_sel_half = (jnp.where(lane_half, 1.0, 0.0).astype(BF16), jnp.where(lane_half, 0.0, 1.0).astype(BF16))

    def block_diagonal(ref, g, offset, width):
        sel = head_sel if width == QB else head_sel_half
        blk = ref[pl.ds(key_rows(g, offset, width)[0], width), :]
        return jnp.concatenate([blk * sel[0], blk * sel[1]], axis=0)


    def scores(offset, width):
        return [lax.dot_general(q_ref[g * QB:(g + 1) * QB, :], block_diagonal(k_ref, g, offset, width),
                                (((1,), (1,)), ((), ())), preferred_element_type=F32)
                for g in groups]

    def logs(z, own_block):
        parts = []
        for g in groups:
            log_keep, log_beta = _log_keep_and_beta(z[g])
            if own_block:
                log_keep = jnp.where(strictly_causal, log_keep, 0.0)
            hi = log_keep.astype(BF16)
            parts.append((log_keep, log_beta, hi, (log_keep - hi.astype(F32)).astype(BF16)))
        return parts

    def fold(offset, width, parts, passed, acc, own_block):
        u = suffix_count[width]
        between, new_acc, new_passed, tops = [], [], [], []
        for g in groups:
            passed_g, exists = passed[g], key_rows(g, offset, width)[1]
            if exists is not True:
                passed_g = passed_g + jnp.where(exists, 0.0, NEG_BIG)
            between.append(passed_g + (jnp.dot(parts[g][2], u, preferred_element_type=F32)
                                       + jnp.dot(parts[g][3], u, preferred_element_type=F32)))
        for g in groups:
            w = jnp.exp2(parts[g][1] + between[g])
            if own_block:
                w = jnp.where(strictly_causal, w, 0.0)
            new_acc.append(acc[g] + jnp.dot(w.astype(BF16), block_diagonal(v_ref, g, offset, width),
                                            preferred_element_type=F32))
        for g in groups:
            total = between[g] + parts[g][0]
            total = (total[:, 0:1], total[:, width:width + 1])
            tops.append(jnp.maximum(total[0], total[1]))
            total = [jnp.broadcast_to(t, (QB, LANES)) for t in total]
            new_passed.append(jnp.concatenate(total, axis=1) if width == LANES
                              else jnp.where(first_head, total[0], total[1]))
        live = jnp.max(functools.reduce(jnp.maximum, tops)) > SB_DEAD_LOG2
        return tuple(new_passed), tuple(new_acc), live

    z_own, z_prev = scores(0, QB), scores(QB, QB)
    parts_own, parts_prev = logs(z_own, True), logs(z_prev, False)
    passed, acc, _ = fold(0, QB, parts_own, [jnp.zeros((QB, 2 * QB), F32)] * SB_GROUP,
                          [jnp.zeros((QB, LANES), F32)] * SB_GROUP, True)
    passed, acc, live = fold(QB, QB, parts_prev, passed, acc, False)

    last_block_start = (step * SB_GROUP + SB_GROUP - 1) * QB
    passed = tuple(jnp.where(first_head, p[:, :QB], p[:, QB:]) for p in passed)

    def cond(c):
        halves, live = c[:2]
        return jnp.logical_and(QB + SB_HALF * (halves + 1) <= last_block_start, live)

    def body(c):
        halves, _, passed, acc = c
        offset = QB + SB_HALF * (halves + 1)
        passed, acc, live = fold(offset, SB_HALF, logs(scores(offset, SB_HALF), False), passed, acc, False)
        return halves + 1, live, passed, acc

    acc = lax.while_loop(cond, body, (0, live, passed, acc))[-1]
    for g in groups:
        rows = slice(g * QB, (g + 1) * QB)
        o_ref[rows, :] = (acc[g] * g_ref[rows, :]).astype(BF16)


def _stick_breaking_attention(qc, kc, vc, gc, batch, seq):
    n = qc.shape[0]
    u, u_half = (jnp.asarray(_suffix_count_matrix(w), dtype=BF16) for w in (QB, SB_HALF))
    whole = lambda b, p, i: (0, 0)
    pairs = W_SB // LANES
    rows = SB_GROUP * QB
    steps = seq // rows
    qmap = lambda b, p, i: (b * steps + i, p)
    kvmap = lambda b, p, i: (b, p)
    return pl.pallas_call(
        _stick_kernel,
        grid=(batch, pairs, steps),
        in_specs=[pl.BlockSpec((rows, LANES), qmap),
                  pl.BlockSpec((seq, LANES), kvmap),
                  pl.BlockSpec((seq, LANES), kvmap),
                  pl.BlockSpec((rows, LANES), qmap),
                  pl.BlockSpec(u.shape, whole),
                  pl.BlockSpec(u_half.shape, whole)],
        out_specs=pl.BlockSpec((rows, LANES), qmap),
        out_shape=jax.ShapeDtypeStruct((n, W_SB), BF16),
        compiler_params=_cparams(("parallel", "parallel", "arbitrary")),
        name="stick_breaking_attention",
    )(qc, kc, vc, gc, u, u_half)


SUBLANES = 8
CONV_WIDTH = 4


def _rglru_kernel(x_ref, g_ref, cw_ref, cb_ref, wa_ref, ba_ref, wx_ref, bxg_ref, lam_ref, o_ref,
                  xbuf, a_sc, u_sc, h_sc, hcar):
    tc = x_ref.shape[0]

    @pl.when(pl.program_id(1) == 0)
    def _():
        xbuf[0:SUBLANES, :] = jnp.zeros((SUBLANES, W_LRU), F32)
        hcar[...] = jnp.zeros_like(hcar)

    x = x_ref[...]
    xbuf[SUBLANES:SUBLANES + tc, :] = x
    xc = x * cw_ref[CONV_WIDTH - 1:CONV_WIDTH, :] + cb_ref[...]
    for back in range(1, CONV_WIDTH):
        xc = xc + xbuf[SUBLANES - back:SUBLANES - back + tc, :] * cw_ref[CONV_WIDTH - 1 - back:CONV_WIDTH - back, :]
    xbuf[0:SUBLANES, :] = xbuf[tc:tc + SUBLANES, :]

    xcb = xc.astype(BF16)
    r = jax.nn.sigmoid(jnp.dot(xcb, wa_ref[...], preferred_element_type=F32) + ba_ref[...])
    i = jax.nn.sigmoid(jnp.dot(xcb, wx_ref[...], preferred_element_type=F32) + bxg_ref[...])
    lam = lam_ref[...]
    log_sig_lam = -(jnp.maximum(-lam, 0.0) + jnp.log1p(jnp.exp(-jnp.abs(lam))))
    log_a = LRU_C * r * log_sig_lam
    a = jnp.exp(log_a)
    a_sc[...] = a
    one_minus_a2 = jnp.tanh(-log_a) * (a * a + 1.0)
    root = jnp.where(one_minus_a2 > 0.0, one_minus_a2 * lax.rsqrt(one_minus_a2), 0.0)
    u_sc[...] = root * (i * xc)

    rows = lax.broadcasted_iota(jnp.int32, (SUBLANES, W_LRU), 0)

    def step(j, h_prev):
        start = pl.multiple_of(j * SUBLANES, SUBLANES)
        a = a_sc[pl.ds(start, SUBLANES), :]
        b = u_sc[pl.ds(start, SUBLANES), :]
        for shift in (1, 2, 4):
            keep = rows >= shift
            a_prev = jnp.where(keep, pltpu.roll(a, shift, 0), 1.0)
            b_prev = jnp.where(keep, pltpu.roll(b, shift, 0), 0.0)
            b = a * b_prev + b
            a = a * a_prev
        h = a * h_prev + b
        h_sc[pl.ds(start, SUBLANES), :] = h
        return jnp.broadcast_to(h[SUBLANES - 1:SUBLANES, :], (SUBLANES, W_LRU))

    hcar[...] = lax.fori_loop(0, tc // SUBLANES, step, hcar[...], unroll=16)
    o_ref[...] = (h_sc[...] * g_ref[...]).astype(BF16)


def _rg_lru(bx, bg, cw, cb, wa, ba, wx, bxg, lam, batch, seq, tc):
    n = bx.shape[0]
    nchunks = seq // tc
    row = lambda b, c: (b * nchunks + c, 0)
    const = lambda b, c: (0, 0)
    return pl.pallas_call(
        _rglru_kernel,
        grid=(batch, nchunks),
        in_specs=[pl.BlockSpec((tc, W_LRU), row),
                  pl.BlockSpec((tc, W_LRU), row),
                  pl.BlockSpec((CONV_WIDTH, W_LRU), const),
                  pl.BlockSpec((1, W_LRU), const),
                  pl.BlockSpec((W_LRU, W_LRU), const),
                  pl.BlockSpec((1, W_LRU), const),
                  pl.BlockSpec((W_LRU, W_LRU), const),
                  pl.BlockSpec((1, W_LRU), const),
                  pl.BlockSpec((1, W_LRU), const)],
        out_specs=pl.BlockSpec((tc, W_LRU), row),
        out_shape=jax.ShapeDtypeStruct((n, W_LRU), BF16),
        scratch_shapes=[pltpu.VMEM((tc + SUBLANES, W_LRU), F32),
                        pltpu.VMEM((tc, W_LRU), F32),
                        pltpu.VMEM((tc, W_LRU), F32),
                        pltpu.VMEM((tc, W_LRU), F32),
                        pltpu.VMEM((SUBLANES, W_LRU), F32)],
        compiler_params=_cparams(("arbitrary", "arbitrary")),
        name="rg_lru",
    )(bx, bg, cw, cb, wa, ba, wx, bxg, lam)


def _residual_out(x_ref, ya_ref, yb_ref, yc_ref, w_ref):
    acc = x_ref[...]
    acc = acc + jnp.dot(ya_ref[...], w_ref[0:W_DIL, :], preferred_element_type=F32)
    acc = acc + jnp.dot(yb_ref[...], w_ref[W_DIL:W_DIL + W_LRU, :], preferred_element_type=F32)
    return acc + jnp.dot(yc_ref[...], w_ref[W_DIL + W_LRU:, :], preferred_element_type=F32)


def _final_kernel(x_ref, ya_ref, yb_ref, yc_ref, w_ref, fg_ref, o_ref):
    acc = _residual_out(x_ref, ya_ref, yb_ref, yc_ref, w_ref)
    ms = jnp.mean(acc * acc, axis=-1, keepdims=True)
    o_ref[...] = (acc * lax.rsqrt(ms + EPS)) * fg_ref[...]


def _boundary_kernel(x_ref, ya_ref, yb_ref, yc_ref, w_ref, *refs):
    h_ref = refs[5]
    h = _residual_out(x_ref, ya_ref, yb_ref, yc_ref, w_ref)
    h_ref[...] = h
    _project_in(h, *refs[:5], *refs[6:])


def _outproj_specs(tm, layer):
    return [pl.BlockSpec((tm, D_MODEL), _ROW),
            pl.BlockSpec((tm, W_DIL), _ROW),
            pl.BlockSpec((tm, W_LRU), _ROW),
            pl.BlockSpec((tm, W_SB), _ROW),
            pl.BlockSpec((None, D_MODEL, D_MODEL), lambda i: (layer, 0, 0))]


def _final_projection(h, ya, yb, yc, w_bf16, layer, final_gain, tm):
    n = h.shape[0]
    return pl.pallas_call(
        _final_kernel,
        grid=(n // tm,),
        in_specs=_outproj_specs(tm, layer) + [pl.BlockSpec((1, D_MODEL), _CONST)],
        out_specs=pl.BlockSpec((tm, D_MODEL), _ROW),
        out_shape=jax.ShapeDtypeStruct((n, D_MODEL), F32),
        compiler_params=_cparams(("parallel",)),
        name="out_projection",
    )(h, ya, yb, yc, w_bf16, final_gain)


def _boundary_projection(h, ya, yb, yc, w_out_bf16, gain, w_in_bf16, layer, tables, seq, tm):
    n = h.shape[0]
    in_specs, out_specs, out_shapes = _inproj_specs(n, seq, tm, layer + 1)
    return pl.pallas_call(
        _boundary_kernel,
        grid=(n // tm,),
        in_specs=_outproj_specs(tm, layer) + in_specs,
        out_specs=[pl.BlockSpec((tm, D_MODEL), _ROW)] + out_specs,
        out_shape=[jax.ShapeDtypeStruct((n, D_MODEL), F32)] + out_shapes,
        compiler_params=_cparams(("parallel",)),
        name="out_in_projection",
    )(h, ya, yb, yc, w_out_bf16, gain, w_in_bf16, *tables)


def _rope_tables(seq):
    pos = np.arange(seq, dtype=np.float64)
    inv_freq = ROPE_THETA ** (-np.arange(0, ROPE_DIM, 2, dtype=np.float64) / ROPE_DIM)
    ang = pos[:, None] * inv_freq[None, :]
    cos, sin = np.cos(ang), np.sin(ang)
    half = ROPE_DIM // 2
    lane = np.arange(LANES) % HEAD_DIM
    freq = lane % half
    cos_l, sin_l = cos[:, freq], sin[:, freq]
    cos_t = np.where(lane < ROPE_DIM, cos_l, 1.0)
    s1_t = np.where(lane < half, -sin_l, 0.0)
    s2_t = np.where((lane >= half) & (lane < ROPE_DIM), sin_l, 0.0)
    return tuple(jnp.asarray(t, dtype=F32) for t in (cos_t, s1_t, s2_t))


def _block_diagonal(w):
    nb, d, _ = w.shape
    eye = jnp.eye(nb, dtype=w.dtype)
    return (w[:, :, None, :] * eye[:, None, :, None]).reshape(nb * d, nb * d)


def kernel(x, norm_gain, w_in, conv_w, conv_b, gate_a_w, gate_a_b, gate_x_w, gate_x_b, lru_lambda, w_out, final_gain):
    batch, seq, d = x.shape
    depth = w_in.shape[0]
    n = batch * seq
    tables = _rope_tables(seq)
    w_in_bf16, w_out_bf16 = w_in.astype(BF16), w_out.astype(BF16)
    tm = 512
    h = x.reshape(n, d)
    projected = _in_projection(h, norm_gain[0].reshape(1, d), w_in_bf16, 0, tables, seq, tm)
    for l in range(depth):
        qa, ka, va, ga, bx, bg, qc, kc, vc, gc = projected
        ya = _dilated_attention(qa, ka, va, ga, batch, seq)
        yb = _rg_lru(bx, bg, conv_w[l], conv_b[l].reshape(1, W_LRU),
                     _block_diagonal(gate_a_w[l]).astype(BF16), gate_a_b[l].reshape(1, W_LRU),
                     _block_diagonal(gate_x_w[l]).astype(BF16), gate_x_b[l].reshape(1, W_LRU),
                     lru_lambda[l].reshape(1, W_LRU), batch, seq, tc=2048)
        yc = _stick_breaking_attention(qc, kc, vc, gc, batch, seq)
        if l + 1 < depth:
            h, *projected = _boundary_projection(h, ya, yb, yc, w_out_bf16, norm_gain[l + 1].reshape(1, d),
                                                 w_in_bf16, l, tables, seq, tm)
        else:
            h = _final_projection(h, ya, yb, yc, w_out_bf16, l, final_gain.reshape(1, d), 4 * tm)
    return h.reshape(batch, seq, d)
```
